```python
import math
import jax, jax.numpy as jnp
from jax import lax
import numpy as np

D_MODEL = 2048
BATCH = 2
SEQ = 16384
DEPTH = 2

N_MEM = 256
HEAD_DIM = 128
MOBA_HEADS = 8
MOBA_BLOCK = 256
MOBA_TOPK = 3
MOBA_QBLOCK = 128
HGRN_HEADS = 4
HGRN_DK = 128
HGRN_DV = 128
HGRN_CHUNK = 64
MEM_HEADS = 4
D_FF = 5632
N_BRANCH = 3
RMS_EPS = 1e-6
NEG_INF = -1e30
F_MIN = 1e-20

MOBA_W = MOBA_HEADS * HEAD_DIM
HGRN_KW = HGRN_HEADS * HGRN_DK
HGRN_VW = HGRN_HEADS * HGRN_DV
MEM_W = MEM_HEADS * HEAD_DIM
N_IN = 3 * MOBA_W + 2 * HGRN_KW + 2 * HGRN_VW + MEM_W + N_BRANCH * D_MODEL

kernel_name = "hybrid_moba_hgrn2_memory_macaron"


def rms_norm(x, g):
    xf = x.astype(jnp.float32)
    r = lax.rsqrt(jnp.mean(xf * xf, axis=-1, keepdims=True) + RMS_EPS)
    return (xf * r).astype(x.dtype) * g


def swiglu_ffn(h, w1, w3, w2):
    return (jax.nn.silu(h @ w1) * (h @ w3)) @ w2


def alibi_slopes(n):
    return jnp.exp2(-8.0 * jnp.arange(1, n + 1, dtype=jnp.float32) / n)


def split_columns(p):
    sizes = (MOBA_W, MOBA_W, MOBA_W, HGRN_KW, HGRN_KW, HGRN_VW, HGRN_VW, MEM_W, N_BRANCH * D_MODEL)
    outs, off = [], 0
    for s in sizes:
        outs.append(p[..., off:off + s])
        off += s
    return outs


def moba_attention(q, k, v):
    B, T, H, Dh = q.shape
    BS, QB = MOBA_BLOCK, MOBA_QBLOCK
    Tp = -(-T // BS) * BS
    NB, NQ = Tp // BS, Tp // QB
    K = min(MOBA_TOPK, NB)
    scale = Dh ** -0.5

    def heads_major(a):
        a = jnp.pad(a, ((0, 0), (0, Tp - T), (0, 0), (0, 0)))
        return a.transpose(0, 2, 1, 3).reshape(B * H, Tp, Dh)

    qh, kh, vh = heads_major(q), heads_major(k), heads_major(v)
    kb = kh.reshape(B * H, NB, BS, Dh)
    vb = vh.reshape(B * H, NB, BS, Dh)
    k_mean = jnp.mean(kb.astype(jnp.float32), axis=2)
    t = jnp.arange(Tp, dtype=jnp.int32)
    n_past = t // BS
    past = jnp.arange(NB, dtype=jnp.int32)[None, :] < n_past[:, None]
    gate = jnp.einsum('ntd,nbd->ntb', qh.astype(jnp.float32), k_mean)
    gate = jnp.where(past[None], gate, NEG_INF)
    _, sel = lax.top_k(gate, K)
    sel_ok = (jnp.arange(K, dtype=jnp.int32)[None, :] < n_past[:, None]).reshape(NQ, QB, K)
    slopes = jnp.tile(alibi_slopes(H), B)
    offs_s = jnp.arange(BS, dtype=jnp.int32)
    offs_q = jnp.arange(QB, dtype=jnp.int32)

    def one_head(args):
        q_n, k_n, v_n, sel_n, m = args

        def one_block(args2):
            c, q_c, sel_c, ok_c = args2
            t_q = c * QB + offs_q
            own = (c * QB) // BS
            k_own, v_own = k_n[own], v_n[own]
            k_sel, v_sel = k_n[sel_c], v_n[sel_c]
            pos_own = own * BS + offs_s
            pos_sel = sel_c[..., None] * BS + offs_s
            d_own = (t_q[:, None] - pos_own[None, :]).astype(jnp.float32)
            d_sel = (t_q[:, None, None] - pos_sel).astype(jnp.float32)
            l_own = jnp.einsum('qd,sd->qs', q_c, k_own).astype(jnp.float32) * scale - m * d_own
            l_own = jnp.where(d_own >= 0, l_own, NEG_INF)
            l_sel = jnp.einsum('qd,qksd->qks', q_c, k_sel).astype(jnp.float32) * scale - m * d_sel
            l_sel = jnp.where(ok_c[..., None], l_sel, NEG_INF)
            p = jax.nn.softmax(jnp.concatenate([l_sel.reshape(QB, K * BS), l_own], axis=-1), axis=-1)
            p_sel = p[:, :K * BS].reshape(QB, K, BS).astype(v_n.dtype)
            p_own = p[:, K * BS:].astype(v_n.dtype)
            return jnp.einsum('qks,qksd->qd', p_sel, v_sel) + jnp.einsum('qs,sd->qd', p_own, v_own)

        return lax.map(one_block, (jnp.arange(NQ, dtype=jnp.int32), q_n, sel_n, sel_ok))

    out = lax.map(one_head, (qh.reshape(B * H, NQ, QB, Dh), kb, vb,
                             sel.reshape(B * H, NQ, QB, K), slopes))
    return out.reshape(B, H, Tp, Dh).transpose(0, 2, 1, 3)[:, :T]


def hgrn2_chunkwise(q, k, v, log_f):
    B, T, H, Dk = q.shape
    Dv = v.shape[-1]
    C = HGRN_CHUNK
    NC = T // C

    def to_chunks(a):
        return a.astype(jnp.float32).reshape(B, NC, C, H, a.shape[-1]).transpose(1, 0, 3, 2, 4)

    qc, kc, vc, gc = to_chunks(q), to_chunks(k), to_chunks(v), to_chunks(log_f)
    causal = jnp.tril(jnp.ones((C, C), dtype=bool))[:, :, None]

    def step(S, xs):
        qi, ki, vi, gi = xs
        G = jnp.cumsum(gi, axis=2)
        o_inter = jnp.einsum('bhcd,bhdv->bhcv', qi * jnp.exp(G), S)
        diff = G[:, :, :, None, :] - G[:, :, None, :, :]
        decay = jnp.where(causal, jnp.exp(jnp.minimum(diff, 0.0)), 0.0)
        A = jnp.einsum('bhtsd,bhsd->bhts', qi[:, :, :, None, :] * decay, ki)
        o_intra = jnp.einsum('bhts,bhsv->bhtv', A, vi)
        G_last = G[:, :, -1]
        S_new = jnp.exp(G_last)[..., None] * S + jnp.einsum(
            'bhsd,bhsv->bhdv', ki * jnp.exp(G_last[:, :, None] - G), vi)
        return S_new, o_inter + o_intra

    S0 = jnp.zeros((B, H, Dk, Dv), jnp.float32)
    _, o = lax.scan(step, S0, (qc, kc, vc, gc))
    return o.transpose(1, 0, 3, 2, 4).reshape(B, T, H, Dv)


def memory_attention(q, mk, mv):
    s = jnp.einsum('bthd,bmhd->bhtm', q, mk).astype(jnp.float32) * (q.shape[-1] ** -0.5)
    p = jax.nn.softmax(s, axis=-1).astype(mv.dtype)
    return jnp.einsum('bhtm,bmhd->bthd', p, mv)


def setup_inputs(seed: int = 0) -> dict:
    key = jax.random.key(seed)
    ks = jax.random.split(key, 24)
    f32 = jnp.float32

    def dense(k, shape, fan_in):
        return jax.random.normal(k, shape, f32) * (fan_in ** -0.5)

    def gain(k, shape):
        return 1.0 + 0.02 * jax.random.normal(k, shape, f32)

    L, D = DEPTH, D_MODEL
    return {
        'x': jax.random.normal(ks[0], (BATCH, SEQ, D), f32),
        'mem': jax.random.normal(ks[1], (BATCH, N_MEM, D), f32),
        'ffn1_norm': gain(ks[2], (L, D)),
        'ffn1_w1': dense(ks[3], (L, D, D_FF), D),
        'ffn1_w3': dense(ks[4], (L, D, D_FF), D),
        'ffn1_w2': dense(ks[5], (L, D_FF, D), D_FF),
        'mix_norm': gain(ks[6], (L, D)),
        'w_in': dense(ks[7], (L, D, N_IN), D),
        'hgrn_lb_logits': 0.5 * jax.random.normal(ks[8], (L, HGRN_KW), f32),
        'hgrn_out_norm': gain(ks[9], (L, HGRN_DV)),
        'mem_norm': gain(ks[10], (L, D)),
        'w_mem_kv': dense(ks[11], (L, D, 2 * MEM_W), D),
        'w_proj_moba': dense(ks[12], (L, MOBA_W, D), MOBA_W),
        'w_proj_hgrn': dense(ks[13], (L, HGRN_VW, D), HGRN_VW),
        'w_proj_mem': dense(ks[14], (L, MEM_W, D), MEM_W),
        'w_out': dense(ks[15], (L, D, D), D),
        'ffn2_norm': gain(ks[16], (L, D)),
        'ffn2_w1': dense(ks[17], (L, D, D_FF), D),
        'ffn2_w3': dense(ks[18], (L, D, D_FF), D),
        'ffn2_w2': dense(ks[19], (L, D_FF, D), D_FF),
        'final_norm': gain(ks[20], (D,)),
    }


def reference(x, mem, ffn1_norm, ffn1_w1, ffn1_w3, ffn1_w2, mix_norm, w_in, hgrn_lb_logits,
              hgrn_out_norm, mem_norm, w_mem_kv, w_proj_moba, w_proj_hgrn, w_proj_mem, w_out,
              ffn2_norm, ffn2_w1, ffn2_w3, ffn2_w2, final_norm):
    B, T, _ = x.shape
    M = mem.shape[1]
    p_lb = jax.nn.softmax(hgrn_lb_logits.astype(jnp.float32), axis=0)
    lower_bounds = jnp.cumsum(p_lb, axis=0) - p_lb[0:1]

    for l in range(DEPTH):
        x = x + 0.5 * swiglu_ffn(rms_norm(x, ffn1_norm[l]), ffn1_w1[l], ffn1_w3[l], ffn1_w2[l])

        h = rms_norm(x, mix_norm[l])
        qa, ka, va, qb, fb, ib, gb, qm, gates = split_columns(h @ w_in[l])

        hd = (B, T, MOBA_HEADS, HEAD_DIM)
        oa = moba_attention(qa.reshape(hd), ka.reshape(hd), va.reshape(hd)).reshape(B, T, MOBA_W)

        lb = lower_bounds[l]
        fb32 = fb.astype(jnp.float32)
        f_gate = lb + (1.0 - lb) * jax.nn.sigmoid(fb32)
        log_f = jnp.log(jnp.maximum(f_gate, F_MIN))
        k_b = (1.0 - lb) * jax.nn.sigmoid(-fb32)
        kd = (B, T, HGRN_HEADS, HGRN_DK)
        vd = (B, T, HGRN_HEADS, HGRN_DV)
        ob = hgrn2_chunkwise(jax.nn.silu(qb).reshape(kd), k_b.reshape(kd), ib.reshape(vd),
                             log_f.reshape(kd)).astype(x.dtype)
        ob = (rms_norm(ob, hgrn_out_norm[l]) * jax.nn.sigmoid(gb.reshape(vd))).reshape(B, T, HGRN_VW)

        kv = rms_norm(mem, mem_norm[l]) @ w_mem_kv[l]
        md = (B, M, MEM_HEADS, HEAD_DIM)
        om = memory_attention(qm.reshape(B, T, MEM_HEADS, HEAD_DIM),
                              kv[..., :MEM_W].reshape(md), kv[..., MEM_W:].reshape(md)).reshape(B, T, MEM_W)

        g = jax.nn.sigmoid(gates).reshape(B, T, N_BRANCH, D_MODEL)
        y = (g[:, :, 0] * (oa @ w_proj_moba[l])
             + g[:, :, 1] * (ob @ w_proj_hgrn[l])
             + g[:, :, 2] * (om @ w_proj_mem[l]))
        x = x + y @ w_out[l]

        x = x + 0.5 * swiglu_ffn(rms_norm(x, ffn2_norm[l]), ffn2_w1[l], ffn2_w3[l], ffn2_w2[l])

    return rms_norm(x, final_norm)
```

```python
import functools

import jax
import jax.numpy as jnp
from jax import lax
from jax.experimental import pallas as pl
from jax.experimental.pallas import tpu as pltpu

F32 = jnp.float32
BF16 = jnp.bfloat16

HEAD_DIM = 128
MOBA_HEADS = 8
MOBA_BLOCK = 256
MOBA_TOPK = 3
HGRN_HEADS = 4
MEM_HEADS = 4
RMS_EPS = 1e-6
NEG_INF = -1e30
F_MIN = 1e-20

MOBA_W = MOBA_HEADS * HEAD_DIM
HGRN_W = HGRN_HEADS * HEAD_DIM
MEM_W = MEM_HEADS * HEAD_DIM
MIX_W = 3 * MOBA_W + 4 * HGRN_W + MEM_W

VMEM_LIMIT_BYTES = 60 * 1024 * 1024

AUG_W = 2 * HEAD_DIM
AUG_SEL = HEAD_DIM
AUG_MAX_BLOCKS = 64
AUG_BIAS = AUG_SEL + AUG_MAX_BLOCKS
VT_ROWS = HEAD_DIM + 16

HGRN_SUB = 16


def _cparams(sem):
    return pltpu.CompilerParams(dimension_semantics=sem, vmem_limit_bytes=VMEM_LIMIT_BYTES)


def _dot(a, b):
    return jnp.dot(a, b, preferred_element_type=F32)


def _dot_t0(a, b):
    return lax.dot_general(a, b, (((0,), (0,)), ((), ())), preferred_element_type=F32)


def _rms(xf, g):
    r = lax.rsqrt(jnp.mean(xf * xf, axis=-1, keepdims=True) + RMS_EPS)
    return xf * r * g


def _pick(n, want):
    if n <= want:
        return n
    t = (want // 128) * 128
    while t >= 128:
        if n % t == 0:
            return t
        t -= 128
    return n


def _const_spec(shape):
    nd = len(shape)
    return pl.BlockSpec(shape, lambda *_: (0,) * nd)


def _ffn_kernel(x_ref, g_ref, w1_ref, w3_ref, w2_ref, fg_ref, o_ref, h_ref, *, final):
    j = pl.program_id(1)

    @pl.when(j == 0)
    def _():
        xf = x_ref[...]
        h_ref[...] = _rms(xf, g_ref[...]).astype(BF16)
        o_ref[...] = xf

    h = h_ref[...]
    u = _dot(h, w1_ref[...])
    v = _dot(h, w3_ref[...])
    a = (0.5 * u * jax.nn.sigmoid(u) * v).astype(BF16)
    o_ref[...] += _dot(a, w2_ref[...])

    if final:
        @pl.when(j == pl.num_programs(1) - 1)
        def _():
            o_ref[...] = _rms(o_ref[...], fg_ref[...])


def _ffn(x, g, w1, w3, w2, final_g, *, final):
    n, d = x.shape
    dff = w1.shape[1]
    tm = _pick(n, 512)
    tf = _pick(dff, 512)
    return pl.pallas_call(
        functools.partial(_ffn_kernel, final=final),
        grid=(n // tm, dff // tf),
        in_specs=[
            pl.BlockSpec((tm, d), lambda i, j: (i, 0)),
            pl.BlockSpec((1, d), lambda i, j: (0, 0)),
            pl.BlockSpec((d, tf), lambda i, j: (0, j)),
            pl.BlockSpec((d, tf), lambda i, j: (0, j)),
            pl.BlockSpec((tf, d), lambda i, j: (j, 0)),
            pl.BlockSpec((1, d), lambda i, j: (0, 0)),
        ],
        out_specs=pl.BlockSpec((tm, d), lambda i, j: (i, 0)),
        out_shape=jax.ShapeDtypeStruct((n, d), F32),
        scratch_shapes=[pltpu.VMEM((tm, d), BF16)],
        compiler_params=_cparams(("parallel", "arbitrary")),
    )(x, g.reshape(1, d), w1, w3, w2, final_g.reshape(1, d))


def _normproj_kernel(x_ref, g_ref, w_ref, s_ref, o_ref, h_ref):
    @pl.when(pl.program_id(1) == 0)
    def _():
        h_ref[...] = _rms(x_ref[...], g_ref[...]).astype(BF16)

    o_ref[...] = (_dot(h_ref[...], w_ref[...]) * s_ref[...]).astype(o_ref.dtype)


def _normproj(x, g, w, col_scale):
    n, d = x.shape
    nout = w.shape[1]
    tm = _pick(n, 512)
    tn = _pick(nout, 512)
    return pl.pallas_call(
        _normproj_kernel,
        grid=(n // tm, nout // tn),
        in_specs=[
            pl.BlockSpec((tm, d), lambda i, j: (i, 0)),
            pl.BlockSpec((1, d), lambda i, j: (0, 0)),
            pl.BlockSpec((d, tn), lambda i, j: (0, j)),
            pl.BlockSpec((1, tn), lambda i, j: (0, j)),
        ],
        out_specs=pl.BlockSpec((tm, tn), lambda i, j: (i, j)),
        out_shape=jax.ShapeDtypeStruct((n, nout), BF16),
        scratch_shapes=[pltpu.VMEM((tm, d), BF16)],
        compiler_params=_cparams(("parallel", "arbitrary")),
    )(x, g.reshape(1, d), w, col_scale.reshape(1, nout))


def _moba_prep_kernel(k_ref, v_ref, slope_ref, kaug_ref, vt_ref, kmean_ref):
    j = pl.program_id(1)
    bs = MOBA_BLOCK
    k = k_ref[...]
    kf = k.astype(F32)
    kmean_ref[0, pl.ds(j, 1), :] = jnp.mean(kf, axis=0, keepdims=True)

    slope = slope_ref[0, 0:1, 0:1]
    lane = lax.broadcasted_iota(jnp.int32, (bs, HEAD_DIM), 1)
    row = lax.broadcasted_iota(jnp.int32, (bs, HEAD_DIM), 0).astype(F32)
    jf = j.astype(F32)
    nb = AUG_MAX_BLOCKS
    extra = jnp.where(lane == j, 1.0, 0.0)
    extra = jnp.where(lane == nb, slope * row, extra)
    extra = jnp.where(lane == nb + 1, slope * (jf * bs), extra)
    extra = jnp.where((lane == nb + 2) | (lane == nb + 3), 1.0, extra)
    kaug_ref[0, 0, :, 0:HEAD_DIM] = k
    kaug_ref[0, 0, :, HEAD_DIM:AUG_W] = extra.astype(BF16)

    vt = v_ref[...].astype(F32).T
    vt_ref[0, 0, 0:HEAD_DIM, :] = vt.astype(BF16)
    orow = lax.broadcasted_iota(jnp.int32, (VT_ROWS - HEAD_DIM, bs), 0)
    vt_ref[0, 0, HEAD_DIM:VT_ROWS, :] = jnp.where(orow == 0, 1.0, 0.0).astype(BF16)


def _moba_prep(pa, slopes, batch, seq):
    nb = seq // MOBA_BLOCK
    bh = batch * MOBA_HEADS
    kcol = MOBA_W // HEAD_DIM
    vcol = 2 * MOBA_W // HEAD_DIM

    def kmap(n, j):
        return ((n // MOBA_HEADS) * nb + j, kcol + n % MOBA_HEADS)

    def vmap(n, j):
        return ((n // MOBA_HEADS) * nb + j, vcol + n % MOBA_HEADS)

    return pl.pallas_call(
        _moba_prep_kernel,
        grid=(bh, nb),
        in_specs=[
            pl.BlockSpec((MOBA_BLOCK, HEAD_DIM), kmap),
            pl.BlockSpec((MOBA_BLOCK, HEAD_DIM), vmap),
            pl.BlockSpec((1, 1, HEAD_DIM), lambda n, j: (n, 0, 0)),
        ],
        out_specs=[
            pl.BlockSpec((1, 1, MOBA_BLOCK, AUG_W), lambda n, j: (n, j, 0, 0)),
            pl.BlockSpec((1, 1, VT_ROWS, MOBA_BLOCK), lambda n, j: (n, j, 0, 0)),
            pl.BlockSpec((1, nb, HEAD_DIM), lambda n, j: (n, 0, 0)),
        ],
        out_shape=[
            jax.ShapeDtypeStruct((bh, nb, MOBA_BLOCK, AUG_W), BF16),
            jax.ShapeDtypeStruct((bh, nb, VT_ROWS, MOBA_BLOCK), BF16),
            jax.ShapeDtypeStruct((bh, nb, HEAD_DIM), F32),
        ],
        compiler_params=_cparams(("parallel", "arbitrary")),
    )(pa, pa, slopes)


def _moba_attn_kernel(q_ref, kaug_ref, vt_ref, kmean_ref, slope_ref, o_ref, qa_ref):
    i = pl.program_id(1)
    bs = MOBA_BLOCK
    nb = kmean_ref.shape[1]
    slope = slope_ref[0, 0:1, 0:1]

    qt = q_ref[...].astype(F32).T.astype(BF16)

    km = kmean_ref[0]
    km_hi = km.astype(BF16)
    km_lo = (km - km_hi.astype(F32)).astype(BF16)
    gate = _dot(km_hi, qt) + _dot(km_lo, qt)
    blk = lax.broadcasted_iota(jnp.int32, (nb, bs), 0)
    gate = jnp.where(blk < i, gate, NEG_INF)
    sel = jnp.zeros((nb, bs), jnp.bool_)
    for r in range(MOBA_TOPK):
        mx = jnp.max(gate, axis=0, keepdims=True)
        first = jnp.min(jnp.where(gate == mx, blk, nb), axis=0, keepdims=True)
        hit = blk == first
        sel = sel | (hit & (i > r))
        gate = jnp.where(hit, -jnp.inf, gate)

    lq = lax.broadcasted_iota(jnp.int32, (16, bs), 1).astype(F32)
    brow = lax.broadcasted_iota(jnp.int32, (16, bs), 0)
    i_f = i.astype(F32)
    bias = jnp.where(brow < 2, 1.0, 0.0)
    bias = jnp.where(brow == 2, -slope * (i_f * bs), bias)
    bias = jnp.where(brow == 3, -slope * lq, bias)
    qa_ref[0:HEAD_DIM, :] = qt
    qa_ref[AUG_SEL:AUG_SEL + nb, :] = jnp.where(sel, 0.0, NEG_INF).astype(BF16)
    if nb < AUG_MAX_BLOCKS:
        qa_ref[AUG_SEL + nb:AUG_BIAS, :] = jnp.zeros((AUG_MAX_BLOCKS - nb, bs), BF16)
    qa_ref[AUG_BIAS:AUG_BIAS + 16, :] = bias.astype(BF16)
    qa_ref[AUG_BIAS + 16:AUG_W, :] = jnp.zeros((AUG_W - AUG_BIAS - 16, bs), BF16)

    s = _dot(kaug_ref[0, i, :, 0:HEAD_DIM], qt)
    lk = lax.broadcasted_iota(jnp.int32, (bs, bs), 0)
    lqq = lax.broadcasted_iota(jnp.int32, (bs, bs), 1)
    dist = lqq - lk
    s = jnp.where(dist >= 0, s - slope * dist.astype(F32), NEG_INF)
    m0 = jnp.max(s, axis=0, keepdims=True)
    p = jnp.exp(s - m0).astype(BF16)
    acc0 = _dot(vt_ref[0, i], p)

    qa = qa_ref[...]

    def body(j, carry):
        m_run, acc = carry
        s = _dot(kaug_ref[0, j], qa)
        m_new = jnp.maximum(m_run, jnp.max(s, axis=0, keepdims=True))
        alpha = jnp.exp(m_run - m_new)
        p = jnp.exp(s - m_new).astype(BF16)
        acc = acc * alpha + _dot(vt_ref[0, j], p)
        return m_new, acc

    _, acc = lax.fori_loop(0, i, body, (m0, acc0))
    out_t = acc[0:HEAD_DIM] / acc[HEAD_DIM:HEAD_DIM + 1]
    o_ref[...] = out_t.T.astype(o_ref.dtype)


def _moba_attn(pa, kaug, vt, kmean, slopes, batch, seq):
    nb = seq // MOBA_BLOCK
    bh = batch * MOBA_HEADS

    def qmap(n, i):
        return ((n // MOBA_HEADS) * nb + i, n % MOBA_HEADS)

    return pl.pallas_call(
        _moba_attn_kernel,
        grid=(bh, nb),
        in_specs=[
            pl.BlockSpec((MOBA_BLOCK, HEAD_DIM), qmap),
            pl.BlockSpec((1, nb, MOBA_BLOCK, AUG_W), lambda n, i: (n, 0, 0, 0)),
            pl.BlockSpec((1, nb, VT_ROWS, MOBA_BLOCK), lambda n, i: (n, 0, 0, 0)),
            pl.BlockSpec((1, nb, HEAD_DIM), lambda n, i: (n, 0, 0)),
            pl.BlockSpec((1, 1, HEAD_DIM), lambda n, i: (n, 0, 0)),
        ],
        out_specs=pl.BlockSpec((MOBA_BLOCK, HEAD_DIM), qmap),
        out_shape=jax.ShapeDtypeStruct((batch * seq, MOBA_W), BF16),
        scratch_shapes=[pltpu.VMEM((AUG_W, MOBA_BLOCK), BF16)],
        compiler_params=_cparams(("parallel", "arbitrary")),
    )(pa, kaug, vt, kmean, slopes)


def _split3(x):
    hi = x.astype(BF16)
    r1 = x - hi.astype(F32)
    mid = r1.astype(BF16)
    lo = (r1 - mid.astype(F32)).astype(BF16)
    return hi, mid, lo


def _hgrn_kernel(qb_ref, fb_ref, ib_ref, gb_ref, lbl_ref, gn_ref, o_ref, s_ref, oi_ref, *, layer):
    tt = qb_ref.shape[0]
    sub = HGRN_SUB
    hd = HEAD_DIM

    @pl.when(pl.program_id(1) == 0)
    def _():
        s_ref[...] = jnp.zeros_like(s_ref)

    logits = lbl_ref[...]
    e = jnp.exp(logits - jnp.max(logits, axis=0, keepdims=True))
    pl_ = e / jnp.sum(e, axis=0, keepdims=True)
    lb = jnp.sum(pl_[0:layer + 1], axis=0, keepdims=True) - pl_[0:1]

    fb = fb_ref[...].astype(F32)
    t = jnp.exp(-jnp.abs(fb))
    r = 1.0 / (1.0 + t)
    sig_pos = jnp.where(fb >= 0, r, t * r)
    sig_neg = jnp.where(fb >= 0, t * r, r)
    f_gate = lb + (1.0 - lb) * sig_pos
    logf = jnp.log(jnp.maximum(f_gate, F_MIN))
    k_all = (1.0 - lb) * sig_neg
    qb = qb_ref[...].astype(F32)
    q_all = qb * jax.nn.sigmoid(qb)
    v_all = ib_ref[...].astype(F32)

    ri = lax.broadcasted_iota(jnp.int32, (tt, tt), 0)
    ci = lax.broadcasted_iota(jnp.int32, (tt, tt), 1)
    same = (ri // sub) == (ci // sub)
    tri = jnp.where(same & (ci <= ri), 1.0, 0.0).astype(BF16)
    hi, mid, lo = _split3(logf)
    gl = _dot(tri, hi) + _dot(tri, mid) + _dot(tri, lo)

    rowin = lax.broadcasted_iota(jnp.int32, (tt, hd), 0) % sub
    ones_hd = jnp.ones((hd, hd), BF16)
    ones_sub = jnp.ones((sub, hd), BF16)

    for h in range(HGRN_HEADS):
        cs = slice(h * hd, (h + 1) * hd)
        q = q_all[:, cs]
        k = k_all[:, cs]
        v = v_all[:, cs]
        g = gl[:, cs]
        lf = (hi[:, cs], mid[:, cs], lo[:, cs])

        def dbody(d, od, q=q, k=k, v=v, g=g):
            kr = pltpu.roll(k, d, 0)
            vr = pltpu.roll(v, d, 0)
            gr = pltpu.roll(g, d, 0)
            valid = rowin >= d
            prod = jnp.where(valid, q * kr * jnp.exp(g - gr), 0.0).astype(BF16)
            a = _dot(prod, ones_hd)
            return od + a * vr

        o_diag = lax.fori_loop(0, sub, dbody, jnp.zeros((tt, hd), F32))

        qd = (q * jnp.exp(g)).astype(BF16)
        vb = v.astype(BF16)
        st = s_ref[h]
        for c in range(tt // sub):
            rs = slice(c * sub, (c + 1) * sub)
            oi_ref[rs, cs] = _dot(qd[rs], st.astype(BF16))
            g_end = g[c * sub + sub - 1:c * sub + sub]
            kd = (k[rs] * jnp.exp(g_end - g[rs])).astype(BF16)
            tot = (_dot_t0(lf[0][rs], ones_sub) + _dot_t0(lf[1][rs], ones_sub)
                   + _dot_t0(lf[2][rs], ones_sub))
            st = jnp.exp(tot) * st + _dot_t0(kd, vb[rs])
        s_ref[h] = st

        o = oi_ref[:, cs] + o_diag
        rr = lax.rsqrt(jnp.mean(o * o, axis=-1, keepdims=True) + RMS_EPS)
        gate = jax.nn.sigmoid(gb_ref[:, cs].astype(F32))
        o_ref[:, cs] = (o * rr * gn_ref[...] * gate).astype(o_ref.dtype)


def _hgrn(pa, lb_logits, out_norm, layer, batch, seq):
    tt = min(256, seq)
    nt = seq // tt
    c0 = 3 * MOBA_W // HGRN_W
    nl = lb_logits.shape[0]

    def cmap(off):
        return lambda b, t: (b * nt + t, c0 + off)

    return pl.pallas_call(
        functools.partial(_hgrn_kernel, layer=layer),
        grid=(batch, nt),
        in_specs=[
            pl.BlockSpec((tt, HGRN_W), cmap(0)),
            pl.BlockSpec((tt, HGRN_W), cmap(1)),
            pl.BlockSpec((tt, HGRN_W), cmap(2)),
            pl.BlockSpec((tt, HGRN_W), cmap(3)),
            pl.BlockSpec((nl, HGRN_W), lambda b, t: (0, 0)),
            pl.BlockSpec((1, HEAD_DIM), lambda b, t: (0, 0)),
        ],
        out_specs=pl.BlockSpec((tt, HGRN_W), lambda b, t: (b * nt + t, 0)),
        out_shape=jax.ShapeDtypeStruct((batch * seq, HGRN_W), BF16),
        scratch_shapes=[pltpu.VMEM((HGRN_HEADS, HEAD_DIM, HEAD_DIM), F32),
                        pltpu.VMEM((tt, HGRN_W), F32)],
        compiler_params=_cparams(("parallel", "arbitrary")),
    )(pa, pa, pa, pa, lb_logits, out_norm.reshape(1, HEAD_DIM))


def _memkv_kernel(mem_ref, g_ref, w_ref, kt_ref, v_ref):
    h = _rms(mem_ref[0], g_ref[...]).astype(BF16)
    kv = _dot(h, w_ref[...])
    kt_ref[0] = kv[:, 0:MEM_W].T.astype(BF16)
    v_ref[0] = kv[:, MEM_W:].astype(BF16)


def _memkv(mem, g, w):
    b, m, d = mem.shape
    return pl.pallas_call(
        _memkv_kernel,
        grid=(b,),
        in_specs=[
            pl.BlockSpec((1, m, d), lambda i: (i, 0, 0)),
            pl.BlockSpec((1, d), lambda i: (0, 0)),
            pl.BlockSpec((d, 2 * MEM_W), lambda i: (0, 0)),
        ],
        out_specs=[
            pl.BlockSpec((1, MEM_W, m), lambda i: (i, 0, 0)),
            pl.BlockSpec((1, m, MEM_W), lambda i: (i, 0, 0)),
        ],
        out_shape=[
            jax.ShapeDtypeStruct((b, MEM_W, m), BF16),
            jax.ShapeDtypeStruct((b, m, MEM_W), BF16),
        ],
        compiler_params=_cparams(("parallel",)),
    )(mem, g.reshape(1, d), w)


def _merge_kernel(x_ref, oa_ref, ob_ref, qm_ref, g0_ref, g1_ref, g2_ref, mkt_ref, mv_ref,
                  wa_ref, wb_ref, wm_ref, wo_ref, o_ref, om_ref):
    scale = HEAD_DIM ** -0.5
    for h in range(MEM_HEADS):
        cs = slice(h * HEAD_DIM, (h + 1) * HEAD_DIM)
        s = _dot(qm_ref[:, cs], mkt_ref[0, cs, :]) * scale
        p = jnp.exp(s - jnp.max(s, axis=-1, keepdims=True))
        l = jnp.sum(p, axis=-1, keepdims=True)
        om_ref[:, cs] = (_dot(p.astype(BF16), mv_ref[0, :, cs]) / l).astype(BF16)

    y = jax.nn.sigmoid(g0_ref[...].astype(F32)) * _dot(oa_ref[...], wa_ref[...])
    y += jax.nn.sigmoid(g1_ref[...].astype(F32)) * _dot(ob_ref[...], wb_ref[...])
    y += jax.nn.sigmoid(g2_ref[...].astype(F32)) * _dot(om_ref[...], wm_ref[...])
    o_ref[...] = x_ref[...] + _dot(y.astype(BF16), wo_ref[...])


def _merge(x, oa, ob, pa, gates, mkt, mv, wa, wb, wm, wo, batch, seq):
    n, d = x.shape
    m = mv.shape[1]
    tm = _pick(seq, 256)
    per_b = seq // tm
    qcol = (3 * MOBA_W + 4 * HGRN_W) // MEM_W
    row = lambda i: (i, 0)
    return pl.pallas_call(
        _merge_kernel,
        grid=(n // tm,),
        in_specs=[
            pl.BlockSpec((tm, d), row),
            pl.BlockSpec((tm, MOBA_W), row),
            pl.BlockSpec((tm, HGRN_W), row),
            pl.BlockSpec((tm, MEM_W), lambda i: (i, qcol)),
            pl.BlockSpec((tm, d), lambda i: (i, 0)),
            pl.BlockSpec((tm, d), lambda i: (i, 1)),
            pl.BlockSpec((tm, d), lambda i: (i, 2)),
            pl.BlockSpec((1, MEM_W, m), lambda i: (i // per_b, 0, 0)),
            pl.BlockSpec((1, m, MEM_W), lambda i: (i // per_b, 0, 0)),
            _const_spec((MOBA_W, d)),
            _const_spec((HGRN_W, d)),
            _const_spec((MEM_W, d)),
            _const_spec((d, d)),
        ],
        out_specs=pl.BlockSpec((tm, d), row),
        out_shape=jax.ShapeDtypeStruct((n, d), F32),
        scratch_shapes=[pltpu.VMEM((tm, MEM_W), BF16)],
        compiler_params=_cparams(("parallel",)),
    )(x, oa, ob, pa, gates, gates, gates, mkt, mv, wa, wb, wm, wo)


def kernel(x, mem, ffn1_norm, ffn1_w1, ffn1_w3, ffn1_w2, mix_norm, w_in, hgrn_lb_logits,
           hgrn_out_norm, mem_norm, w_mem_kv, w_proj_moba, w_proj_hgrn, w_proj_mem, w_out,
           ffn2_norm, ffn2_w1, ffn2_w3, ffn2_w2, final_norm):
    batch, seq, d = x.shape
    depth = ffn1_w1.shape[0]
    assert seq % MOBA_BLOCK == 0 and seq // MOBA_BLOCK <= AUG_MAX_BLOCKS
    assert w_in.shape[-1] == MIX_W + 3 * d
    bf = lambda a: a.astype(BF16)

    hs = jnp.arange(1, MOBA_HEADS + 1, dtype=F32)
    slopes = jnp.tile(jnp.exp2(-8.0 * hs / MOBA_HEADS), batch)
    slopes = jnp.broadcast_to(slopes[:, None, None], (batch * MOBA_HEADS, 1, HEAD_DIM))
    mix_scale = jnp.concatenate([jnp.full((MOBA_W,), HEAD_DIM ** -0.5, F32),
                                 jnp.ones((MIX_W - MOBA_W,), F32)])
    gate_scale = jnp.ones((3 * d,), F32)

    xs = x.reshape(batch * seq, d)
    for l in range(depth):
        last = l == depth - 1
        xs = _ffn(xs, ffn1_norm[l], bf(ffn1_w1[l]), bf(ffn1_w3[l]), bf(ffn1_w2[l]),
                  final_norm, final=False)

        w_l = w_in[l]
        pa = _normproj(xs, mix_norm[l], bf(w_l[:, :MIX_W]), mix_scale)
        gates = _normproj(xs, mix_norm[l], bf(w_l[:, MIX_W:]), gate_scale)

        kaug, vt, kmean = _moba_prep(pa, slopes, batch, seq)
        oa = _moba_attn(pa, kaug, vt, kmean, slopes, batch, seq)
        ob = _hgrn(pa, hgrn_lb_logits, hgrn_out_norm[l], l, batch, seq)
        mkt, mv = _memkv(mem, mem_norm[l], bf(w_mem_kv[l]))
        xs = _merge(xs, oa, ob, pa, gates, mkt, mv, bf(w_proj_moba[l]), bf(w_proj_hgrn[l]),
                    bf(w_proj_mem[l]), bf(w_out[l]), batch, seq)

        xs = _ffn(xs, ffn2_norm[l], bf(ffn2_w1[l]), bf(ffn2_w3[l]), bf(ffn2_w2[l]),
                  final_norm, final=last)
    return xs.reshape(batch, seq, d)
```

```python
import functools

import jax
import jax.numpy as jnp
from jax import lax
from jax.experimental import pallas as pl
from jax.experimental.pallas import tpu as pltpu

F32 = jnp.float32
BF16 = jnp.bfloat16

HEAD_DIM = 128
MOBA_HEADS = 8
MOBA_BLOCK = 256
MOBA_TOPK = 3
HGRN_HEADS = 4
MEM_HEADS = 4
RMS_EPS = 1e-6
NEG_INF = -1e30
F_MIN = 1e-20

MOBA_W = MOBA_HEADS * HEAD_DIM
HGRN_W = HGRN_HEADS * HEAD_DIM
MEM_W = MEM_HEADS * HEAD_DIM
MIX_W = 3 * MOBA_W + 4 * HGRN_W + MEM_W

VMEM_LIMIT_BYTES = 60 * 1024 * 1024

AUG_W = 2 * HEAD_DIM
AUG_SEL = HEAD_DIM
AUG_MAX_BLOCKS = 64
AUG_BIAS = AUG_SEL + AUG_MAX_BLOCKS
VT_ROWS = HEAD_DIM + 16
MOBA_GROUP = 4
MOBA_QTILE = 2

HGRN_SUB = 16


def _cparams(sem):
    return pltpu.CompilerParams(dimension_semantics=sem, vmem_limit_bytes=VMEM_LIMIT_BYTES)


def _dot(a, b):
    return jnp.dot(a, b, preferred_element_type=F32)


def _dot_t0(a, b):
    return lax.dot_general(a, b, (((0,), (0,)), ((), ())), preferred_element_type=F32)


def _rms(xf, g):
    r = lax.rsqrt(jnp.mean(xf * xf, axis=-1, keepdims=True) + RMS_EPS)
    return xf * r * g


def _pick(n, want):
    if n <= want:
        return n
    t = (want // 128) * 128
    while t >= 128:
        if n % t == 0:
            return t
        t -= 128
    return n


def _const_spec(shape):
    nd = len(shape)
    return pl.BlockSpec(shape, lambda *_: (0,) * nd)


def _ffn_kernel(x_ref, g_ref, w1_ref, w3_ref, w2_ref, fg_ref, o_ref, h_ref, *, final):
    j = pl.program_id(1)

    @pl.when(j == 0)
    def _():
        xf = x_ref[...]
        h_ref[...] = _rms(xf, g_ref[...]).astype(BF16)
        o_ref[...] = xf

    h = h_ref[...]
    u = _dot(h, w1_ref[...])
    v = _dot(h, w3_ref[...])
    a = (0.5 * u * jax.nn.sigmoid(u) * v).astype(BF16)
    o_ref[...] += _dot(a, w2_ref[...])

    if final:
        @pl.when(j == pl.num_programs(1) - 1)
        def _():
            o_ref[...] = _rms(o_ref[...], fg_ref[...])


def _ffn(x, g, w1, w3, w2, final_g, *, final):
    n, d = x.shape
    dff = w1.shape[1]
    tm = _pick(n, 512)
    tf = _pick(dff, 512)
    return pl.pallas_call(
        functools.partial(_ffn_kernel, final=final),
        name="ffn_final" if final else "ffn",
        grid=(n // tm, dff // tf),
        in_specs=[
            pl.BlockSpec((tm, d), lambda i, j: (i, 0)),
            pl.BlockSpec((1, d), lambda i, j: (0, 0)),
            pl.BlockSpec((d, tf), lambda i, j: (0, j)),
            pl.BlockSpec((d, tf), lambda i, j: (0, j)),
            pl.BlockSpec((tf, d), lambda i, j: (j, 0)),
            pl.BlockSpec((1, d), lambda i, j: (0, 0)),
        ],
        out_specs=pl.BlockSpec((tm, d), lambda i, j: (i, 0)),
        out_shape=jax.ShapeDtypeStruct((n, d), F32),
        scratch_shapes=[pltpu.VMEM((tm, d), BF16)],
        compiler_params=_cparams(("parallel", "arbitrary")),
    )(x, g.reshape(1, d), w1, w3, w2, final_g.reshape(1, d))


def _normproj_kernel(x_ref, g_ref, w_ref, s_ref, o_ref, h_ref):
    @pl.when(pl.program_id(1) == 0)
    def _():
        h_ref[...] = _rms(x_ref[...], g_ref[...]).astype(BF16)

    o_ref[...] = (_dot(h_ref[...], w_ref[...]) * s_ref[...]).astype(o_ref.dtype)


def _normproj(x, g, w, col_scale):
    n, d = x.shape
    nout = w.shape[1]
    tm = _pick(n, 512)
    tn = _pick(nout, 512)
    return pl.pallas_call(
        _normproj_kernel,
        name="normproj",
        grid=(n // tm, nout // tn),
        in_specs=[
            pl.BlockSpec((tm, d), lambda i, j: (i, 0)),
            pl.BlockSpec((1, d), lambda i, j: (0, 0)),
            pl.BlockSpec((d, tn), lambda i, j: (0, j)),
            pl.BlockSpec((1, tn), lambda i, j: (0, j)),
        ],
        out_specs=pl.BlockSpec((tm, tn), lambda i, j: (i, j)),
        out_shape=jax.ShapeDtypeStruct((n, nout), BF16),
        scratch_shapes=[pltpu.VMEM((tm, d), BF16)],
        compiler_params=_cparams(("parallel", "arbitrary")),
    )(x, g.reshape(1, d), w, col_scale.reshape(1, nout))


def _moba_prep_kernel(k_ref, v_ref, slope_ref, kaug_ref, vt_ref, vtb_ref, kmean_ref):
    j = pl.program_id(1)
    bs = MOBA_BLOCK
    k = k_ref[...]
    kf = k.astype(F32)
    kmean_ref[0, pl.ds(j, 1), :] = jnp.mean(kf, axis=0, keepdims=True)

    slope = slope_ref[0, 0:1, 0:1]
    lane = lax.broadcasted_iota(jnp.int32, (bs, HEAD_DIM), 1)
    row = lax.broadcasted_iota(jnp.int32, (bs, HEAD_DIM), 0).astype(F32)
    jf = j.astype(F32)
    nb = AUG_MAX_BLOCKS
    extra = jnp.where(lane == j, 1.0, 0.0)
    extra = jnp.where(lane == nb, slope * row, extra)
    extra = jnp.where(lane == nb + 1, slope * (jf * bs), extra)
    extra = jnp.where((lane == nb + 2) | (lane == nb + 3), 1.0, extra)
    kaug_ref[0, 0, :, 0:HEAD_DIM] = k
    kaug_ref[0, 0, :, HEAD_DIM:AUG_W] = extra.astype(BF16)

    vt = v_ref[...].astype(F32).T
    orow = lax.broadcasted_iota(jnp.int32, (VT_ROWS - HEAD_DIM, bs), 0)
    ones_row = jnp.where(orow == 0, 1.0, 0.0).astype(BF16)
    for ref in (vt_ref, vtb_ref):
        ref[0, 0, 0:HEAD_DIM, :] = vt.astype(BF16)
        ref[0, 0, HEAD_DIM:VT_ROWS, :] = ones_row


def _moba_prep(pa, slopes, batch, seq):
    nb = seq // MOBA_BLOCK
    bh = batch * MOBA_HEADS
    kcol = MOBA_W // HEAD_DIM
    vcol = 2 * MOBA_W // HEAD_DIM

    def kmap(n, j):
        return ((n // MOBA_HEADS) * nb + j, kcol + n % MOBA_HEADS)

    def vmap(n, j):
        return ((n // MOBA_HEADS) * nb + j, vcol + n % MOBA_HEADS)

    return pl.pallas_call(
        _moba_prep_kernel,
        name="moba_prep",
        grid=(bh, nb),
        in_specs=[
            pl.BlockSpec((MOBA_BLOCK, HEAD_DIM), kmap),
            pl.BlockSpec((MOBA_BLOCK, HEAD_DIM), vmap),
            pl.BlockSpec((1, 1, HEAD_DIM), lambda n, j: (n, 0, 0)),
        ],
        out_specs=[
            pl.BlockSpec((1, 1, MOBA_BLOCK, AUG_W), lambda n, j: (n, j, 0, 0)),
            pl.BlockSpec((1, 1, VT_ROWS, MOBA_BLOCK),
                         lambda n, j: (n, j // MOBA_GROUP, 0, j % MOBA_GROUP)),
            pl.BlockSpec((1, 1, VT_ROWS, MOBA_BLOCK), lambda n, j: (n, j, 0, 0)),
            pl.BlockSpec((1, nb, HEAD_DIM), lambda n, j: (n, 0, 0)),
        ],
        out_shape=[
            jax.ShapeDtypeStruct((bh, nb, MOBA_BLOCK, AUG_W), BF16),
            jax.ShapeDtypeStruct((bh, nb // MOBA_GROUP, VT_ROWS, MOBA_GROUP * MOBA_BLOCK), BF16),
            jax.ShapeDtypeStruct((bh, nb, VT_ROWS, MOBA_BLOCK), BF16),
            jax.ShapeDtypeStruct((bh, nb, HEAD_DIM), F32),
        ],
        compiler_params=_cparams(("parallel", "arbitrary")),
    )(pa, pa, slopes)


def _moba_attn_kernel(q_ref, kaug_ref, vt_ref, vtb_ref, kmean_ref, slope_ref, o_ref, qa_ref,
                      sa_ref, sb_ref):
    t = pl.program_id(1)
    bs = MOBA_BLOCK
    tq = MOBA_QTILE * bs
    nb = kmean_ref.shape[1]
    slope = slope_ref[0, 0:1, 0:1]

    qt = q_ref[...].astype(F32).T.astype(BF16)

    lane = lax.broadcasted_iota(jnp.int32, (1, tq), 1)
    own = t * MOBA_QTILE + lane // bs
    lq = (lane % bs).astype(F32)

    km = kmean_ref[0]
    km_hi = km.astype(BF16)
    km_lo = (km - km_hi.astype(F32)).astype(BF16)
    gate = _dot(km_hi, qt) + _dot(km_lo, qt)
    blk = lax.broadcasted_iota(jnp.int32, (nb, tq), 0)
    gate = jnp.where(blk < own, gate, NEG_INF)
    sel = jnp.zeros((nb, tq), jnp.bool_)
    for r in range(MOBA_TOPK):
        mx = jnp.max(gate, axis=0, keepdims=True)
        first = jnp.min(jnp.where(gate == mx, blk, nb), axis=0, keepdims=True)
        hit = blk == first
        sel = sel | (hit & (own > r))
        gate = jnp.where(hit, -jnp.inf, gate)

    brow = lax.broadcasted_iota(jnp.int32, (16, tq), 0)
    bias = jnp.where(brow < 2, 1.0, 0.0)
    bias = jnp.where(brow == 2, -slope * (own * bs).astype(F32), bias)
    bias = jnp.where(brow == 3, -slope * lq, bias)
    qa_ref[0:HEAD_DIM, :] = qt
    qa_ref[AUG_SEL:AUG_SEL + nb, :] = jnp.where(sel, 0.0, NEG_INF).astype(BF16)
    if nb < AUG_MAX_BLOCKS:
        qa_ref[AUG_SEL + nb:AUG_BIAS, :] = jnp.zeros((AUG_MAX_BLOCKS - nb, tq), BF16)
    qa_ref[AUG_BIAS:AUG_BIAS + 16, :] = bias.astype(BF16)
    qa_ref[AUG_BIAS + 16:AUG_W, :] = jnp.zeros((AUG_W - AUG_BIAS - 16, tq), BF16)

    lk = lax.broadcasted_iota(jnp.int32, (bs, bs), 0)
    lqq = lax.broadcasted_iota(jnp.int32, (bs, bs), 1)
    dist = lqq - lk
    m_parts, acc_parts = [], []
    for u in range(MOBA_QTILE):
        i = t * MOBA_QTILE + u
        s = _dot(kaug_ref[0, i, :, 0:HEAD_DIM], qt[:, u * bs:(u + 1) * bs])
        s = jnp.where(dist >= 0, s - slope * dist.astype(F32), NEG_INF)
        m_u = jnp.max(s, axis=0, keepdims=True)
        p = jnp.exp(s - m_u).astype(BF16)
        acc_parts.append(_dot(vtb_ref[0, i], p))
        m_parts.append(m_u)
    m0 = jnp.concatenate(m_parts, axis=1)
    acc0 = jnp.concatenate(acc_parts, axis=1)

    qa = qa_ref[...]
    grp = MOBA_GROUP

    ngroups = nb // grp

    def scores(g):
        kb = kaug_ref[0, pl.ds(g * grp, grp)].reshape(grp * bs, AUG_W)
        return _dot(kb, qa)

    def attend(s, g, carry):
        m_run, acc = carry
        m_new = jnp.maximum(m_run, jnp.max(s, axis=0, keepdims=True))
        p = jnp.exp(s - m_new).astype(BF16)
        return m_new, acc * jnp.exp(m_run - m_new) + _dot(vt_ref[0, g], p)

    def body(h, carry):
        sb_ref[...] = scores(2 * h + 1)
        carry = attend(sa_ref[...], 2 * h, carry)
        sa_ref[...] = scores(jnp.minimum(2 * h + 2, ngroups - 1))
        return attend(sb_ref[...], 2 * h + 1, carry)

    n_past = t * MOBA_QTILE + MOBA_QTILE - 1
    npairs = (n_past + 2 * grp - 1) // (2 * grp)
    sa_ref[...] = scores(0)
    _, acc = lax.fori_loop(0, npairs, body, (m0, acc0))
    out_t = acc[0:HEAD_DIM] / acc[HEAD_DIM:HEAD_DIM + 1]
    o_ref[...] = out_t.T.astype(o_ref.dtype)


def _moba_attn(pa, kaug, vt, vtb, kmean, slopes, batch, seq):
    nb = seq // MOBA_BLOCK
    nt = nb // MOBA_QTILE
    tq = MOBA_QTILE * MOBA_BLOCK
    bh = batch * MOBA_HEADS

    def qmap(n, t):
        return ((n // MOBA_HEADS) * nt + t, n % MOBA_HEADS)

    return pl.pallas_call(
        _moba_attn_kernel,
        name="moba_attn",
        grid=(bh, nt),
        in_specs=[
            pl.BlockSpec((tq, HEAD_DIM), qmap),
            pl.BlockSpec((1, nb, MOBA_BLOCK, AUG_W), lambda n, t: (n, 0, 0, 0)),
            pl.BlockSpec((1, nb // MOBA_GROUP, VT_ROWS, MOBA_GROUP * MOBA_BLOCK),
                         lambda n, t: (n, 0, 0, 0)),
            pl.BlockSpec((1, nb, VT_ROWS, MOBA_BLOCK), lambda n, t: (n, 0, 0, 0)),
            pl.BlockSpec((1, nb, HEAD_DIM), lambda n, t: (n, 0, 0)),
            pl.BlockSpec((1, 1, HEAD_DIM), lambda n, t: (n, 0, 0)),
        ],
        out_specs=pl.BlockSpec((tq, HEAD_DIM), qmap),
        out_shape=jax.ShapeDtypeStruct((batch * seq, MOBA_W), BF16),
        scratch_shapes=[pltpu.VMEM((AUG_W, tq), BF16),
                        pltpu.VMEM((MOBA_GROUP * MOBA_BLOCK, tq), F32),
                        pltpu.VMEM((MOBA_GROUP * MOBA_BLOCK, tq), F32)],
        compiler_params=_cparams(("parallel", "arbitrary")),
    )(pa, kaug, vt, vtb, kmean, slopes)


def _split3(x):
    hi = x.astype(BF16)
    r1 = x - hi.astype(F32)
    mid = r1.astype(BF16)
    lo = (r1 - mid.astype(F32)).astype(BF16)
    return hi, mid, lo


def _hgrn_kernel(qb_ref, fb_ref, ib_ref, gb_ref, lbl_ref, gn_ref, o_ref, s_ref, oi_ref, st_ref,
                 *, layer):
    tt = qb_ref.shape[0]
    sub = HGRN_SUB
    hd = HEAD_DIM

    @pl.when(pl.program_id(1) == 0)
    def _():
        s_ref[...] = jnp.zeros_like(s_ref)

    logits = lbl_ref[...]
    e = jnp.exp(logits - jnp.max(logits, axis=0, keepdims=True))
    pl_ = e / jnp.sum(e, axis=0, keepdims=True)
    lb = jnp.sum(pl_[0:layer + 1], axis=0, keepdims=True) - pl_[0:1]

    fb = fb_ref[...].astype(F32)
    t = jnp.exp(-jnp.abs(fb))
    r = 1.0 / (1.0 + t)
    sig_pos = jnp.where(fb >= 0, r, t * r)
    sig_neg = jnp.where(fb >= 0, t * r, r)
    f_gate = lb + (1.0 - lb) * sig_pos
    logf = jnp.log(jnp.maximum(f_gate, F_MIN))
    k_all = (1.0 - lb) * sig_neg
    qb = qb_ref[...].astype(F32)
    q_all = qb * jax.nn.sigmoid(qb)
    v_all = ib_ref[...].astype(F32)

    ri = lax.broadcasted_iota(jnp.int32, (tt, tt), 0)
    ci = lax.broadcasted_iota(jnp.int32, (tt, tt), 1)
    same = (ri // sub) == (ci // sub)
    tri = jnp.where(same & (ci <= ri), 1.0, 0.0).astype(BF16)
    hi, mid, lo = _split3(logf)
    gl = _dot(tri, hi) + _dot(tri, mid) + _dot(tri, lo)

    half = sub // 2
    nsub = tt // sub
    nv = tt // half
    lane_sum = jnp.ones((hd, hd), BF16)
    row_in = lax.broadcasted_iota(jnp.int32, (nv, half, hd), 1)

    def pair_terms(qx, gx, kx, vx, causal):
        n = qx.shape[0]
        out = jnp.zeros_like(qx)
        for rho in range(half):
            kr = pltpu.roll(kx, rho, 1) if rho else kx
            vr = pltpu.roll(vx, rho, 1) if rho else vx
            gr = pltpu.roll(gx[1], rho, 1) if rho else gx[1]
            prod = qx * kr * jnp.exp(gx[0] - gr)
            if causal and rho:
                prod = jnp.where(row_in >= rho, prod, 0.0)
            a = _dot(prod.reshape(n * half, hd).astype(BF16), lane_sum)
            out = out + a.reshape(n, half, hd) * vr
        return out

    for h in range(HGRN_HEADS):
        cs = slice(h * hd, (h + 1) * hd)
        q = q_all[:, cs]
        k = k_all[:, cs]
        v = v_all[:, cs]
        g = gl[:, cs]

        q3, k3, v3, g3 = (a.reshape(nv, half, hd) for a in (q, k, v, g))
        od = pair_terms(q3, (g3, g3), k3, v3, True).reshape(nsub, 2, half, hd)
        q4, k4, v4, g4 = (a.reshape(nsub, 2, half, hd) for a in (q, k, v, g))
        oh = pair_terms(q4[:, 1], (g4[:, 1], g4[:, 0]), k4[:, 0], v4[:, 0], False)
        o_diag = jnp.stack([od[:, 0], od[:, 1] + oh], axis=1).reshape(tt, hd)

        gs = g.reshape(nsub, sub, hd)
        g_end = gs[:, sub - 1:sub, :]
        qd = (q * jnp.exp(g)).astype(BF16).reshape(nsub, sub, hd)
        kd = (k.reshape(nsub, sub, hd) * jnp.exp(g_end - gs)).astype(BF16)
        vb = v.astype(BF16).reshape(nsub, sub, hd)
        dec = jnp.exp(g_end)
        upd = [_dot_t0(vb[c], kd[c]) for c in range(nsub)]
        st = s_ref[h]
        for c in range(nsub):
            st_ref[c] = st.astype(BF16)
            st = st * dec[c] + upd[c]
        s_ref[h] = st
        for c in range(nsub):
            oi_ref[c * sub:(c + 1) * sub, cs] = lax.dot_general(
                qd[c], st_ref[c], (((1,), (1,)), ((), ())), preferred_element_type=F32)

        o = oi_ref[:, cs] + o_diag
        rr = lax.rsqrt(jnp.mean(o * o, axis=-1, keepdims=True) + RMS_EPS)
        gate = jax.nn.sigmoid(gb_ref[:, cs].astype(F32))
        o_ref[:, cs] = (o * rr * gn_ref[...] * gate).astype(o_ref.dtype)


def _hgrn(pa, lb_logits, out_norm, layer, batch, seq):
    tt = min(256, seq)
    nt = seq // tt
    c0 = 3 * MOBA_W // HGRN_W
    nl = lb_logits.shape[0]

    def cmap(off):
        return lambda b, t: (b * nt + t, c0 + off)

    return pl.pallas_call(
        functools.partial(_hgrn_kernel, layer=layer),
        name="hgrn",
        grid=(batch, nt),
        in_specs=[
            pl.BlockSpec((tt, HGRN_W), cmap(0)),
            pl.BlockSpec((tt, HGRN_W), cmap(1)),
            pl.BlockSpec((tt, HGRN_W), cmap(2)),
            pl.BlockSpec((tt, HGRN_W), cmap(3)),
            pl.BlockSpec((nl, HGRN_W), lambda b, t: (0, 0)),
            pl.BlockSpec((1, HEAD_DIM), lambda b, t: (0, 0)),
        ],
        out_specs=pl.BlockSpec((tt, HGRN_W), lambda b, t: (b * nt + t, 0)),
        out_shape=jax.ShapeDtypeStruct((batch * seq, HGRN_W), BF16),
        scratch_shapes=[pltpu.VMEM((HGRN_HEADS, HEAD_DIM, HEAD_DIM), F32),
                        pltpu.VMEM((tt, HGRN_W), F32),
                        pltpu.VMEM((tt // HGRN_SUB, HEAD_DIM, HEAD_DIM), BF16)],
        compiler_params=_cparams(("parallel", "arbitrary")),
    )(pa, pa, pa, pa, lb_logits, out_norm.reshape(1, HEAD_DIM))


def _memkv_kernel(mem_ref, g_ref, w_ref, kt_ref, v_ref):
    h = _rms(mem_ref[0], g_ref[...]).astype(BF16)
    kv = _dot(h, w_ref[...])
    kt_ref[0] = kv[:, 0:MEM_W].T.astype(BF16)
    v_ref[0] = kv[:, MEM_W:].astype(BF16)


def _memkv(mem, g, w):
    b, m, d = mem.shape
    return pl.pallas_call(
        _memkv_kernel,
        name="memkv",
        grid=(b,),
        in_specs=[
            pl.BlockSpec((1, m, d), lambda i: (i, 0, 0)),
            pl.BlockSpec((1, d), lambda i: (0, 0)),
            pl.BlockSpec((d, 2 * MEM_W), lambda i: (0, 0)),
        ],
        out_specs=[
            pl.BlockSpec((1, MEM_W, m), lambda i: (i, 0, 0)),
            pl.BlockSpec((1, m, MEM_W), lambda i: (i, 0, 0)),
        ],
        out_shape=[
            jax.ShapeDtypeStruct((b, MEM_W, m), BF16),
            jax.ShapeDtypeStruct((b, m, MEM_W), BF16),
        ],
        compiler_params=_cparams(("parallel",)),
    )(mem, g.reshape(1, d), w)


def _merge_kernel(x_ref, oa_ref, ob_ref, qm_ref, g0_ref, g1_ref, g2_ref, mkt_ref, mv_ref,
                  wa_ref, wb_ref, wm_ref, wo_ref, o_ref, om_ref):
    scale = HEAD_DIM ** -0.5
    for h in range(MEM_HEADS):
        cs = slice(h * HEAD_DIM, (h + 1) * HEAD_DIM)
        s = _dot(qm_ref[:, cs], mkt_ref[0, cs, :]) * scale
        p = jnp.exp(s - jnp.max(s, axis=-1, keepdims=True))
        l = jnp.sum(p, axis=-1, keepdims=True)
        om_ref[:, cs] = (_dot(p.astype(BF16), mv_ref[0, :, cs]) / l).astype(BF16)

    y = jax.nn.sigmoid(g0_ref[...].astype(F32)) * _dot(oa_ref[...], wa_ref[...])
    y += jax.nn.sigmoid(g1_ref[...].astype(F32)) * _dot(ob_ref[...], wb_ref[...])
    y += jax.nn.sigmoid(g2_ref[...].astype(F32)) * _dot(om_ref[...], wm_ref[...])
    o_ref[...] = x_ref[...] + _dot(y.astype(BF16), wo_ref[...])


def _merge(x, oa, ob, pa, gates, mkt, mv, wa, wb, wm, wo, batch, seq):
    n, d = x.shape
    m = mv.shape[1]
    tm = _pick(seq, 256)
    per_b = seq // tm
    qcol = (3 * MOBA_W + 4 * HGRN_W) // MEM_W
    row = lambda i: (i, 0)
    return pl.pallas_call(
        _merge_kernel,
        name="merge",
        grid=(n // tm,),
        in_specs=[
            pl.BlockSpec((tm, d), row),
            pl.BlockSpec((tm, MOBA_W), row),
            pl.BlockSpec((tm, HGRN_W), row),
            pl.BlockSpec((tm, MEM_W), lambda i: (i, qcol)),
            pl.BlockSpec((tm, d), lambda i: (i, 0)),
            pl.BlockSpec((tm, d), lambda i: (i, 1)),
            pl.BlockSpec((tm, d), lambda i: (i, 2)),
            pl.BlockSpec((1, MEM_W, m), lambda i: (i // per_b, 0, 0)),
            pl.BlockSpec((1, m, MEM_W), lambda i: (i // per_b, 0, 0)),
            _const_spec((MOBA_W, d)),
            _const_spec((HGRN_W, d)),
            _const_spec((MEM_W, d)),
            _const_spec((d, d)),
        ],
        out_specs=pl.BlockSpec((tm, d), row),
        out_shape=jax.ShapeDtypeStruct((n, d), F32),
        scratch_shapes=[pltpu.VMEM((tm, MEM_W), BF16)],
        compiler_params=_cparams(("parallel",)),
    )(x, oa, ob, pa, gates, gates, gates, mkt, mv, wa, wb, wm, wo)


def kernel(x, mem, ffn1_norm, ffn1_w1, ffn1_w3, ffn1_w2, mix_norm, w_in, hgrn_lb_logits,
           hgrn_out_norm, mem_norm, w_mem_kv, w_proj_moba, w_proj_hgrn, w_proj_mem, w_out,
           ffn2_norm, ffn2_w1, ffn2_w3, ffn2_w2, final_norm):
    batch, seq, d = x.shape
    depth = ffn1_w1.shape[0]
    assert seq % MOBA_BLOCK == 0 and seq // MOBA_BLOCK <= AUG_MAX_BLOCKS
    assert w_in.shape[-1] == MIX_W + 3 * d
    bf = lambda a: a.astype(BF16)

    hs = jnp.arange(1, MOBA_HEADS + 1, dtype=F32)
    slopes = jnp.tile(jnp.exp2(-8.0 * hs / MOBA_HEADS), batch)
    slopes = jnp.broadcast_to(slopes[:, None, None], (batch * MOBA_HEADS, 1, HEAD_DIM))
    mix_scale = jnp.concatenate([jnp.full((MOBA_W,), HEAD_DIM ** -0.5, F32),
                                 jnp.ones((MIX_W - MOBA_W,), F32)])
    gate_scale = jnp.ones((3 * d,), F32)

    xs = x.reshape(batch * seq, d)
    for l in range(depth):
        last = l == depth - 1
        xs = _ffn(xs, ffn1_norm[l], bf(ffn1_w1[l]), bf(ffn1_w3[l]), bf(ffn1_w2[l]),
                  final_norm, final=False)

        w_l = w_in[l]
        pa = _normproj(xs, mix_norm[l], bf(w_l[:, :MIX_W]), mix_scale)
        gates = _normproj(xs, mix_norm[l], bf(w_l[:, MIX_W:]), gate_scale)

        kaug, vt, vtb, kmean = _moba_prep(pa, slopes, batch, seq)
        oa = _moba_attn(pa, kaug, vt, vtb, kmean, slopes, batch, seq)
        ob = _hgrn(pa, hgrn_lb_logits, hgrn_out_norm[l], l, batch, seq)
        mkt, mv = _memkv(mem, mem_norm[l], bf(w_mem_kv[l]))
        xs = _merge(xs, oa, ob, pa, gates, mkt, mv, bf(w_proj_moba[l]), bf(w_proj_hgrn[l]),
                    bf(w_proj_mem[l]), bf(w_out[l]), batch, seq)

        xs = _ffn(xs, ffn2_norm[l], bf(ffn2_w1[l]), bf(ffn2_w3[l]), bf(ffn2_w2[l]),
                  final_norm, final=last)
    return xs.reshape(batch, seq, d)
```

```python
import functools

import jax
import jax.numpy as jnp
from jax import lax
from jax.experimental import pallas as pl
from jax.experimental.pallas import tpu as pltpu

F32 = jnp.float32
BF16 = jnp.bfloat16

HEAD_DIM = 128
MOBA_HEADS = 8
MOBA_BLOCK = 256
MOBA_TOPK = 3
HGRN_HEADS = 4
MEM_HEADS = 4
RMS_EPS = 1e-6
NEG_INF = -1e30
F_MIN = 1e-20

MOBA_W = MOBA_HEADS * HEAD_DIM
HGRN_W = HGRN_HEADS * HEAD_DIM
MEM_W = MEM_HEADS * HEAD_DIM
MIX_W = 3 * MOBA_W + 4 * HGRN_W + MEM_W

VMEM_LIMIT_BYTES = 60 * 1024 * 1024

AUG_W = 2 * HEAD_DIM
AUG_SEL = HEAD_DIM
AUG_MAX_BLOCKS = 64
AUG_BIAS = AUG_SEL + AUG_MAX_BLOCKS
VT_ROWS = HEAD_DIM + 16
MOBA_GROUP = 4
MOBA_QTILE = 4

HGRN_SUB = 16
FFN_OUT_CHUNK = 512


def _cparams(sem):
    return pltpu.CompilerParams(dimension_semantics=sem, vmem_limit_bytes=VMEM_LIMIT_BYTES)


def _dot(a, b):
    return jnp.dot(a, b, preferred_element_type=F32)


def _dot_t0(a, b):
    return lax.dot_general(a, b, (((0,), (0,)), ((), ())), preferred_element_type=F32)


def _rms(xf, g):
    r = lax.rsqrt(jnp.mean(xf * xf, axis=-1, keepdims=True) + RMS_EPS)
    return xf * r * g


def _pick(n, want):
    if n <= want:
        return n
    t = (want // 128) * 128
    while t >= 128:
        if n % t == 0:
            return t
        t -= 128
    return n


def _const_spec(shape):
    nd = len(shape)
    return pl.BlockSpec(shape, lambda *_: (0,) * nd)


def _ffn_kernel(x_ref, g_ref, w1_ref, w3_ref, w2_ref, fg_ref, o_ref, h_ref, *, final):
    j = pl.program_id(1)

    @pl.when(j == 0)
    def _():
        xf = x_ref[...]
        h_ref[...] = _rms(xf, g_ref[...]).astype(BF16)
        o_ref[...] = xf

    h = h_ref[...]
    u = _dot(h, w1_ref[...])
    v = _dot(h, w3_ref[...])
    a = (0.5 * u * jax.nn.sigmoid(u) * v).astype(BF16)
    tn = FFN_OUT_CHUNK if o_ref.shape[1] % FFN_OUT_CHUNK == 0 else o_ref.shape[1]
    for c in range(o_ref.shape[1] // tn):
        cs = slice(c * tn, (c + 1) * tn)
        o_ref[:, cs] += _dot(a, w2_ref[:, cs])

    if final:
        @pl.when(j == pl.num_programs(1) - 1)
        def _():
            o_ref[...] = _rms(o_ref[...], fg_ref[...])


def _ffn(x, g, w1, w3, w2, final_g, *, final):
    n, d = x.shape
    dff = w1.shape[1]
    tm = _pick(n, 1024)
    tf = _pick(dff, 512)
    return pl.pallas_call(
        functools.partial(_ffn_kernel, final=final),
        name="ffn_final" if final else "ffn",
        grid=(n // tm, dff // tf),
        in_specs=[
            pl.BlockSpec((tm, d), lambda i, j: (i, 0), pipeline_mode=pl.Buffered(1)),
            pl.BlockSpec((1, d), lambda i, j: (0, 0)),
            pl.BlockSpec((d, tf), lambda i, j: (0, j)),
            pl.BlockSpec((d, tf), lambda i, j: (0, j)),
            pl.BlockSpec((tf, d), lambda i, j: (j, 0)),
            pl.BlockSpec((1, d), lambda i, j: (0, 0)),
        ],
        out_specs=pl.BlockSpec((tm, d), lambda i, j: (i, 0)),
        out_shape=jax.ShapeDtypeStruct((n, d), F32),
        scratch_shapes=[pltpu.VMEM((tm, d), BF16)],
        compiler_params=_cparams(("parallel", "arbitrary")),
    )(x, g.reshape(1, d), w1, w3, w2, final_g.reshape(1, d))


def _normproj_kernel(x_ref, g_ref, w_ref, s_ref, o_ref, h_ref, *, tn):
    h_ref[...] = _rms(x_ref[...], g_ref[...]).astype(BF16)
    for c in range(w_ref.shape[1] // tn):
        cs = slice(c * tn, (c + 1) * tn)
        o_ref[:, cs] = (_dot(h_ref[...], w_ref[:, cs]) * s_ref[:, cs]).astype(o_ref.dtype)


def _normproj(x, g, w, col_scale):
    n, d = x.shape
    nout = w.shape[1]
    tm = _pick(n, 512)
    tn = _pick(nout, 512)
    return pl.pallas_call(
        functools.partial(_normproj_kernel, tn=tn),
        name="normproj",
        grid=(n // tm,),
        in_specs=[
            pl.BlockSpec((tm, d), lambda i: (i, 0)),
            pl.BlockSpec((1, d), lambda i: (0, 0)),
            pl.BlockSpec((d, nout), lambda i: (0, 0), pipeline_mode=pl.Buffered(1)),
            pl.BlockSpec((1, nout), lambda i: (0, 0)),
        ],
        out_specs=pl.BlockSpec((tm, nout), lambda i: (i, 0)),
        out_shape=jax.ShapeDtypeStruct((n, nout), BF16),
        scratch_shapes=[pltpu.VMEM((tm, d), BF16)],
        compiler_params=_cparams(("parallel",)),
    )(x, g.reshape(1, d), w, col_scale.reshape(1, nout))


def _moba_prep_kernel(k_ref, v_ref, slope_ref, kaug_ref, vt_ref, kmean_ref):
    j = pl.program_id(1)
    bs = MOBA_BLOCK
    k = k_ref[...]
    kf = k.astype(F32)
    kmean_ref[0, pl.ds(j, 1), :] = jnp.mean(kf, axis=0, keepdims=True)

    slope = slope_ref[0, 0:1, 0:1]
    lane = lax.broadcasted_iota(jnp.int32, (bs, HEAD_DIM), 1)
    row = lax.broadcasted_iota(jnp.int32, (bs, HEAD_DIM), 0).astype(F32)
    jf = j.astype(F32)
    nb = AUG_MAX_BLOCKS
    extra = jnp.where(lane == j, 1.0, 0.0)
    extra = jnp.where(lane == nb, slope * row, extra)
    extra = jnp.where(lane == nb + 1, slope * (jf * bs), extra)
    extra = jnp.where((lane == nb + 2) | (lane == nb + 3), 1.0, extra)
    kaug_ref[0, 0, :, 0:HEAD_DIM] = k
    kaug_ref[0, 0, :, HEAD_DIM:AUG_W] = extra.astype(BF16)

    vt = v_ref[...].astype(F32).T
    orow = lax.broadcasted_iota(jnp.int32, (VT_ROWS - HEAD_DIM, bs), 0)
    ones_row = jnp.where(orow == 0, 1.0, 0.0).astype(BF16)
    vt_ref[0, 0, 0:HEAD_DIM, :] = vt.astype(BF16)
    vt_ref[0, 0, HEAD_DIM:VT_ROWS, :] = ones_row


def _moba_prep(pa, slopes, batch, seq):
    nb = seq // MOBA_BLOCK
    bh = batch * MOBA_HEADS
    kcol = MOBA_W // HEAD_DIM
    vcol = 2 * MOBA_W // HEAD_DIM

    def kmap(n, j):
        return ((n // MOBA_HEADS) * nb + j, kcol + n % MOBA_HEADS)

    def vmap(n, j):
        return ((n // MOBA_HEADS) * nb + j, vcol + n % MOBA_HEADS)

    return pl.pallas_call(
        _moba_prep_kernel,
        name="moba_prep",
        grid=(bh, nb),
        in_specs=[
            pl.BlockSpec((MOBA_BLOCK, HEAD_DIM), kmap),
            pl.BlockSpec((MOBA_BLOCK, HEAD_DIM), vmap),
            pl.BlockSpec((1, 1, HEAD_DIM), lambda n, j: (n, 0, 0)),
        ],
        out_specs=[
            pl.BlockSpec((1, 1, MOBA_BLOCK, AUG_W), lambda n, j: (n, j, 0, 0)),
            pl.BlockSpec((1, 1, VT_ROWS, MOBA_BLOCK),
                         lambda n, j: (n, j // MOBA_GROUP, 0, j % MOBA_GROUP)),
            pl.BlockSpec((1, nb, HEAD_DIM), lambda n, j: (n, 0, 0)),
        ],
        out_shape=[
            jax.ShapeDtypeStruct((bh, nb, MOBA_BLOCK, AUG_W), BF16),
            jax.ShapeDtypeStruct((bh, nb // MOBA_GROUP, VT_ROWS, MOBA_GROUP * MOBA_BLOCK), BF16),
            jax.ShapeDtypeStruct((bh, nb, HEAD_DIM), F32),
        ],
        compiler_params=_cparams(("parallel", "arbitrary")),
    )(pa, pa, slopes)


def _moba_attn_kernel(q_ref, kaug_ref, vt_ref, kmean_ref, slope_ref, o_ref, qa_ref,
                      sa_ref, sb_ref, sc_ref):
    t = pl.program_id(1)
    bs = MOBA_BLOCK
    tq = MOBA_QTILE * bs
    nb = kmean_ref.shape[1]
    slope = slope_ref[0, 0:1, 0:1]

    qt = q_ref[...].astype(F32).T.astype(BF16)

    lane = lax.broadcasted_iota(jnp.int32, (1, tq), 1)
    own = t * MOBA_QTILE + lane // bs
    lq = (lane % bs).astype(F32)

    km = kmean_ref[0]
    km_hi = km.astype(BF16)
    km_lo = (km - km_hi.astype(F32)).astype(BF16)
    gate = _dot(km_hi, qt) + _dot(km_lo, qt)
    blk = lax.broadcasted_iota(jnp.int32, (nb, tq), 0)
    gate = jnp.where(blk < own, gate, NEG_INF)
    sel = jnp.zeros((nb, tq), jnp.bool_)
    for r in range(MOBA_TOPK):
        mx = jnp.max(gate, axis=0, keepdims=True)
        first = jnp.min(jnp.where(gate == mx, blk, nb), axis=0, keepdims=True)
        hit = blk == first
        sel = sel | (hit & (own > r))
        gate = jnp.where(hit, -jnp.inf, gate)
    sel = sel | (blk == own)

    brow = lax.broadcasted_iota(jnp.int32, (16, tq), 0)
    bias = jnp.where(brow < 2, 1.0, 0.0)
    bias = jnp.where(brow == 2, -slope * (own * bs).astype(F32), bias)
    bias = jnp.where(brow == 3, -slope * lq, bias)
    qa_ref[0:HEAD_DIM, :] = qt
    qa_ref[AUG_SEL:AUG_SEL + nb, :] = jnp.where(sel, 0.0, NEG_INF).astype(BF16)
    if nb < AUG_MAX_BLOCKS:
        qa_ref[AUG_SEL + nb:AUG_BIAS, :] = jnp.zeros((AUG_MAX_BLOCKS - nb, tq), BF16)
    qa_ref[AUG_BIAS:AUG_BIAS + 16, :] = bias.astype(BF16)
    qa_ref[AUG_BIAS + 16:AUG_W, :] = jnp.zeros((AUG_W - AUG_BIAS - 16, tq), BF16)

    qa = qa_ref[...]
    grp = MOBA_GROUP
    ngroups = nb // grp
    g_own = (t * MOBA_QTILE) // grp

    def scores(g):
        kb = kaug_ref[0, pl.ds(g * grp, grp)].reshape(grp * bs, AUG_W)
        return _dot(kb, qa)

    def attend(s, g, carry):
        m_run, acc = carry
        m_new = jnp.maximum(m_run, jnp.max(s, axis=0, keepdims=True))
        p = jnp.exp(s - m_new).astype(BF16)
        return m_new, acc * jnp.exp(m_run - m_new) + _dot(vt_ref[0, g], p)

    sa_ref[...] = scores(g_own)
    lk = lax.broadcasted_iota(jnp.int32, (bs, bs), 0)
    lqq = lax.broadcasted_iota(jnp.int32, (bs, bs), 1)
    for u in range(MOBA_QTILE):
        ro = pl.multiple_of((t * MOBA_QTILE + u - g_own * grp) * bs, bs)
        sq = sa_ref[pl.ds(ro, bs), u * bs:(u + 1) * bs]
        sa_ref[pl.ds(ro, bs), u * bs:(u + 1) * bs] = jnp.where(lk > lqq, NEG_INF, sq)

    def group_at(k):
        return jnp.minimum(k - 1, ngroups - 1)

    def body(h, carry):
        k = 3 * h
        sc_ref[...] = scores(group_at(k + 2))
        carry = attend(sa_ref[...], jnp.where(h == 0, g_own, k - 1), carry)
        sa_ref[...] = scores(group_at(k + 3))
        carry = attend(sb_ref[...], k, carry)
        sb_ref[...] = scores(group_at(k + 4))
        return attend(sc_ref[...], k + 1, carry)

    n_groups = g_own + 1
    n_steps = n_groups // 3
    sb_ref[...] = scores(group_at(1))
    carry = (jnp.full((1, tq), NEG_INF, F32), jnp.zeros((VT_ROWS, tq), F32))
    carry = lax.fori_loop(0, n_steps, body, carry)
    rest = n_groups - 3 * n_steps
    k = 3 * n_steps
    carry = lax.cond(rest >= 1,
                     lambda c: attend(sa_ref[...], jnp.where(k == 0, g_own, k - 1), c),
                     lambda c: c, carry)
    _, acc = lax.cond(rest == 2, lambda c: attend(sb_ref[...], k, c), lambda c: c, carry)
    out_t = acc[0:HEAD_DIM] / acc[HEAD_DIM:HEAD_DIM + 1]
    o_ref[...] = out_t.T.astype(o_ref.dtype)


def _moba_attn(pa, kaug, vt, kmean, slopes, batch, seq):
    nb = seq // MOBA_BLOCK
    nt = nb // MOBA_QTILE
    tq = MOBA_QTILE * MOBA_BLOCK
    bh = batch * MOBA_HEADS

    def qmap(n, t):
        return ((n // MOBA_HEADS) * nt + t, n % MOBA_HEADS)

    return pl.pallas_call(
        _moba_attn_kernel,
        name="moba_attn",
        grid=(bh, nt),
        in_specs=[
            pl.BlockSpec((tq, HEAD_DIM), qmap),
            pl.BlockSpec((1, nb, MOBA_BLOCK, AUG_W), lambda n, t: (n, 0, 0, 0)),
            pl.BlockSpec((1, nb // MOBA_GROUP, VT_ROWS, MOBA_GROUP * MOBA_BLOCK),
                         lambda n, t: (n, 0, 0, 0)),
            pl.BlockSpec((1, nb, HEAD_DIM), lambda n, t: (n, 0, 0)),
            pl.BlockSpec((1, 1, HEAD_DIM), lambda n, t: (n, 0, 0)),
        ],
        out_specs=pl.BlockSpec((tq, HEAD_DIM), qmap),
        out_shape=jax.ShapeDtypeStruct((batch * seq, MOBA_W), BF16),
        scratch_shapes=[pltpu.VMEM((AUG_W, tq), BF16),
                        pltpu.VMEM((MOBA_GROUP * MOBA_BLOCK, tq), F32),
                        pltpu.VMEM((MOBA_GROUP * MOBA_BLOCK, tq), F32),
                        pltpu.VMEM((MOBA_GROUP * MOBA_BLOCK, tq), F32)],
        compiler_params=_cparams(("parallel", "arbitrary")),
    )(pa, kaug, vt, kmean, slopes)


def _split3(x):
    hi = x.astype(BF16)
    r1 = x - hi.astype(F32)
    mid = r1.astype(BF16)
    lo = (r1 - mid.astype(F32)).astype(BF16)
    return hi, mid, lo


def _hgrn_kernel(qb_ref, fb_ref, ib_ref, gb_ref, lbl_ref, gn_ref, o_ref, s_ref, oi_ref, st_ref,
                 *, layer):
    tt = qb_ref.shape[0]
    sub = HGRN_SUB
    hd = HEAD_DIM

    @pl.when(pl.program_id(1) == 0)
    def _():
        s_ref[...] = jnp.zeros_like(s_ref)

    logits = lbl_ref[...]
    e = jnp.exp(logits - jnp.max(logits, axis=0, keepdims=True))
    pl_ = e / jnp.sum(e, axis=0, keepdims=True)
    lb = jnp.sum(pl_[0:layer + 1], axis=0, keepdims=True) - pl_[0:1]

    fb = fb_ref[...].astype(F32)
    t = jnp.exp(-jnp.abs(fb))
    r = 1.0 / (1.0 + t)
    sig_pos = jnp.where(fb >= 0, r, t * r)
    sig_neg = jnp.where(fb >= 0, t * r, r)
    f_gate = lb + (1.0 - lb) * sig_pos
    logf = jnp.log(jnp.maximum(f_gate, F_MIN))
    k_all = (1.0 - lb) * sig_neg
    qb = qb_ref[...].astype(F32)
    q_all = qb * jax.nn.sigmoid(qb)
    v_all = ib_ref[...].astype(F32)

    ri = lax.broadcasted_iota(jnp.int32, (tt, tt), 0)
    ci = lax.broadcasted_iota(jnp.int32, (tt, tt), 1)
    same = (ri // sub) == (ci // sub)
    tri = jnp.where(same & (ci <= ri), 1.0, 0.0).astype(BF16)
    hi, mid, lo = _split3(logf)
    gl = _dot(tri, hi) + _dot(tri, mid) + _dot(tri, lo)

    half = sub // 2
    nsub = tt // sub
    nv = tt // half
    lane_sum = jnp.ones((hd, hd), BF16)
    row_in = lax.broadcasted_iota(jnp.int32, (nv, half, hd), 1)

    def pair_terms(qx, gx, kx, vx, causal):
        n = qx.shape[0]
        out = jnp.zeros_like(qx)
        for rho in range(half):
            kr = pltpu.roll(kx, rho, 1) if rho else kx
            vr = pltpu.roll(vx, rho, 1) if rho else vx
            gr = pltpu.roll(gx[1], rho, 1) if rho else gx[1]
            prod = qx * kr * jnp.exp(gx[0] - gr)
            if causal and rho:
                prod = jnp.where(row_in >= rho, prod, 0.0)
            a = _dot(prod.reshape(n * half, hd).astype(BF16), lane_sum)
            out = out + a.reshape(n, half, hd) * vr
        return out

    for h in range(HGRN_HEADS):
        cs = slice(h * hd, (h + 1) * hd)
        q = q_all[:, cs]
        k = k_all[:, cs]
        v = v_all[:, cs]
        g = gl[:, cs]

        q3, k3, v3, g3 = (a.reshape(nv, half, hd) for a in (q, k, v, g))
        od = pair_terms(q3, (g3, g3), k3, v3, True).reshape(nsub, 2, half, hd)
        q4, k4, v4, g4 = (a.reshape(nsub, 2, half, hd) for a in (q, k, v, g))
        oh = pair_terms(q4[:, 1], (g4[:, 1], g4[:, 0]), k4[:, 0], v4[:, 0], False)
        o_diag = jnp.stack([od[:, 0], od[:, 1] + oh], axis=1).reshape(tt, hd)

        gs = g.reshape(nsub, sub, hd)
        g_end = gs[:, sub - 1:sub, :]
        qd = (q * jnp.exp(g)).astype(BF16).reshape(nsub, sub, hd)
        kd = (k.reshape(nsub, sub, hd) * jnp.exp(g_end - gs)).astype(BF16)
        vb = v.astype(BF16).reshape(nsub, sub, hd)
        dec = jnp.exp(g_end)
        upd = [_dot_t0(vb[c], kd[c]) for c in range(nsub)]
        st = s_ref[h]
        for c in range(nsub):
            st_ref[c] = st.astype(BF16)
            st = st * dec[c] + upd[c]
        s_ref[h] = st
        for c in range(nsub):
            oi_ref[c * sub:(c + 1) * sub, cs] = lax.dot_general(
                qd[c], st_ref[c], (((1,), (1,)), ((), ())), preferred_element_type=F32)

        o = oi_ref[:, cs] + o_diag
        rr = lax.rsqrt(jnp.mean(o * o, axis=-1, keepdims=True) + RMS_EPS)
        gate = jax.nn.sigmoid(gb_ref[:, cs].astype(F32))
        o_ref[:, cs] = (o * rr * gn_ref[...] * gate).astype(o_ref.dtype)


def _hgrn(pa, lb_logits, out_norm, layer, batch, seq):
    tt = min(256, seq)
    nt = seq // tt
    c0 = 3 * MOBA_W // HGRN_W
    nl = lb_logits.shape[0]

    def cmap(off):
        return lambda b, t: (b * nt + t, c0 + off)

    return pl.pallas_call(
        functools.partial(_hgrn_kernel, layer=layer),
        name="hgrn",
        grid=(batch, nt),
        in_specs=[
            pl.BlockSpec((tt, HGRN_W), cmap(0)),
            pl.BlockSpec((tt, HGRN_W), cmap(1)),
            pl.BlockSpec((tt, HGRN_W), cmap(2)),
            pl.BlockSpec((tt, HGRN_W), cmap(3)),
            pl.BlockSpec((nl, HGRN_W), lambda b, t: (0, 0)),
            pl.BlockSpec((1, HEAD_DIM), lambda b, t: (0, 0)),
        ],
        out_specs=pl.BlockSpec((tt, HGRN_W), lambda b, t: (b * nt + t, 0)),
        out_shape=jax.ShapeDtypeStruct((batch * seq, HGRN_W), BF16),
        scratch_shapes=[pltpu.VMEM((HGRN_HEADS, HEAD_DIM, HEAD_DIM), F32),
                        pltpu.VMEM((tt, HGRN_W), F32),
                        pltpu.VMEM((tt // HGRN_SUB, HEAD_DIM, HEAD_DIM), BF16)],
        compiler_params=_cparams(("parallel", "arbitrary")),
    )(pa, pa, pa, pa, lb_logits, out_norm.reshape(1, HEAD_DIM))


def _memkv_kernel(mem_ref, g_ref, w_ref, kt_ref, v_ref):
    h = _rms(mem_ref[0], g_ref[...]).astype(BF16)
    kv = _dot(h, w_ref[...])
    kt_ref[0] = kv[:, 0:MEM_W].T.astype(BF16)
    v_ref[0] = kv[:, MEM_W:].astype(BF16)


def _memkv(mem, g, w):
    b, m, d = mem.shape
    return pl.pallas_call(
        _memkv_kernel,
        name="memkv",
        grid=(b,),
        in_specs=[
            pl.BlockSpec((1, m, d), lambda i: (i, 0, 0)),
            pl.BlockSpec((1, d), lambda i: (0, 0)),
            pl.BlockSpec((d, 2 * MEM_W), lambda i: (0, 0)),
        ],
        out_specs=[
            pl.BlockSpec((1, MEM_W, m), lambda i: (i, 0, 0)),
            pl.BlockSpec((1, m, MEM_W), lambda i: (i, 0, 0)),
        ],
        out_shape=[
            jax.ShapeDtypeStruct((b, MEM_W, m), BF16),
            jax.ShapeDtypeStruct((b, m, MEM_W), BF16),
        ],
        compiler_params=_cparams(("parallel",)),
    )(mem, g.reshape(1, d), w)


def _merge_kernel(x_ref, oa_ref, ob_ref, qm_ref, g0_ref, g1_ref, g2_ref, mkt_ref, mv_ref,
                  wa_ref, wb_ref, wm_ref, wo_ref, o_ref, om_ref):
    scale = HEAD_DIM ** -0.5
    for h in range(MEM_HEADS):
        cs = slice(h * HEAD_DIM, (h + 1) * HEAD_DIM)
        s = _dot(qm_ref[:, cs], mkt_ref[0, cs, :]) * scale
        p = jnp.exp(s - jnp.max(s, axis=-1, keepdims=True))
        l = jnp.sum(p, axis=-1, keepdims=True)
        om_ref[:, cs] = (_dot(p.astype(BF16), mv_ref[0, :, cs]) / l).astype(BF16)

    y = jax.nn.sigmoid(g0_ref[...].astype(F32)) * _dot(oa_ref[...], wa_ref[...])
    y += jax.nn.sigmoid(g1_ref[...].astype(F32)) * _dot(ob_ref[...], wb_ref[...])
    y += jax.nn.sigmoid(g2_ref[...].astype(F32)) * _dot(om_ref[...], wm_ref[...])
    o_ref[...] = x_ref[...] + _dot(y.astype(BF16), wo_ref[...])


def _merge(x, oa, ob, pa, gates, mkt, mv, wa, wb, wm, wo, batch, seq):
    n, d = x.shape
    m = mv.shape[1]
    tm = _pick(seq, 256)
    per_b = seq // tm
    qcol = (3 * MOBA_W + 4 * HGRN_W) // MEM_W
    row = lambda i: (i, 0)
    return pl.pallas_call(
        _merge_kernel,
        name="merge",
        grid=(n // tm,),
        in_specs=[
            pl.BlockSpec((tm, d), row),
            pl.BlockSpec((tm, MOBA_W), row),
            pl.BlockSpec((tm, HGRN_W), row),
            pl.BlockSpec((tm, MEM_W), lambda i: (i, qcol)),
            pl.BlockSpec((tm, d), lambda i: (i, 0)),
            pl.BlockSpec((tm, d), lambda i: (i, 1)),
            pl.BlockSpec((tm, d), lambda i: (i, 2)),
            pl.BlockSpec((1, MEM_W, m), lambda i: (i // per_b, 0, 0)),
            pl.BlockSpec((1, m, MEM_W), lambda i: (i // per_b, 0, 0)),
            _const_spec((MOBA_W, d)),
            _const_spec((HGRN_W, d)),
            _const_spec((MEM_W, d)),
            _const_spec((d, d)),
        ],
        out_specs=pl.BlockSpec((tm, d), row),
        out_shape=jax.ShapeDtypeStruct((n, d), F32),
        scratch_shapes=[pltpu.VMEM((tm, MEM_W), BF16)],
        compiler_params=_cparams(("parallel",)),
    )(x, oa, ob, pa, gates, gates, gates, mkt, mv, wa, wb, wm, wo)


def kernel(x, mem, ffn1_norm, ffn1_w1, ffn1_w3, ffn1_w2, mix_norm, w_in, hgrn_lb_logits,
           hgrn_out_norm, mem_norm, w_mem_kv, w_proj_moba, w_proj_hgrn, w_proj_mem, w_out,
           ffn2_norm, ffn2_w1, ffn2_w3, ffn2_w2, final_norm):
    batch, seq, d = x.shape
    depth = ffn1_w1.shape[0]
    assert seq % MOBA_BLOCK == 0 and seq // MOBA_BLOCK <= AUG_MAX_BLOCKS
    assert w_in.shape[-1] == MIX_W + 3 * d
    bf = lambda a: a.astype(BF16)

    hs = jnp.arange(1, MOBA_HEADS + 1, dtype=F32)
    slopes = jnp.tile(jnp.exp2(-8.0 * hs / MOBA_HEADS), batch)
    slopes = jnp.broadcast_to(slopes[:, None, None], (batch * MOBA_HEADS, 1, HEAD_DIM))
    mix_scale = jnp.concatenate([jnp.full((MOBA_W,), HEAD_DIM ** -0.5, F32),
                                 jnp.ones((MIX_W - MOBA_W,), F32)])
    gate_scale = jnp.ones((3 * d,), F32)

    xs = x.reshape(batch * seq, d)
    for l in range(depth):
        last = l == depth - 1
        xs = _ffn(xs, ffn1_norm[l], bf(ffn1_w1[l]), bf(ffn1_w3[l]), bf(ffn1_w2[l]),
                  final_norm, final=False)

        w_l = w_in[l]
        pa = _normproj(xs, mix_norm[l], bf(w_l[:, :MIX_W]), mix_scale)
        gates = _normproj(xs, mix_norm[l], bf(w_l[:, MIX_W:]), gate_scale)

        kaug, vt, kmean = _moba_prep(pa, slopes, batch, seq)
        oa = _moba_attn(pa, kaug, vt, kmean, slopes, batch, seq)
        ob = _hgrn(pa, hgrn_lb_logits, hgrn_out_norm[l], l, batch, seq)
        mkt, mv = _memkv(mem, mem_norm[l], bf(w_mem_kv[l]))
        xs = _merge(xs, oa, ob, pa, gates, mkt, mv, bf(w_proj_moba[l]), bf(w_proj_hgrn[l]),
                    bf(w_proj_mem[l]), bf(w_out[l]), batch, seq)

        xs = _ffn(xs, ffn2_norm[l], bf(ffn2_w1[l]), bf(ffn2_w3[l]), bf(ffn2_w2[l]),
                  final_norm, final=last)
    return xs.reshape(batch, seq, d)
```

```python
import functools

import jax
import jax.numpy as jnp
from jax import lax
from jax.experimental import pallas as pl
from jax.experimental.pallas import tpu as pltpu
from jax.experimental.pallas import tpu_sc as plsc

F32 = jnp.float32
BF16 = jnp.bfloat16

HEAD_DIM = 128
MOBA_HEADS = 8
MOBA_BLOCK = 256
MOBA_TOPK = 3
HGRN_HEADS = 4
MEM_HEADS = 4
RMS_EPS = 1e-6
NEG_INF = -1e30
F_MIN = 1e-20

MOBA_W = MOBA_HEADS * HEAD_DIM
HGRN_W = HGRN_HEADS * HEAD_DIM
MEM_W = MEM_HEADS * HEAD_DIM
MIX_W = 3 * MOBA_W + 4 * HGRN_W + MEM_W

VMEM_LIMIT_BYTES = 60 * 1024 * 1024

AUG_W = 2 * HEAD_DIM
AUG_SEL = HEAD_DIM
AUG_MAX_BLOCKS = 64
AUG_BIAS = AUG_SEL + AUG_MAX_BLOCKS
VT_ROWS = HEAD_DIM + 16
MOBA_GROUP = 4
MOBA_QTILE = 4

HGRN_SUB = 16
FFN_OUT_CHUNK = 512
SLOT_TILE = 512
ROUTED_TILES_PER_STEP = 8
SC_WINDOW = 128


def _cparams(sem):
    return pltpu.CompilerParams(dimension_semantics=sem, vmem_limit_bytes=VMEM_LIMIT_BYTES)


def _dot(a, b):
    return jnp.dot(a, b, preferred_element_type=F32)


def _dot_t0(a, b):
    return lax.dot_general(a, b, (((0,), (0,)), ((), ())), preferred_element_type=F32)


def _rms(xf, g):
    r = lax.rsqrt(jnp.mean(xf * xf, axis=-1, keepdims=True) + RMS_EPS)
    return xf * r * g


def _pick(n, want):
    if n <= want:
        return n
    t = (want // 128) * 128
    while t >= 128:
        if n % t == 0:
            return t
        t -= 128
    return n


def _const_spec(shape):
    nd = len(shape)
    return pl.BlockSpec(shape, lambda *_: (0,) * nd)


def _ffn_kernel(x_ref, g_ref, w1_ref, w3_ref, w2_ref, fg_ref, o_ref, h_ref, *, final):
    j = pl.program_id(1)

    @pl.when(j == 0)
    def _():
        xf = x_ref[...]
        h_ref[...] = _rms(xf, g_ref[...]).astype(BF16)
        o_ref[...] = xf

    h = h_ref[...]
    u = _dot(h, w1_ref[...])
    v = _dot(h, w3_ref[...])
    a = (0.5 * u * jax.nn.sigmoid(u) * v).astype(BF16)
    tn = FFN_OUT_CHUNK if o_ref.shape[1] % FFN_OUT_CHUNK == 0 else o_ref.shape[1]
    for c in range(o_ref.shape[1] // tn):
        cs = slice(c * tn, (c + 1) * tn)
        o_ref[:, cs] += _dot(a, w2_ref[:, cs])

    if final:
        @pl.when(j == pl.num_programs(1) - 1)
        def _():
            o_ref[...] = _rms(o_ref[...], fg_ref[...])


def _ffn(x, g, w1, w3, w2, final_g, *, final):
    n, d = x.shape
    dff = w1.shape[1]
    tm = _pick(n, 1024)
    tf = _pick(dff, 512)
    return pl.pallas_call(
        functools.partial(_ffn_kernel, final=final),
        name="ffn_final" if final else "ffn",
        grid=(n // tm, dff // tf),
        in_specs=[
            pl.BlockSpec((tm, d), lambda i, j: (i, 0), pipeline_mode=pl.Buffered(1)),
            pl.BlockSpec((1, d), lambda i, j: (0, 0)),
            pl.BlockSpec((d, tf), lambda i, j: (0, j)),
            pl.BlockSpec((d, tf), lambda i, j: (0, j)),
            pl.BlockSpec((tf, d), lambda i, j: (j, 0)),
            pl.BlockSpec((1, d), lambda i, j: (0, 0)),
        ],
        out_specs=pl.BlockSpec((tm, d), lambda i, j: (i, 0)),
        out_shape=jax.ShapeDtypeStruct((n, d), F32),
        scratch_shapes=[pltpu.VMEM((tm, d), BF16)],
        compiler_params=_cparams(("parallel", "arbitrary")),
    )(x, g.reshape(1, d), w1, w3, w2, final_g.reshape(1, d))


def _normproj_kernel(x_ref, g_ref, w_ref, s_ref, o_ref, h_ref, *, tn):
    h_ref[...] = _rms(x_ref[...], g_ref[...]).astype(BF16)
    for c in range(w_ref.shape[1] // tn):
        cs = slice(c * tn, (c + 1) * tn)
        o_ref[:, cs] = (_dot(h_ref[...], w_ref[:, cs]) * s_ref[:, cs]).astype(o_ref.dtype)


def _normproj(x, g, w, col_scale):
    n, d = x.shape
    nout = w.shape[1]
    tm = _pick(n, 512)
    tn = _pick(nout, 512)
    return pl.pallas_call(
        functools.partial(_normproj_kernel, tn=tn),
        name="normproj",
        grid=(n // tm,),
        in_specs=[
            pl.BlockSpec((tm, d), lambda i: (i, 0)),
            pl.BlockSpec((1, d), lambda i: (0, 0)),
            pl.BlockSpec((d, nout), lambda i: (0, 0), pipeline_mode=pl.Buffered(1)),
            pl.BlockSpec((1, nout), lambda i: (0, 0)),
        ],
        out_specs=pl.BlockSpec((tm, nout), lambda i: (i, 0)),
        out_shape=jax.ShapeDtypeStruct((n, nout), BF16),
        scratch_shapes=[pltpu.VMEM((tm, d), BF16)],
        compiler_params=_cparams(("parallel",)),
    )(x, g.reshape(1, d), w, col_scale.reshape(1, nout))


def _moba_prep_kernel(k_ref, v_ref, slope_ref, kaug_ref, vt_ref, vtb_ref, kmean_ref):
    j = pl.program_id(1)
    bs = MOBA_BLOCK
    k = k_ref[...]
    kf = k.astype(F32)
    kmean_ref[0, pl.ds(j, 1), :] = jnp.mean(kf, axis=0, keepdims=True)

    slope = slope_ref[0, 0:1, 0:1]
    lane = lax.broadcasted_iota(jnp.int32, (bs, HEAD_DIM), 1)
    row = lax.broadcasted_iota(jnp.int32, (bs, HEAD_DIM), 0).astype(F32)
    jf = j.astype(F32)
    nb = AUG_MAX_BLOCKS
    extra = jnp.where(lane == j, 1.0, 0.0)
    extra = jnp.where(lane == nb, slope * row, extra)
    extra = jnp.where(lane == nb + 1, slope * (jf * bs), extra)
    extra = jnp.where((lane == nb + 2) | (lane == nb + 3), 1.0, extra)
    kaug_ref[0, 0, :, 0:HEAD_DIM] = k
    kaug_ref[0, 0, :, HEAD_DIM:AUG_W] = extra.astype(BF16)

    vt = v_ref[...].astype(F32).T
    orow = lax.broadcasted_iota(jnp.int32, (VT_ROWS - HEAD_DIM, bs), 0)
    ones_row = jnp.where(orow == 0, 1.0, 0.0).astype(BF16)
    for ref in (vt_ref, vtb_ref):
        ref[0, 0, 0:HEAD_DIM, :] = vt.astype(BF16)
        ref[0, 0, HEAD_DIM:VT_ROWS, :] = ones_row


def _moba_prep(pa, slopes, batch, seq):
    nb = seq // MOBA_BLOCK
    bh = batch * MOBA_HEADS
    kcol = MOBA_W // HEAD_DIM
    vcol = 2 * MOBA_W // HEAD_DIM

    def kmap(n, j):
        return ((n // MOBA_HEADS) * nb + j, kcol + n % MOBA_HEADS)

    def vmap(n, j):
        return ((n // MOBA_HEADS) * nb + j, vcol + n % MOBA_HEADS)

    return pl.pallas_call(
        _moba_prep_kernel,
        name="moba_prep",
        grid=(bh, nb),
        in_specs=[
            pl.BlockSpec((MOBA_BLOCK, HEAD_DIM), kmap),
            pl.BlockSpec((MOBA_BLOCK, HEAD_DIM), vmap),
            pl.BlockSpec((1, 1, HEAD_DIM), lambda n, j: (n, 0, 0)),
        ],
        out_specs=[
            pl.BlockSpec((1, 1, MOBA_BLOCK, AUG_W), lambda n, j: (n, j, 0, 0)),
            pl.BlockSpec((1, 1, VT_ROWS, MOBA_BLOCK),
                         lambda n, j: (n, j // MOBA_GROUP, 0, j % MOBA_GROUP)),
            pl.BlockSpec((1, 1, VT_ROWS, MOBA_BLOCK), lambda n, j: (n, j, 0, 0)),
            pl.BlockSpec((1, nb, HEAD_DIM), lambda n, j: (n, 0, 0)),
        ],
        out_shape=[
            jax.ShapeDtypeStruct((bh, nb, MOBA_BLOCK, AUG_W), BF16),
            jax.ShapeDtypeStruct((bh, nb // MOBA_GROUP, VT_ROWS, MOBA_GROUP * MOBA_BLOCK), BF16),
            jax.ShapeDtypeStruct((bh, nb, VT_ROWS, MOBA_BLOCK), BF16),
            jax.ShapeDtypeStruct((bh, nb, HEAD_DIM), F32),
        ],
        compiler_params=_cparams(("parallel", "arbitrary")),
    )(pa, pa, slopes)


def _moba_attn_kernel(q_ref, kaug_ref, vt_ref, kmean_ref, slope_ref, o_ref, qa_ref,
                      sa_ref, sb_ref, sc_ref):
    t = pl.program_id(1)
    bs = MOBA_BLOCK
    tq = MOBA_QTILE * bs
    nb = kmean_ref.shape[1]
    slope = slope_ref[0, 0:1, 0:1]

    qt = q_ref[...].astype(F32).T.astype(BF16)

    lane = lax.broadcasted_iota(jnp.int32, (1, tq), 1)
    own = t * MOBA_QTILE + lane // bs
    lq = (lane % bs).astype(F32)

    km = kmean_ref[0]
    km_hi = km.astype(BF16)
    km_lo = (km - km_hi.astype(F32)).astype(BF16)
    gate = _dot(km_hi, qt) + _dot(km_lo, qt)
    blk = lax.broadcasted_iota(jnp.int32, (nb, tq), 0)
    gate = jnp.where(blk < own, gate, NEG_INF)
    sel = jnp.zeros((nb, tq), jnp.bool_)
    for r in range(MOBA_TOPK):
        mx = jnp.max(gate, axis=0, keepdims=True)
        first = jnp.min(jnp.where(gate == mx, blk, nb), axis=0, keepdims=True)
        hit = blk == first
        sel = sel | (hit & (own > r))
        gate = jnp.where(hit, -jnp.inf, gate)
    sel = sel | (blk == own)

    brow = lax.broadcasted_iota(jnp.int32, (16, tq), 0)
    bias = jnp.where(brow < 2, 1.0, 0.0)
    bias = jnp.where(brow == 2, -slope * (own * bs).astype(F32), bias)
    bias = jnp.where(brow == 3, -slope * lq, bias)
    qa_ref[0:HEAD_DIM, :] = qt
    qa_ref[AUG_SEL:AUG_SEL + nb, :] = jnp.where(sel, 0.0, NEG_INF).astype(BF16)
    if nb < AUG_MAX_BLOCKS:
        qa_ref[AUG_SEL + nb:AUG_BIAS, :] = jnp.zeros((AUG_MAX_BLOCKS - nb, tq), BF16)
    qa_ref[AUG_BIAS:AUG_BIAS + 16, :] = bias.astype(BF16)
    qa_ref[AUG_BIAS + 16:AUG_W, :] = jnp.zeros((AUG_W - AUG_BIAS - 16, tq), BF16)

    qa = qa_ref[...]
    grp = MOBA_GROUP
    ngroups = nb // grp
    g_own = (t * MOBA_QTILE) // grp

    def scores(g):
        kb = kaug_ref[0, pl.ds(g * grp, grp)].reshape(grp * bs, AUG_W)
        return _dot(kb, qa)

    def attend(s, g, carry):
        m_run, acc = carry
        m_new = jnp.maximum(m_run, jnp.max(s, axis=0, keepdims=True))
        p = jnp.exp(s - m_new).astype(BF16)
        return m_new, acc * jnp.exp(m_run - m_new) + _dot(vt_ref[0, g], p)

    sa_ref[...] = scores(g_own)
    lk = lax.broadcasted_iota(jnp.int32, (bs, bs), 0)
    lqq = lax.broadcasted_iota(jnp.int32, (bs, bs), 1)
    for u in range(MOBA_QTILE):
        ro = pl.multiple_of((t * MOBA_QTILE + u - g_own * grp) * bs, bs)
        sq = sa_ref[pl.ds(ro, bs), u * bs:(u + 1) * bs]
        sa_ref[pl.ds(ro, bs), u * bs:(u + 1) * bs] = jnp.where(lk > lqq, NEG_INF, sq)

    def group_at(k):
        return jnp.minimum(k - 1, ngroups - 1)

    def body(h, carry):
        k = 3 * h
        sc_ref[...] = scores(group_at(k + 2))
        carry = attend(sa_ref[...], jnp.where(h == 0, g_own, k - 1), carry)
        sa_ref[...] = scores(group_at(k + 3))
        carry = attend(sb_ref[...], k, carry)
        sb_ref[...] = scores(group_at(k + 4))
        return attend(sc_ref[...], k + 1, carry)

    n_groups = g_own + 1
    n_steps = n_groups // 3
    sb_ref[...] = scores(group_at(1))
    carry = (jnp.full((1, tq), NEG_INF, F32), jnp.zeros((VT_ROWS, tq), F32))
    carry = lax.fori_loop(0, n_steps, body, carry)
    rest = n_groups - 3 * n_steps
    k = 3 * n_steps
    carry = lax.cond(rest >= 1,
                     lambda c: attend(sa_ref[...], jnp.where(k == 0, g_own, k - 1), c),
                     lambda c: c, carry)
    _, acc = lax.cond(rest == 2, lambda c: attend(sb_ref[...], k, c), lambda c: c, carry)
    out_t = acc[0:HEAD_DIM] / acc[HEAD_DIM:HEAD_DIM + 1]
    o_ref[...] = out_t.T.astype(o_ref.dtype)


def _moba_attn(pa, kaug, vt, kmean, slopes, batch, seq):
    nb = seq // MOBA_BLOCK
    nt = nb // MOBA_QTILE
    tq = MOBA_QTILE * MOBA_BLOCK
    bh = batch * MOBA_HEADS

    def qmap(n, t):
        return ((n // MOBA_HEADS) * nt + t, n % MOBA_HEADS)

    return pl.pallas_call(
        _moba_attn_kernel,
        name="moba_attn",
        grid=(bh, nt),
        in_specs=[
            pl.BlockSpec((tq, HEAD_DIM), qmap),
            pl.BlockSpec((1, nb, MOBA_BLOCK, AUG_W), lambda n, t: (n, 0, 0, 0)),
            pl.BlockSpec((1, nb // MOBA_GROUP, VT_ROWS, MOBA_GROUP * MOBA_BLOCK),
                         lambda n, t: (n, 0, 0, 0)),
            pl.BlockSpec((1, nb, HEAD_DIM), lambda n, t: (n, 0, 0)),
            pl.BlockSpec((1, 1, HEAD_DIM), lambda n, t: (n, 0, 0)),
        ],
        out_specs=pl.BlockSpec((tq, HEAD_DIM), qmap),
        out_shape=jax.ShapeDtypeStruct((batch * seq, MOBA_W), BF16),
        scratch_shapes=[pltpu.VMEM((AUG_W, tq), BF16),
                        pltpu.VMEM((MOBA_GROUP * MOBA_BLOCK, tq), F32),
                        pltpu.VMEM((MOBA_GROUP * MOBA_BLOCK, tq), F32),
                        pltpu.VMEM((MOBA_GROUP * MOBA_BLOCK, tq), F32)],
        compiler_params=_cparams(("parallel", "arbitrary")),
    )(pa, kaug, vt, kmean, slopes)


def _moba_gate_kernel(q_ref, kmean_ref, qrow_ref, sel_ref, rank_ref, cnt_ref):
    t = pl.program_id(1)
    bs = MOBA_BLOCK
    tq = q_ref.shape[0]
    nb = kmean_ref.shape[1]

    q = q_ref[...]
    qrow_ref[...] = q.astype(F32)
    qt = q.astype(F32).T.astype(BF16)

    lane = lax.broadcasted_iota(jnp.int32, (1, tq), 1)
    own = t * (tq // bs) + lane // bs
    km = kmean_ref[0]
    km_hi = km.astype(BF16)
    km_lo = (km - km_hi.astype(F32)).astype(BF16)
    gate = _dot(km_hi, qt) + _dot(km_lo, qt)
    blk = lax.broadcasted_iota(jnp.int32, (nb, tq), 0)
    gate = jnp.where(blk < own, gate, NEG_INF)

    @pl.when(t == 0)
    def _():
        cnt_ref[...] = jnp.zeros_like(cnt_ref)

    run = cnt_ref[0][:, 0:1]
    qi = lax.broadcasted_iota(jnp.int32, (tq, tq), 0)
    qj = lax.broadcasted_iota(jnp.int32, (tq, tq), 1)
    before = jnp.where(qi < qj, 1.0, 0.0).astype(BF16)
    sels, ranks = [], []
    for r in range(MOBA_TOPK):
        mx = jnp.max(gate, axis=0, keepdims=True)
        first = jnp.min(jnp.where(gate == mx, blk, nb), axis=0, keepdims=True)
        hit = blk == first
        gate = jnp.where(hit, -jnp.inf, gate)
        valid = own > r
        oh = jnp.where(hit & valid, 1.0, 0.0)
        prior = _dot(oh.astype(BF16), before)
        ranks.append(jnp.sum(oh * (run + prior), axis=0, keepdims=True))
        run = run + jnp.sum(oh, axis=1, keepdims=True)
        sels.append(jnp.where(valid, first, -1))
    cnt_ref[0] = jnp.broadcast_to(run, cnt_ref.shape[1:])
    pad = jnp.zeros((8 - MOBA_TOPK, tq), jnp.int32)
    sel_ref[0] = jnp.concatenate(sels + [pad], axis=0)
    rank_ref[0] = jnp.concatenate([x.astype(jnp.int32) for x in ranks] + [pad], axis=0)


def _moba_gate(pa, kmean, batch, seq):
    nb = seq // MOBA_BLOCK
    tq = min(MOBA_QTILE * MOBA_BLOCK, seq)
    nt = seq // tq
    bh = batch * MOBA_HEADS

    def qmap(n, t):
        return ((n // MOBA_HEADS) * nt + t, n % MOBA_HEADS)

    return pl.pallas_call(
        _moba_gate_kernel,
        name="moba_gate",
        grid=(bh, nt),
        in_specs=[
            pl.BlockSpec((tq, HEAD_DIM), qmap),
            pl.BlockSpec((1, nb, HEAD_DIM), lambda n, t: (n, 0, 0)),
        ],
        out_specs=[
            pl.BlockSpec((tq, HEAD_DIM), lambda n, t: (n * nt + t, 0)),
            pl.BlockSpec((1, 8, tq), lambda n, t: (n, 0, t)),
            pl.BlockSpec((1, 8, tq), lambda n, t: (n, 0, t)),
            pl.BlockSpec((1, nb, HEAD_DIM), lambda n, t: (n, 0, 0)),
        ],
        out_shape=[
            jax.ShapeDtypeStruct((bh * seq, HEAD_DIM), F32),
            jax.ShapeDtypeStruct((bh, 8, seq), jnp.int32),
            jax.ShapeDtypeStruct((bh, 8, seq), jnp.int32),
            jax.ShapeDtypeStruct((bh, nb, HEAD_DIM), F32),
        ],
        compiler_params=_cparams(("parallel", "arbitrary")),
    )(pa, kmean)


def _moba_slot_kernel(sel_ref, rank_ref, base_ref, slot_ref, *, trash):
    nb = base_ref.shape[1]
    tq = sel_ref.shape[2]
    base = base_ref[0][:, 0:1]
    blk = lax.broadcasted_iota(jnp.int32, (nb, tq), 0)
    rows = []
    for r in range(MOBA_TOPK):
        sel = sel_ref[0, r:r + 1, :]
        start = jnp.sum(jnp.where(blk == sel, base, 0.0), axis=0, keepdims=True)
        slot = start.astype(jnp.int32) + rank_ref[0, r:r + 1, :]
        rows.append(jnp.where(sel >= 0, slot, trash))
    rows.append(jnp.full((8 - MOBA_TOPK, tq), trash, jnp.int32))
    slot_ref[0] = jnp.concatenate(rows, axis=0)


def _moba_slots(sel, rank, base, trash):
    bh, _, seq = sel.shape
    nb = base.shape[1]
    tq = min(2048, seq)
    spec = pl.BlockSpec((1, 8, tq), lambda n, t: (n, 0, t))
    return pl.pallas_call(
        functools.partial(_moba_slot_kernel, trash=trash),
        name="moba_slots",
        grid=(bh, seq // tq),
        in_specs=[spec, spec, pl.BlockSpec((1, nb, HEAD_DIM), lambda n, t: (n, 0, 0))],
        out_specs=spec,
        out_shape=jax.ShapeDtypeStruct((bh, 8, seq), jnp.int32),
        compiler_params=_cparams(("parallel", "parallel")),
    )(sel, rank, base)


def _sc_mesh():
    return plsc.VectorSubcoreMesh(core_axis_name="core", subcore_axis_name="subcore")


def _sc_scatter_rows(rows, slots, n_out):
    bh8, seq = slots.shape
    bh = bh8 // 8
    nw = seq // SC_WINDOW

    @pl.kernel(out_type=jax.ShapeDtypeStruct((n_out, HEAD_DIM), rows.dtype), mesh=_sc_mesh(),
               scratch_types=[])
    def scatter(x_hbm, i_hbm, o_hbm):
        def body(x_vmem, i_vmem):
            pltpu.sync_copy(x_vmem, o_hbm.at[i_vmem.at[0]])

        pltpu.emit_pipeline(
            body,
            grid=(bh * MOBA_TOPK * nw,),
            in_specs=[
                pl.BlockSpec((SC_WINDOW, HEAD_DIM),
                             index_map=lambda i: ((i // (MOBA_TOPK * nw)) * nw + i % nw, 0)),
                pl.BlockSpec((1, SC_WINDOW),
                             index_map=lambda i: ((i // (MOBA_TOPK * nw)) * 8 + (i // nw) % MOBA_TOPK,
                                                  i % nw)),
            ],
            out_specs=[],
            core_axis_name=("core", "subcore"),
            dimension_semantics=(pltpu.PARALLEL,),
        )(x_hbm, i_hbm)

    return scatter(rows, slots)


def _sc_gather_rows(table, slots):
    bh8, seq = slots.shape
    bh = bh8 // 8
    nw = seq // SC_WINDOW
    n_out = bh * MOBA_TOPK * seq

    @pl.kernel(out_type=jax.ShapeDtypeStruct((n_out, HEAD_DIM), table.dtype), mesh=_sc_mesh())
    def gather(x_hbm, i_hbm, o_hbm):
        def body(i_vmem, o_vmem):
            pltpu.sync_copy(x_hbm.at[i_vmem.at[0]], o_vmem)

        pltpu.emit_pipeline(
            body,
            grid=(bh * MOBA_TOPK * nw,),
            in_specs=[
                pl.BlockSpec((1, SC_WINDOW),
                             index_map=lambda i: ((i // (MOBA_TOPK * nw)) * 8 + (i // nw) % MOBA_TOPK,
                                                  i % nw)),
            ],
            out_specs=[pl.BlockSpec((SC_WINDOW, HEAD_DIM), index_map=lambda i: (i, 0))],
            core_axis_name=("core", "subcore"),
            dimension_semantics=(pltpu.PARALLEL,),
        )(i_hbm, o_hbm)

    return gather(table, slots)


def _moba_routed_kernel(tb_ref, nt_ref, qs_ref, kaug_ref, vtb_ref, o_ref, sa_ref, sb_ref):
    n = pl.program_id(0)
    g = pl.program_id(1)
    tpg = qs_ref.shape[0] // SLOT_TILE
    tiles_cap = pl.num_programs(1) * tpg

    @pl.when(g * tpg < nt_ref[n])
    def _():
        row = lax.broadcasted_iota(jnp.int32, (AUG_W - HEAD_DIM, SLOT_TILE), 0)
        tail = jnp.where(row == AUG_BIAS - HEAD_DIM, 1.0, 0.0).astype(BF16)
        frow = lax.broadcasted_iota(jnp.int32, (HEAD_DIM, SLOT_TILE), 0)
        blocks = [tb_ref[n * tiles_cap + g * tpg + u] for u in range(tpg)]

        def scores(u):
            qt = qs_ref[u * SLOT_TILE:(u + 1) * SLOT_TILE, :].T.astype(BF16)
            return _dot(kaug_ref[0, blocks[u]], jnp.concatenate([qt, tail], axis=0))

        bufs = (sa_ref, sb_ref)
        sa_ref[...] = scores(0)
        for u in range(tpg):
            if u + 1 < tpg:
                bufs[(u + 1) % 2][...] = scores(u + 1)
            s = bufs[u % 2][...]
            m = jnp.max(s, axis=0, keepdims=True)
            p = jnp.exp(s - m).astype(BF16)
            acc = _dot(vtb_ref[0, blocks[u]], p)
            l = acc[HEAD_DIM:HEAD_DIM + 1]
            o = (acc[0:HEAD_DIM] / l).astype(BF16).astype(F32)
            lse = m + jnp.log(l)
            ob = lax.bitcast_convert_type(o, jnp.uint32)
            lb = lax.bitcast_convert_type(lse, jnp.uint32)
            extra = jnp.where(frow == 0, lb >> 16, jnp.where(frow == 1, lb & 0xFFFF, 0))
            packed = lax.bitcast_convert_type(ob | extra, F32)
            o_ref[u * SLOT_TILE:(u + 1) * SLOT_TILE, :] = packed.T


def _moba_routed(qs, kaug, vtb, tile_block, n_tiles, tiles_cap):
    bh, nb = kaug.shape[0], kaug.shape[1]
    tpg = ROUTED_TILES_PER_STEP
    steps = tiles_cap // tpg
    rows = tpg * SLOT_TILE

    def qmap(n, g, tb, nt):
        used = jnp.maximum((nt[n] + tpg - 1) // tpg, 1)
        return (n * steps + jnp.minimum(g, used - 1), 0)

    grid_spec = pltpu.PrefetchScalarGridSpec(
        num_scalar_prefetch=2,
        grid=(bh, steps),
        in_specs=[
            pl.BlockSpec((rows, HEAD_DIM), qmap),
            pl.BlockSpec((1, nb, MOBA_BLOCK, AUG_W), lambda n, g, tb, nt: (n, 0, 0, 0)),
            pl.BlockSpec((1, nb, VT_ROWS, MOBA_BLOCK), lambda n, g, tb, nt: (n, 0, 0, 0)),
        ],
        out_specs=pl.BlockSpec((rows, HEAD_DIM), lambda n, g, tb, nt: (n * steps + g, 0)),
        scratch_shapes=[pltpu.VMEM((MOBA_BLOCK, SLOT_TILE), F32),
                        pltpu.VMEM((MOBA_BLOCK, SLOT_TILE), F32)],
    )
    return pl.pallas_call(
        _moba_routed_kernel,
        name="moba_routed",
        grid_spec=grid_spec,
        out_shape=jax.ShapeDtypeStruct((qs.shape[0], HEAD_DIM), F32),
        compiler_params=_cparams(("parallel", "arbitrary")),
    )(tile_block, n_tiles, qs, kaug, vtb)


def _moba_merge_kernel(q_ref, kaug_ref, vtb_ref, og_ref, sel_ref, slope_ref, o_ref):
    t = pl.program_id(1)
    bs = MOBA_BLOCK
    tq = q_ref.shape[0]
    slope = slope_ref[0, 0:1, 0:1]

    qt = q_ref[...].astype(F32).T.astype(BF16)
    lane = lax.broadcasted_iota(jnp.int32, (1, tq), 1)
    tpos = (t * tq + lane).astype(F32)
    lk = lax.broadcasted_iota(jnp.int32, (bs, bs), 0)
    lq = lax.broadcasted_iota(jnp.int32, (bs, bs), 1)
    dist = (lq - lk).astype(F32)
    accs, ms = [], []
    for u in range(tq // bs):
        s = _dot(kaug_ref[0, u, :, 0:HEAD_DIM], qt[:, u * bs:(u + 1) * bs])
        s = jnp.where(dist >= 0.0, s - slope * dist, NEG_INF)
        m_u = jnp.max(s, axis=0, keepdims=True)
        p = jnp.exp(s - m_u).astype(BF16)
        accs.append(_dot(vtb_ref[0, u], p))
        ms.append(m_u)
    acc = jnp.concatenate(accs, axis=1)
    l = acc[HEAD_DIM:HEAD_DIM + 1]
    parts = [acc[0:HEAD_DIM] / l]
    lses = [jnp.concatenate(ms, axis=1) + jnp.log(l)]

    for r in range(MOBA_TOPK):
        u = lax.bitcast_convert_type(og_ref[0, r, 0].T, jnp.uint32)
        lb = ((u[0:1] & 0xFFFF) << 16) | (u[1:2] & 0xFFFF)
        lse_r = lax.bitcast_convert_type(lb, F32)
        sel_r = sel_ref[0, r:r + 1, :]
        valid = sel_r >= 0
        lses.append(jnp.where(valid, lse_r + slope * ((sel_r * bs).astype(F32) - tpos), NEG_INF))
        hi_half = lax.bitcast_convert_type((u >> 16) << 16, F32)
        parts.append(jnp.where(valid, hi_half, 0.0))
    mx = functools.reduce(jnp.maximum, lses)
    ws = [jnp.exp(x - mx) for x in lses]
    num = functools.reduce(lambda a, b: a + b, [w * o for w, o in zip(ws, parts)])
    out_t = num / functools.reduce(lambda a, b: a + b, ws)
    o_ref[...] = out_t.T.astype(o_ref.dtype)


def _moba_merge(pa, kaug, vtb, og, sel, slopes, batch, seq):
    tq = MOBA_QTILE * MOBA_BLOCK
    nt = seq // tq
    bh = batch * MOBA_HEADS
    og = og.reshape(bh, MOBA_TOPK, nt, tq, HEAD_DIM)

    def qmap(n, t):
        return ((n // MOBA_HEADS) * nt + t, n % MOBA_HEADS)

    return pl.pallas_call(
        _moba_merge_kernel,
        name="moba_merge",
        grid=(bh, nt),
        in_specs=[
            pl.BlockSpec((tq, HEAD_DIM), qmap),
            pl.BlockSpec((1, MOBA_QTILE, MOBA_BLOCK, AUG_W), lambda n, t: (n, t, 0, 0)),
            pl.BlockSpec((1, MOBA_QTILE, VT_ROWS, MOBA_BLOCK), lambda n, t: (n, t, 0, 0)),
            pl.BlockSpec((1, MOBA_TOPK, 1, tq, HEAD_DIM), lambda n, t: (n, 0, t, 0, 0)),
            pl.BlockSpec((1, 8, tq), lambda n, t: (n, 0, t)),
            pl.BlockSpec((1, 1, HEAD_DIM), lambda n, t: (n, 0, 0)),
        ],
        out_specs=pl.BlockSpec((tq, HEAD_DIM), qmap),
        out_shape=jax.ShapeDtypeStruct((batch * seq, MOBA_W), BF16),
        compiler_params=_cparams(("parallel", "parallel")),
    )(pa, kaug, vtb, og, sel, slopes)


def _moba_routed_attention(pa, kaug, vt, vtb, kmean, slopes, batch, seq):
    nb = seq // MOBA_BLOCK
    bh = batch * MOBA_HEADS
    qrows, sel, rank, cnt = _moba_gate(pa, kmean, batch, seq)

    step_slots = SLOT_TILE * ROUTED_TILES_PER_STEP
    cap = -(-(MOBA_TOPK * seq + nb * SLOT_TILE) // step_slots) * step_slots
    tiles_cap = cap // SLOT_TILE
    counts = cnt[:, :, 0].astype(jnp.int32)
    padded = (counts + SLOT_TILE - 1) // SLOT_TILE * SLOT_TILE
    ends = jnp.cumsum(padded, axis=1)
    base = ends - padded + (jnp.arange(bh, dtype=jnp.int32) * cap)[:, None]
    n_tiles = ends[:, -1] // SLOT_TILE
    tile_idx = jnp.arange(tiles_cap, dtype=jnp.int32)
    tile_block = jnp.sum(ends[:, None, :] // SLOT_TILE <= tile_idx[None, :, None], axis=-1)
    tile_block = jnp.minimum(tile_block, nb - 1).astype(jnp.int32).reshape(bh * tiles_cap)
    base_b = jnp.broadcast_to(base.astype(F32)[:, :, None], (bh, nb, HEAD_DIM))

    n_rows = bh * cap + SLOT_TILE
    slots = _moba_slots(sel, rank, base_b, bh * cap).reshape(bh * 8, seq)
    qs = _sc_scatter_rows(qrows, slots, n_rows)
    part = _moba_routed(qs, kaug, vtb, tile_block, n_tiles, tiles_cap)
    og = _sc_gather_rows(part, slots)
    return _moba_merge(pa, kaug, vtb, og, sel, slopes, batch, seq)


def _split3(x):
    hi = x.astype(BF16)
    r1 = x - hi.astype(F32)
    mid = r1.astype(BF16)
    lo = (r1 - mid.astype(F32)).astype(BF16)
    return hi, mid, lo


def _hgrn_kernel(qb_ref, fb_ref, ib_ref, gb_ref, lbl_ref, gn_ref, o_ref, s_ref, oi_ref, st_ref,
                 *, layer):
    tt = qb_ref.shape[0]
    sub = HGRN_SUB
    hd = HEAD_DIM

    @pl.when(pl.program_id(1) == 0)
    def _():
        s_ref[...] = jnp.zeros_like(s_ref)

    logits = lbl_ref[...]
    e = jnp.exp(logits - jnp.max(logits, axis=0, keepdims=True))
    pl_ = e / jnp.sum(e, axis=0, keepdims=True)
    lb = jnp.sum(pl_[0:layer + 1], axis=0, keepdims=True) - pl_[0:1]

    fb = fb_ref[...].astype(F32)
    t = jnp.exp(-jnp.abs(fb))
    r = 1.0 / (1.0 + t)
    sig_pos = jnp.where(fb >= 0, r, t * r)
    sig_neg = jnp.where(fb >= 0, t * r, r)
    f_gate = lb + (1.0 - lb) * sig_pos
    logf = jnp.log(jnp.maximum(f_gate, F_MIN))
    k_all = (1.0 - lb) * sig_neg
    qb = qb_ref[...].astype(F32)
    q_all = qb * jax.nn.sigmoid(qb)
    v_all = ib_ref[...].astype(F32)

    ri = lax.broadcasted_iota(jnp.int32, (tt, tt), 0)
    ci = lax.broadcasted_iota(jnp.int32, (tt, tt), 1)
    same = (ri // sub) == (ci // sub)
    tri = jnp.where(same & (ci <= ri), 1.0, 0.0).astype(BF16)
    hi, mid, lo = _split3(logf)
    gl = _dot(tri, hi) + _dot(tri, mid) + _dot(tri, lo)

    half = sub // 2
    nsub = tt // sub
    nv = tt // half
    lane_sum = jnp.ones((hd, hd), BF16)
    row_in = lax.broadcasted_iota(jnp.int32, (nv, half, hd), 1)

    def pair_terms(qx, gx, kx, vx, causal):
        n = qx.shape[0]
        out = jnp.zeros_like(qx)
        for rho in range(half):
            kr = pltpu.roll(kx, rho, 1) if rho else kx
            vr = pltpu.roll(vx, rho, 1) if rho else vx
            gr = pltpu.roll(gx[1], rho, 1) if rho else gx[1]
            prod = qx * kr * jnp.exp(gx[0] - gr)
            if causal and rho:
                prod = jnp.where(row_in >= rho, prod, 0.0)
            a = _dot(prod.reshape(n * half, hd).astype(BF16), lane_sum)
            out = out + a.reshape(n, half, hd) * vr
        return out

    for h in range(HGRN_HEADS):
        cs = slice(h * hd, (h + 1) * hd)
        q = q_all[:, cs]
        k = k_all[:, cs]
        v = v_all[:, cs]
        g = gl[:, cs]

        q3, k3, v3, g3 = (a.reshape(nv, half, hd) for a in (q, k, v, g))
        od = pair_terms(q3, (g3, g3), k3, v3, True).reshape(nsub, 2, half, hd)
        q4, k4, v4, g4 = (a.reshape(nsub, 2, half, hd) for a in (q, k, v, g))
        oh = pair_terms(q4[:, 1], (g4[:, 1], g4[:, 0]), k4[:, 0], v4[:, 0], False)
        o_diag = jnp.stack([od[:, 0], od[:, 1] + oh], axis=1).reshape(tt, hd)

        gs = g.reshape(nsub, sub, hd)
        g_end = gs[:, sub - 1:sub, :]
        qd = (q * jnp.exp(g)).astype(BF16).reshape(nsub, sub, hd)
        kd = (k.reshape(nsub, sub, hd) * jnp.exp(g_end - gs)).astype(BF16)
        vb = v.astype(BF16).reshape(nsub, sub, hd)
        dec = jnp.exp(g_end)
        upd = [_dot_t0(vb[c], kd[c]) for c in range(nsub)]
        st = s_ref[h]
        for c in range(nsub):
            st_ref[c] = st.astype(BF16)
            st = st * dec[c] + upd[c]
        s_ref[h] = st
        for c in range(nsub):
            oi_ref[c * sub:(c + 1) * sub, cs] = lax.dot_general(
                qd[c], st_ref[c], (((1,), (1,)), ((), ())), preferred_element_type=F32)

        o = oi_ref[:, cs] + o_diag
        rr = lax.rsqrt(jnp.mean(o * o, axis=-1, keepdims=True) + RMS_EPS)
        gate = jax.nn.sigmoid(gb_ref[:, cs].astype(F32))
        o_ref[:, cs] = (o * rr * gn_ref[...] * gate).astype(o_ref.dtype)


def _hgrn(pa, lb_logits, out_norm, layer, batch, seq):
    tt = min(256, seq)
    nt = seq // tt
    c0 = 3 * MOBA_W // HGRN_W
    nl = lb_logits.shape[0]

    def cmap(off):
        return lambda b, t: (b * nt + t, c0 + off)

    return pl.pallas_call(
        functools.partial(_hgrn_kernel, layer=layer),
        name="hgrn",
        grid=(batch, nt),
        in_specs=[
            pl.BlockSpec((tt, HGRN_W), cmap(0)),
            pl.BlockSpec((tt, HGRN_W), cmap(1)),
            pl.BlockSpec((tt, HGRN_W), cmap(2)),
            pl.BlockSpec((tt, HGRN_W), cmap(3)),
            pl.BlockSpec((nl, HGRN_W), lambda b, t: (0, 0)),
            pl.BlockSpec((1, HEAD_DIM), lambda b, t: (0, 0)),
        ],
        out_specs=pl.BlockSpec((tt, HGRN_W), lambda b, t: (b * nt + t, 0)),
        out_shape=jax.ShapeDtypeStruct((batch * seq, HGRN_W), BF16),
        scratch_shapes=[pltpu.VMEM((HGRN_HEADS, HEAD_DIM, HEAD_DIM), F32),
                        pltpu.VMEM((tt, HGRN_W), F32),
                        pltpu.VMEM((tt // HGRN_SUB, HEAD_DIM, HEAD_DIM), BF16)],
        compiler_params=_cparams(("parallel", "arbitrary")),
    )(pa, pa, pa, pa, lb_logits, out_norm.reshape(1, HEAD_DIM))


def _memkv_kernel(mem_ref, g_ref, w_ref, kt_ref, v_ref):
    h = _rms(mem_ref[0], g_ref[...]).astype(BF16)
    kv = _dot(h, w_ref[...])
    kt_ref[0] = kv[:, 0:MEM_W].T.astype(BF16)
    v_ref[0] = kv[:, MEM_W:].astype(BF16)


def _memkv(mem, g, w):
    b, m, d = mem.shape
    return pl.pallas_call(
        _memkv_kernel,
        name="memkv",
        grid=(b,),
        in_specs=[
            pl.BlockSpec((1, m, d), lambda i: (i, 0, 0)),
            pl.BlockSpec((1, d), lambda i: (0, 0)),
            pl.BlockSpec((d, 2 * MEM_W), lambda i: (0, 0)),
        ],
        out_specs=[
            pl.BlockSpec((1, MEM_W, m), lambda i: (i, 0, 0)),
            pl.BlockSpec((1, m, MEM_W), lambda i: (i, 0, 0)),
        ],
        out_shape=[
            jax.ShapeDtypeStruct((b, MEM_W, m), BF16),
            jax.ShapeDtypeStruct((b, m, MEM_W), BF16),
        ],
        compiler_params=_cparams(("parallel",)),
    )(mem, g.reshape(1, d), w)


def _merge_kernel(x_ref, oa_ref, ob_ref, qm_ref, g0_ref, g1_ref, g2_ref, mkt_ref, mv_ref,
                  wa_ref, wb_ref, wm_ref, wo_ref, o_ref, om_ref):
    scale = HEAD_DIM ** -0.5
    for h in range(MEM_HEADS):
        cs = slice(h * HEAD_DIM, (h + 1) * HEAD_DIM)
        s = _dot(qm_ref[:, cs], mkt_ref[0, cs, :]) * scale
        p = jnp.exp(s - jnp.max(s, axis=-1, keepdims=True))
        l = jnp.sum(p, axis=-1, keepdims=True)
        om_ref[:, cs] = (_dot(p.astype(BF16), mv_ref[0, :, cs]) / l).astype(BF16)

    y = jax.nn.sigmoid(g0_ref[...].astype(F32)) * _dot(oa_ref[...], wa_ref[...])
    y += jax.nn.sigmoid(g1_ref[...].astype(F32)) * _dot(ob_ref[...], wb_ref[...])
    y += jax.nn.sigmoid(g2_ref[...].astype(F32)) * _dot(om_ref[...], wm_ref[...])
    o_ref[...] = x_ref[...] + _dot(y.astype(BF16), wo_ref[...])


def _merge(x, oa, ob, pa, gates, mkt, mv, wa, wb, wm, wo, batch, seq):
    n, d = x.shape
    m = mv.shape[1]
    tm = _pick(seq, 256)
    per_b = seq // tm
    qcol = (3 * MOBA_W + 4 * HGRN_W) // MEM_W
    row = lambda i: (i, 0)
    return pl.pallas_call(
        _merge_kernel,
        name="merge",
        grid=(n // tm,),
        in_specs=[
            pl.BlockSpec((tm, d), row),
            pl.BlockSpec((tm, MOBA_W), row),
            pl.BlockSpec((tm, HGRN_W), row),
            pl.BlockSpec((tm, MEM_W), lambda i: (i, qcol)),
            pl.BlockSpec((tm, d), lambda i: (i, 0)),
            pl.BlockSpec((tm, d), lambda i: (i, 1)),
            pl.BlockSpec((tm, d), lambda i: (i, 2)),
            pl.BlockSpec((1, MEM_W, m), lambda i: (i // per_b, 0, 0)),
            pl.BlockSpec((1, m, MEM_W), lambda i: (i // per_b, 0, 0)),
            _const_spec((MOBA_W, d)),
            _const_spec((HGRN_W, d)),
            _const_spec((MEM_W, d)),
            _const_spec((d, d)),
        ],
        out_specs=pl.BlockSpec((tm, d), row),
        out_shape=jax.ShapeDtypeStruct((n, d), F32),
        scratch_shapes=[pltpu.VMEM((tm, MEM_W), BF16)],
        compiler_params=_cparams(("parallel",)),
    )(x, oa, ob, pa, gates, gates, gates, mkt, mv, wa, wb, wm, wo)


def kernel(x, mem, ffn1_norm, ffn1_w1, ffn1_w3, ffn1_w2, mix_norm, w_in, hgrn_lb_logits,
           hgrn_out_norm, mem_norm, w_mem_kv, w_proj_moba, w_proj_hgrn, w_proj_mem, w_out,
           ffn2_norm, ffn2_w1, ffn2_w3, ffn2_w2, final_norm):
    batch, seq, d = x.shape
    depth = ffn1_w1.shape[0]
    assert seq % MOBA_BLOCK == 0 and seq // MOBA_BLOCK <= AUG_MAX_BLOCKS
    assert w_in.shape[-1] == MIX_W + 3 * d
    bf = lambda a: a.astype(BF16)

    hs = jnp.arange(1, MOBA_HEADS + 1, dtype=F32)
    slopes = jnp.tile(jnp.exp2(-8.0 * hs / MOBA_HEADS), batch)
    slopes = jnp.broadcast_to(slopes[:, None, None], (batch * MOBA_HEADS, 1, HEAD_DIM))
    mix_scale = jnp.concatenate([jnp.full((MOBA_W,), HEAD_DIM ** -0.5, F32),
                                 jnp.ones((MIX_W - MOBA_W,), F32)])
    gate_scale = jnp.ones((3 * d,), F32)

    xs = x.reshape(batch * seq, d)
    for l in range(depth):
        last = l == depth - 1
        xs = _ffn(xs, ffn1_norm[l], bf(ffn1_w1[l]), bf(ffn1_w3[l]), bf(ffn1_w2[l]),
                  final_norm, final=False)

        w_l = w_in[l]
        pa = _normproj(xs, mix_norm[l], bf(w_l[:, :MIX_W]), mix_scale)
        gates = _normproj(xs, mix_norm[l], bf(w_l[:, MIX_W:]), gate_scale)

        kaug, vt, vtb, kmean = _moba_prep(pa, slopes, batch, seq)
        oa = _moba_routed_attention(pa, kaug, vt, vtb, kmean, slopes, batch, seq)
        ob = _hgrn(pa, hgrn_lb_logits, hgrn_out_norm[l], l, batch, seq)
        mkt, mv = _memkv(mem, mem_norm[l], bf(w_mem_kv[l]))
        xs = _merge(xs, oa, ob, pa, gates, mkt, mv, bf(w_proj_moba[l]), bf(w_proj_hgrn[l]),
                    bf(w_proj_mem[l]), bf(w_out[l]), batch, seq)

        xs = _ffn(xs, ffn2_norm[l], bf(ffn2_w1[l]), bf(ffn2_w3[l]), bf(ffn2_w2[l]),
                  final_norm, final=last)
    return xs.reshape(batch, seq, d)
```

```python
import functools

import jax
import jax.numpy as jnp
from jax import lax
from jax.experimental import pallas as pl
from jax.experimental.pallas import tpu as pltpu
from jax.experimental.pallas import tpu_sc as plsc

F32 = jnp.float32
BF16 = jnp.bfloat16

HEAD_DIM = 128
MOBA_HEADS = 8
MOBA_BLOCK = 256
MOBA_TOPK = 3
HGRN_HEADS = 4
MEM_HEADS = 4
RMS_EPS = 1e-6
NEG_INF = -1e30
F_MIN = 1e-20

MOBA_W = MOBA_HEADS * HEAD_DIM
HGRN_W = HGRN_HEADS * HEAD_DIM
MEM_W = MEM_HEADS * HEAD_DIM
MIX_W = 3 * MOBA_W + 4 * HGRN_W + MEM_W

VMEM_LIMIT_BYTES = 60 * 1024 * 1024

AUG_W = 2 * HEAD_DIM
AUG_BIAS = HEAD_DIM
VT_ROWS = HEAD_DIM + 16
MOBA_QTILE = 4

HGRN_SUB = 16
FFN_OUT_CHUNK = 512
SLOT_TILE = 512
ROUTED_TILES_PER_STEP = 8
SC_WINDOW = 128


def _cparams(sem):
    return pltpu.CompilerParams(dimension_semantics=sem, vmem_limit_bytes=VMEM_LIMIT_BYTES)


def _dot(a, b):
    return jnp.dot(a, b, preferred_element_type=F32)


def _dot_t0(a, b):
    return lax.dot_general(a, b, (((0,), (0,)), ((), ())), preferred_element_type=F32)


def _rms(xf, g):
    r = lax.rsqrt(jnp.mean(xf * xf, axis=-1, keepdims=True) + RMS_EPS)
    return xf * r * g


def _pick(n, want):
    if n <= want:
        return n
    t = (want // 128) * 128
    while t >= 128:
        if n % t == 0:
            return t
        t -= 128
    return n


def _const_spec(shape):
    nd = len(shape)
    return pl.BlockSpec(shape, lambda *_: (0,) * nd)


def _ffn_kernel(x_ref, g_ref, w1_ref, w3_ref, w2_ref, fg_ref, o_ref, h_ref, *, final):
    j = pl.program_id(1)

    @pl.when(j == 0)
    def _():
        xf = x_ref[...]
        h_ref[...] = _rms(xf, g_ref[...]).astype(BF16)
        o_ref[...] = xf

    h = h_ref[...]
    u = _dot(h, w1_ref[...])
    v = _dot(h, w3_ref[...])
    a = (0.5 * u * jax.nn.sigmoid(u) * v).astype(BF16)
    tn = FFN_OUT_CHUNK if o_ref.shape[1] % FFN_OUT_CHUNK == 0 else o_ref.shape[1]
    for c in range(o_ref.shape[1] // tn):
        cs = slice(c * tn, (c + 1) * tn)
        o_ref[:, cs] += _dot(a, w2_ref[:, cs])

    if final:
        @pl.when(j == pl.num_programs(1) - 1)
        def _():
            o_ref[...] = _rms(o_ref[...], fg_ref[...])


def _ffn(x, g, w1, w3, w2, final_g, *, final):
    n, d = x.shape
    dff = w1.shape[1]
    tm = _pick(n, 1024)
    tf = _pick(dff, 512)
    return pl.pallas_call(
        functools.partial(_ffn_kernel, final=final),
        name="ffn_final" if final else "ffn",
        grid=(n // tm, dff // tf),
        in_specs=[
            pl.BlockSpec((tm, d), lambda i, j: (i, 0), pipeline_mode=pl.Buffered(1)),
            pl.BlockSpec((1, d), lambda i, j: (0, 0)),
            pl.BlockSpec((d, tf), lambda i, j: (0, j)),
            pl.BlockSpec((d, tf), lambda i, j: (0, j)),
            pl.BlockSpec((tf, d), lambda i, j: (j, 0)),
            pl.BlockSpec((1, d), lambda i, j: (0, 0)),
        ],
        out_specs=pl.BlockSpec((tm, d), lambda i, j: (i, 0)),
        out_shape=jax.ShapeDtypeStruct((n, d), F32),
        scratch_shapes=[pltpu.VMEM((tm, d), BF16)],
        compiler_params=_cparams(("parallel", "arbitrary")),
    )(x, g.reshape(1, d), w1, w3, w2, final_g.reshape(1, d))


def _normproj_kernel(x_ref, g_ref, w_ref, s_ref, o_ref, h_ref, *, tn):
    h_ref[...] = _rms(x_ref[...], g_ref[...]).astype(BF16)
    for c in range(w_ref.shape[1] // tn):
        cs = slice(c * tn, (c + 1) * tn)
        o_ref[:, cs] = (_dot(h_ref[...], w_ref[:, cs]) * s_ref[:, cs]).astype(o_ref.dtype)


def _normproj(x, g, w, col_scale):
    n, d = x.shape
    nout = w.shape[1]
    tm = _pick(n, 512)
    tn = _pick(nout, 512)
    return pl.pallas_call(
        functools.partial(_normproj_kernel, tn=tn),
        name="normproj",
        grid=(n // tm,),
        in_specs=[
            pl.BlockSpec((tm, d), lambda i: (i, 0)),
            pl.BlockSpec((1, d), lambda i: (0, 0)),
            pl.BlockSpec((d, nout), lambda i: (0, 0), pipeline_mode=pl.Buffered(1)),
            pl.BlockSpec((1, nout), lambda i: (0, 0)),
        ],
        out_specs=pl.BlockSpec((tm, nout), lambda i: (i, 0)),
        out_shape=jax.ShapeDtypeStruct((n, nout), BF16),
        scratch_shapes=[pltpu.VMEM((tm, d), BF16)],
        compiler_params=_cparams(("parallel",)),
    )(x, g.reshape(1, d), w, col_scale.reshape(1, nout))


def _moba_prep_kernel(k_ref, v_ref, slope_ref, kaug_ref, vtb_ref, kmean_ref):
    g = pl.program_id(1)
    bs = MOBA_BLOCK
    nblk = k_ref.shape[0] // bs
    slope = slope_ref[0, 0:1, 0:1]
    lane = lax.broadcasted_iota(jnp.int32, (bs, AUG_W - HEAD_DIM), 1)
    row = lax.broadcasted_iota(jnp.int32, (bs, AUG_W - HEAD_DIM), 0).astype(F32)
    extra = jnp.where(lane == AUG_BIAS - HEAD_DIM, slope * row, 0.0).astype(BF16)
    orow = lax.broadcasted_iota(jnp.int32, (VT_ROWS - HEAD_DIM, bs), 0)
    ones_row = jnp.where(orow == 0, 1.0, 0.0).astype(BF16)
    for u in range(nblk):
        k = k_ref[u * bs:(u + 1) * bs, :]
        kmean_ref[0, pl.ds(g * nblk + u, 1), :] = jnp.mean(k.astype(F32), axis=0, keepdims=True)
        kaug_ref[0, u, :, 0:HEAD_DIM] = k
        kaug_ref[0, u, :, HEAD_DIM:AUG_W] = extra
        vt = v_ref[u * bs:(u + 1) * bs, :].astype(F32).T
        vtb_ref[0, u, 0:HEAD_DIM, :] = vt.astype(BF16)
        vtb_ref[0, u, HEAD_DIM:VT_ROWS, :] = ones_row


def _moba_prep(pa, slopes, batch, seq):
    nb = seq // MOBA_BLOCK
    grp = MOBA_QTILE
    ng = nb // grp
    bh = batch * MOBA_HEADS
    kcol = MOBA_W // HEAD_DIM
    vcol = 2 * MOBA_W // HEAD_DIM

    def kmap(n, g):
        return ((n // MOBA_HEADS) * ng + g, kcol + n % MOBA_HEADS)

    def vmap(n, g):
        return ((n // MOBA_HEADS) * ng + g, vcol + n % MOBA_HEADS)

    return pl.pallas_call(
        _moba_prep_kernel,
        name="moba_prep",
        grid=(bh, ng),
        in_specs=[
            pl.BlockSpec((grp * MOBA_BLOCK, HEAD_DIM), kmap),
            pl.BlockSpec((grp * MOBA_BLOCK, HEAD_DIM), vmap),
            pl.BlockSpec((1, 1, HEAD_DIM), lambda n, g: (n, 0, 0)),
        ],
        out_specs=[
            pl.BlockSpec((1, grp, MOBA_BLOCK, AUG_W), lambda n, g: (n, g, 0, 0)),
            pl.BlockSpec((1, grp, VT_ROWS, MOBA_BLOCK), lambda n, g: (n, g, 0, 0)),
            pl.BlockSpec((1, nb, HEAD_DIM), lambda n, g: (n, 0, 0)),
        ],
        out_shape=[
            jax.ShapeDtypeStruct((bh, nb, MOBA_BLOCK, AUG_W), BF16),
            jax.ShapeDtypeStruct((bh, nb, VT_ROWS, MOBA_BLOCK), BF16),
            jax.ShapeDtypeStruct((bh, nb, HEAD_DIM), F32),
        ],
        compiler_params=_cparams(("parallel", "arbitrary")),
    )(pa, pa, slopes)


def _moba_gate_kernel(q_ref, kmean_ref, qrow_ref, sel_ref, rank_ref, cnt_ref):
    t = pl.program_id(1)
    bs = MOBA_BLOCK
    tq = q_ref.shape[0]
    nb = kmean_ref.shape[1]

    q = q_ref[...]
    qrow_ref[...] = q.astype(F32)
    qt = q.astype(F32).T.astype(BF16)

    lane = lax.broadcasted_iota(jnp.int32, (1, tq), 1)
    own = t * (tq // bs) + lane // bs
    km = kmean_ref[0]
    km_hi = km.astype(BF16)
    km_lo = (km - km_hi.astype(F32)).astype(BF16)
    gate = _dot(km_hi, qt) + _dot(km_lo, qt)
    blk = lax.broadcasted_iota(jnp.int32, (nb, tq), 0)
    gate = jnp.where(blk < own, gate, NEG_INF)

    @pl.when(t == 0)
    def _():
        cnt_ref[...] = jnp.zeros_like(cnt_ref)

    run = cnt_ref[0][:, 0:1]
    qi = lax.broadcasted_iota(jnp.int32, (tq, tq), 0)
    qj = lax.broadcasted_iota(jnp.int32, (tq, tq), 1)
    before = jnp.where(qi < qj, 1.0, 0.0).astype(BF16)
    sels, ranks = [], []
    for r in range(MOBA_TOPK):
        mx = jnp.max(gate, axis=0, keepdims=True)
        first = jnp.min(jnp.where(gate == mx, blk, nb), axis=0, keepdims=True)
        hit = blk == first
        gate = jnp.where(hit, -jnp.inf, gate)
        valid = own > r
        oh = jnp.where(hit & valid, 1.0, 0.0)
        prior = _dot(oh.astype(BF16), before)
        ranks.append(jnp.sum(oh * (run + prior), axis=0, keepdims=True))
        run = run + jnp.sum(oh, axis=1, keepdims=True)
        sels.append(jnp.where(valid, first, -1))
    cnt_ref[0] = jnp.broadcast_to(run, cnt_ref.shape[1:])
    pad = jnp.zeros((8 - MOBA_TOPK, tq), jnp.int32)
    sel_ref[0] = jnp.concatenate(sels + [pad], axis=0)
    rank_ref[0] = jnp.concatenate([x.astype(jnp.int32) for x in ranks] + [pad], axis=0)


def _moba_gate(pa, kmean, batch, seq):
    nb = seq // MOBA_BLOCK
    tq = min(MOBA_QTILE * MOBA_BLOCK, seq)
    nt = seq // tq
    bh = batch * MOBA_HEADS

    def qmap(n, t):
        return ((n // MOBA_HEADS) * nt + t, n % MOBA_HEADS)

    return pl.pallas_call(
        _moba_gate_kernel,
        name="moba_gate",
        grid=(bh, nt),
        in_specs=[
            pl.BlockSpec((tq, HEAD_DIM), qmap),
            pl.BlockSpec((1, nb, HEAD_DIM), lambda n, t: (n, 0, 0)),
        ],
        out_specs=[
            pl.BlockSpec((tq, HEAD_DIM), lambda n, t: (n * nt + t, 0)),
            pl.BlockSpec((1, 8, tq), lambda n, t: (n, 0, t)),
            pl.BlockSpec((1, 8, tq), lambda n, t: (n, 0, t)),
            pl.BlockSpec((1, nb, HEAD_DIM), lambda n, t: (n, 0, 0)),
        ],
        out_shape=[
            jax.ShapeDtypeStruct((bh * seq, HEAD_DIM), F32),
            jax.ShapeDtypeStruct((bh, 8, seq), jnp.int32),
            jax.ShapeDtypeStruct((bh, 8, seq), jnp.int32),
            jax.ShapeDtypeStruct((bh, nb, HEAD_DIM), F32),
        ],
        compiler_params=_cparams(("parallel", "arbitrary")),
    )(pa, kmean)


def _moba_slot_kernel(sel_ref, rank_ref, base_ref, slot_ref, *, trash):
    nb = base_ref.shape[1]
    tq = sel_ref.shape[2]
    base = base_ref[0][:, 0:1]
    blk = lax.broadcasted_iota(jnp.int32, (nb, tq), 0)
    spare = trash + lax.broadcasted_iota(jnp.int32, (1, tq), 1) % SLOT_TILE
    rows = []
    for r in range(MOBA_TOPK):
        sel = sel_ref[0, r:r + 1, :]
        start = jnp.sum(jnp.where(blk == sel, base, 0.0), axis=0, keepdims=True)
        slot = start.astype(jnp.int32) + rank_ref[0, r:r + 1, :]
        rows.append(jnp.where(sel >= 0, slot, spare))
    rows.append(jnp.broadcast_to(spare, (8 - MOBA_TOPK, tq)))
    slot_ref[0] = jnp.concatenate(rows, axis=0)


def _moba_slots(sel, rank, base, trash):
    bh, _, seq = sel.shape
    nb = base.shape[1]
    tq = min(2048, seq)
    spec = pl.BlockSpec((1, 8, tq), lambda n, t: (n, 0, t))
    return pl.pallas_call(
        functools.partial(_moba_slot_kernel, trash=trash),
        name="moba_slots",
        grid=(bh, seq // tq),
        in_specs=[spec, spec, pl.BlockSpec((1, nb, HEAD_DIM), lambda n, t: (n, 0, 0))],
        out_specs=spec,
        out_shape=jax.ShapeDtypeStruct((bh, 8, seq), jnp.int32),
        compiler_params=_cparams(("parallel", "parallel")),
    )(sel, rank, base)


def _sc_mesh():
    return plsc.VectorSubcoreMesh(core_axis_name="core", subcore_axis_name="subcore")


def _sc_scatter_rows(rows, slots, n_out):
    bh8, seq = slots.shape
    bh = bh8 // 8
    nw = seq // SC_WINDOW

    @pl.kernel(out_type=jax.ShapeDtypeStruct((n_out, HEAD_DIM), rows.dtype), mesh=_sc_mesh(),
               scratch_types=[])
    def scatter(x_hbm, i_hbm, o_hbm):
        def body(x_vmem, i_vmem):
            pltpu.sync_copy(x_vmem, o_hbm.at[i_vmem.at[0]])

        pltpu.emit_pipeline(
            body,
            grid=(bh * MOBA_TOPK * nw,),
            in_specs=[
                pl.BlockSpec((SC_WINDOW, HEAD_DIM),
                             index_map=lambda i: ((i // (MOBA_TOPK * nw)) * nw + i % nw, 0)),
                pl.BlockSpec((1, SC_WINDOW),
                             index_map=lambda i: ((i // (MOBA_TOPK * nw)) * 8 + (i // nw) % MOBA_TOPK,
                                                  i % nw)),
            ],
            out_specs=[],
            core_axis_name=("core", "subcore"),
            dimension_semantics=(pltpu.PARALLEL,),
        )(x_hbm, i_hbm)

    return scatter(rows, slots)


def _sc_gather_rows(table, slots):
    bh8, seq = slots.shape
    bh = bh8 // 8
    nw = seq // SC_WINDOW
    n_out = bh * MOBA_TOPK * seq

    @pl.kernel(out_type=jax.ShapeDtypeStruct((n_out, HEAD_DIM), table.dtype), mesh=_sc_mesh())
    def gather(x_hbm, i_hbm, o_hbm):
        def body(i_vmem, o_vmem):
            pltpu.sync_copy(x_hbm.at[i_vmem.at[0]], o_vmem)

        pltpu.emit_pipeline(
            body,
            grid=(bh * MOBA_TOPK * nw,),
            in_specs=[
                pl.BlockSpec((1, SC_WINDOW),
                             index_map=lambda i: ((i // (MOBA_TOPK * nw)) * 8 + (i // nw) % MOBA_TOPK,
                                                  i % nw)),
            ],
            out_specs=[pl.BlockSpec((SC_WINDOW, HEAD_DIM), index_map=lambda i: (i, 0))],
            core_axis_name=("core", "subcore"),
            dimension_semantics=(pltpu.PARALLEL,),
        )(i_hbm, o_hbm)

    return gather(table, slots)


def _moba_routed_kernel(tb_ref, nt_ref, qs_ref, kaug_ref, vtb_ref, o_ref, sa_ref, sb_ref):
    n = pl.program_id(0)
    g = pl.program_id(1)
    tpg = qs_ref.shape[0] // SLOT_TILE
    tiles_cap = pl.num_programs(1) * tpg

    @pl.when(g * tpg < nt_ref[n])
    def _():
        row = lax.broadcasted_iota(jnp.int32, (AUG_W - HEAD_DIM, SLOT_TILE), 0)
        tail = jnp.where(row == AUG_BIAS - HEAD_DIM, 1.0, 0.0).astype(BF16)
        frow = lax.broadcasted_iota(jnp.int32, (HEAD_DIM, SLOT_TILE), 0)
        blocks = [tb_ref[n * tiles_cap + g * tpg + u] for u in range(tpg)]

        def scores(u):
            qt = qs_ref[u * SLOT_TILE:(u + 1) * SLOT_TILE, :].T.astype(BF16)
            return _dot(kaug_ref[0, blocks[u]], jnp.concatenate([qt, tail], axis=0))

        bufs = (sa_ref, sb_ref)
        sa_ref[...] = scores(0)
        for u in range(tpg):
            if u + 1 < tpg:
                bufs[(u + 1) % 2][...] = scores(u + 1)
            s = bufs[u % 2][...]
            m = jnp.max(s, axis=0, keepdims=True)
            p = jnp.exp(s - m).astype(BF16)
            acc = _dot(vtb_ref[0, blocks[u]], p)
            l = acc[HEAD_DIM:HEAD_DIM + 1]
            o = (acc[0:HEAD_DIM] / l).astype(BF16).astype(F32)
            lse = m + jnp.log(l)
            ob = lax.bitcast_convert_type(o, jnp.uint32)
            lb = lax.bitcast_convert_type(lse, jnp.uint32)
            extra = jnp.where(frow == 0, lb >> 16, jnp.where(frow == 1, lb & 0xFFFF, 0))
            packed = lax.bitcast_convert_type(ob | extra, F32)
            o_ref[u * SLOT_TILE:(u + 1) * SLOT_TILE, :] = packed.T


def _moba_routed(qs, kaug, vtb, tile_block, n_tiles, tiles_cap):
    bh, nb = kaug.shape[0], kaug.shape[1]
    tpg = ROUTED_TILES_PER_STEP
    steps = tiles_cap // tpg
    rows = tpg * SLOT_TILE

    def qmap(n, g, tb, nt):
        used = jnp.maximum((nt[n] + tpg - 1) // tpg, 1)
        return (n * steps + jnp.minimum(g, used - 1), 0)

    grid_spec = pltpu.PrefetchScalarGridSpec(
        num_scalar_prefetch=2,
        grid=(bh, steps),
        in_specs=[
            pl.BlockSpec((rows, HEAD_DIM), qmap),
            pl.BlockSpec((1, nb, MOBA_BLOCK, AUG_W), lambda n, g, tb, nt: (n, 0, 0, 0)),
            pl.BlockSpec((1, nb, VT_ROWS, MOBA_BLOCK), lambda n, g, tb, nt: (n, 0, 0, 0)),
        ],
        out_specs=pl.BlockSpec((rows, HEAD_DIM), lambda n, g, tb, nt: (n * steps + g, 0)),
        scratch_shapes=[pltpu.VMEM((MOBA_BLOCK, SLOT_TILE), F32),
                        pltpu.VMEM((MOBA_BLOCK, SLOT_TILE), F32)],
    )
    return pl.pallas_call(
        _moba_routed_kernel,
        name="moba_routed",
        grid_spec=grid_spec,
        out_shape=jax.ShapeDtypeStruct((qs.shape[0], HEAD_DIM), F32),
        compiler_params=_cparams(("parallel", "arbitrary")),
    )(tile_block, n_tiles, qs, kaug, vtb)


def _moba_merge_kernel(q_ref, kaug_ref, vtb_ref, og_ref, sel_ref, slope_ref, o_ref):
    t = pl.program_id(1)
    bs = MOBA_BLOCK
    tq = q_ref.shape[0]
    slope = slope_ref[0, 0:1, 0:1]

    qt = q_ref[...].astype(F32).T.astype(BF16)
    lane = lax.broadcasted_iota(jnp.int32, (1, tq), 1)
    tpos = (t * tq + lane).astype(F32)
    lk = lax.broadcasted_iota(jnp.int32, (bs, bs), 0)
    lq = lax.broadcasted_iota(jnp.int32, (bs, bs), 1)
    dist = (lq - lk).astype(F32)
    accs, ms = [], []
    for u in range(tq // bs):
        s = _dot(kaug_ref[0, u, :, 0:HEAD_DIM], qt[:, u * bs:(u + 1) * bs])
        s = jnp.where(dist >= 0.0, s - slope * dist, NEG_INF)
        m_u = jnp.max(s, axis=0, keepdims=True)
        p = jnp.exp(s - m_u).astype(BF16)
        accs.append(_dot(vtb_ref[0, u], p))
        ms.append(m_u)
    acc = jnp.concatenate(accs, axis=1)
    l = acc[HEAD_DIM:HEAD_DIM + 1]
    parts = [acc[0:HEAD_DIM] / l]
    lses = [jnp.concatenate(ms, axis=1) + jnp.log(l)]

    for r in range(MOBA_TOPK):
        u = lax.bitcast_convert_type(og_ref[0, r, 0].T, jnp.uint32)
        lb = ((u[0:1] & 0xFFFF) << 16) | (u[1:2] & 0xFFFF)
        lse_r = lax.bitcast_convert_type(lb, F32)
        sel_r = sel_ref[0, r:r + 1, :]
        valid = sel_r >= 0
        lses.append(jnp.where(valid, lse_r + slope * ((sel_r * bs).astype(F32) - tpos), NEG_INF))
        hi_half = lax.bitcast_convert_type((u >> 16) << 16, F32)
        parts.append(jnp.where(valid, hi_half, 0.0))
    mx = functools.reduce(jnp.maximum, lses)
    ws = [jnp.exp(x - mx) for x in lses]
    num = functools.reduce(lambda a, b: a + b, [w * o for w, o in zip(ws, parts)])
    out_t = num / functools.reduce(lambda a, b: a + b, ws)
    o_ref[...] = out_t.T.astype(o_ref.dtype)


def _moba_merge(pa, kaug, vtb, og, sel, slopes, batch, seq):
    tq = MOBA_QTILE * MOBA_BLOCK
    nt = seq // tq
    bh = batch * MOBA_HEADS
    og = og.reshape(bh, MOBA_TOPK, nt, tq, HEAD_DIM)

    def qmap(n, t):
        return ((n // MOBA_HEADS) * nt + t, n % MOBA_HEADS)

    return pl.pallas_call(
        _moba_merge_kernel,
        name="moba_merge",
        grid=(bh, nt),
        in_specs=[
            pl.BlockSpec((tq, HEAD_DIM), qmap),
            pl.BlockSpec((1, MOBA_QTILE, MOBA_BLOCK, AUG_W), lambda n, t: (n, t, 0, 0)),
            pl.BlockSpec((1, MOBA_QTILE, VT_ROWS, MOBA_BLOCK), lambda n, t: (n, t, 0, 0)),
            pl.BlockSpec((1, MOBA_TOPK, 1, tq, HEAD_DIM), lambda n, t: (n, 0, t, 0, 0)),
            pl.BlockSpec((1, 8, tq), lambda n, t: (n, 0, t)),
            pl.BlockSpec((1, 1, HEAD_DIM), lambda n, t: (n, 0, 0)),
        ],
        out_specs=pl.BlockSpec((tq, HEAD_DIM), qmap),
        out_shape=jax.ShapeDtypeStruct((batch * seq, MOBA_W), BF16),
        compiler_params=_cparams(("parallel", "parallel")),
    )(pa, kaug, vtb, og, sel, slopes)


def _moba_routed_attention(pa, kaug, vtb, kmean, slopes, batch, seq):
    nb = seq // MOBA_BLOCK
    bh = batch * MOBA_HEADS
    qrows, sel, rank, cnt = _moba_gate(pa, kmean, batch, seq)

    step_slots = SLOT_TILE * ROUTED_TILES_PER_STEP
    cap = -(-(MOBA_TOPK * seq + nb * SLOT_TILE) // step_slots) * step_slots
    tiles_cap = cap // SLOT_TILE
    counts = cnt[:, :, 0].astype(jnp.int32)
    padded = (counts + SLOT_TILE - 1) // SLOT_TILE * SLOT_TILE
    ends = jnp.cumsum(padded, axis=1)
    base = ends - padded + (jnp.arange(bh, dtype=jnp.int32) * cap)[:, None]
    n_tiles = ends[:, -1] // SLOT_TILE
    tile_idx = jnp.arange(tiles_cap, dtype=jnp.int32)
    tile_block = jnp.sum(ends[:, None, :] // SLOT_TILE <= tile_idx[None, :, None], axis=-1)
    tile_block = jnp.minimum(tile_block, nb - 1).astype(jnp.int32).reshape(bh * tiles_cap)
    base_b = jnp.broadcast_to(base.astype(F32)[:, :, None], (bh, nb, HEAD_DIM))

    n_rows = bh * cap + SLOT_TILE
    slots = _moba_slots(sel, rank, base_b, bh * cap).reshape(bh * 8, seq)
    qs = _sc_scatter_rows(qrows, slots, n_rows)
    part = _moba_routed(qs, kaug, vtb, tile_block, n_tiles, tiles_cap)
    og = _sc_gather_rows(part, slots)
    return _moba_merge(pa, kaug, vtb, og, sel, slopes, batch, seq)


def _split3(x):
    hi = x.astype(BF16)
    r1 = x - hi.astype(F32)
    mid = r1.astype(BF16)
    lo = (r1 - mid.astype(F32)).astype(BF16)
    return hi, mid, lo


def _hgrn_kernel(qb_ref, fb_ref, ib_ref, gb_ref, lbl_ref, gn_ref, o_ref, s_ref, oi_ref, st_ref,
                 *, layer):
    tt = qb_ref.shape[0]
    sub = HGRN_SUB
    hd = HEAD_DIM

    @pl.when(pl.program_id(1) == 0)
    def _():
        s_ref[...] = jnp.zeros_like(s_ref)

    logits = lbl_ref[...]
    e = jnp.exp(logits - jnp.max(logits, axis=0, keepdims=True))
    pl_ = e / jnp.sum(e, axis=0, keepdims=True)
    lb = jnp.sum(pl_[0:layer + 1], axis=0, keepdims=True) - pl_[0:1]

    fb = fb_ref[...].astype(F32)
    t = jnp.exp(-jnp.abs(fb))
    r = 1.0 / (1.0 + t)
    sig_pos = jnp.where(fb >= 0, r, t * r)
    sig_neg = jnp.where(fb >= 0, t * r, r)
    f_gate = lb + (1.0 - lb) * sig_pos
    logf = jnp.log(jnp.maximum(f_gate, F_MIN))
    k_all = (1.0 - lb) * sig_neg
    qb = qb_ref[...].astype(F32)
    q_all = qb * jax.nn.sigmoid(qb)
    v_all = ib_ref[...].astype(F32)

    ri = lax.broadcasted_iota(jnp.int32, (tt, tt), 0)
    ci = lax.broadcasted_iota(jnp.int32, (tt, tt), 1)
    same = (ri // sub) == (ci // sub)
    tri = jnp.where(same & (ci <= ri), 1.0, 0.0).astype(BF16)
    hi, mid, lo = _split3(logf)
    gl = _dot(tri, hi) + _dot(tri, mid) + _dot(tri, lo)

    half = sub // 2
    nsub = tt // sub
    nv = tt // half
    lane_sum = jnp.ones((hd, hd), BF16)
    row_in = lax.broadcasted_iota(jnp.int32, (nv, half, hd), 1)

    def pair_terms(qx, gx, kx, vx, causal):
        n = qx.shape[0]
        out = jnp.zeros_like(qx)
        for rho in range(half):
            kr = pltpu.roll(kx, rho, 1) if rho else kx
            vr = pltpu.roll(vx, rho, 1) if rho else vx
            gr = pltpu.roll(gx[1], rho, 1) if rho else gx[1]
            prod = qx * kr * jnp.exp(gx[0] - gr)
            if causal and rho:
                prod = jnp.where(row_in >= rho, prod, 0.0)
            a = _dot(prod.reshape(n * half, hd).astype(BF16), lane_sum)
            out = out + a.reshape(n, half, hd) * vr
        return out

    for h in range(HGRN_HEADS):
        cs = slice(h * hd, (h + 1) * hd)
        q = q_all[:, cs]
        k = k_all[:, cs]
        v = v_all[:, cs]
        g = gl[:, cs]

        q3, k3, v3, g3 = (a.reshape(nv, half, hd) for a in (q, k, v, g))
        od = pair_terms(q3, (g3, g3), k3, v3, True).reshape(nsub, 2, half, hd)
        q4, k4, v4, g4 = (a.reshape(nsub, 2, half, hd) for a in (q, k, v, g))
        oh = pair_terms(q4[:, 1], (g4[:, 1], g4[:, 0]), k4[:, 0], v4[:, 0], False)
        o_diag = jnp.stack([od[:, 0], od[:, 1] + oh], axis=1).reshape(tt, hd)

        gs = g.reshape(nsub, sub, hd)
        g_end = gs[:, sub - 1:sub, :]
        qd = (q * jnp.exp(g)).astype(BF16).reshape(nsub, sub, hd)
        kd = (k.reshape(nsub, sub, hd) * jnp.exp(g_end - gs)).astype(BF16)
        vb = v.astype(BF16).reshape(nsub, sub, hd)
        dec = jnp.exp(g_end)
        upd = [_dot_t0(vb[c], kd[c]) for c in range(nsub)]
        st = s_ref[h]
        for c in range(nsub):
            st_ref[c] = st.astype(BF16)
            st = st * dec[c] + upd[c]
        s_ref[h] = st
        for c in range(nsub):
            oi_ref[c * sub:(c + 1) * sub, cs] = lax.dot_general(
                qd[c], st_ref[c], (((1,), (1,)), ((), ())), preferred_element_type=F32)

        o = oi_ref[:, cs] + o_diag
        rr = lax.rsqrt(jnp.mean(o * o, axis=-1, keepdims=True) + RMS_EPS)
        gate = jax.nn.sigmoid(gb_ref[:, cs].astype(F32))
        o_ref[:, cs] = (o * rr * gn_ref[...] * gate).astype(o_ref.dtype)


def _hgrn(pa, lb_logits, out_norm, layer, batch, seq):
    tt = min(256, seq)
    nt = seq // tt
    c0 = 3 * MOBA_W // HGRN_W
    nl = lb_logits.shape[0]

    def cmap(off):
        return lambda b, t: (b * nt + t, c0 + off)

    return pl.pallas_call(
        functools.partial(_hgrn_kernel, layer=layer),
        name="hgrn",
        grid=(batch, nt),
        in_specs=[
            pl.BlockSpec((tt, HGRN_W), cmap(0)),
            pl.BlockSpec((tt, HGRN_W), cmap(1)),
            pl.BlockSpec((tt, HGRN_W), cmap(2)),
            pl.BlockSpec((tt, HGRN_W), cmap(3)),
            pl.BlockSpec((nl, HGRN_W), lambda b, t: (0, 0)),
            pl.BlockSpec((1, HEAD_DIM), lambda b, t: (0, 0)),
        ],
        out_specs=pl.BlockSpec((tt, HGRN_W), lambda b, t: (b * nt + t, 0)),
        out_shape=jax.ShapeDtypeStruct((batch * seq, HGRN_W), BF16),
        scratch_shapes=[pltpu.VMEM((HGRN_HEADS, HEAD_DIM, HEAD_DIM), F32),
                        pltpu.VMEM((tt, HGRN_W), F32),
                        pltpu.VMEM((tt // HGRN_SUB, HEAD_DIM, HEAD_DIM), BF16)],
        compiler_params=_cparams(("parallel", "arbitrary")),
    )(pa, pa, pa, pa, lb_logits, out_norm.reshape(1, HEAD_DIM))


def _memkv_kernel(mem_ref, g_ref, w_ref, kt_ref, v_ref):
    h = _rms(mem_ref[0], g_ref[...]).astype(BF16)
    kv = _dot(h, w_ref[...])
    kt_ref[0] = kv[:, 0:MEM_W].T.astype(BF16)
    v_ref[0] = kv[:, MEM_W:].astype(BF16)


def _memkv(mem, g, w):
    b, m, d = mem.shape
    return pl.pallas_call(
        _memkv_kernel,
        name="memkv",
        grid=(b,),
        in_specs=[
            pl.BlockSpec((1, m, d), lambda i: (i, 0, 0)),
            pl.BlockSpec((1, d), lambda i: (0, 0)),
            pl.BlockSpec((d, 2 * MEM_W), lambda i: (0, 0)),
        ],
        out_specs=[
            pl.BlockSpec((1, MEM_W, m), lambda i: (i, 0, 0)),
            pl.BlockSpec((1, m, MEM_W), lambda i: (i, 0, 0)),
        ],
        out_shape=[
            jax.ShapeDtypeStruct((b, MEM_W, m), BF16),
            jax.ShapeDtypeStruct((b, m, MEM_W), BF16),
        ],
        compiler_params=_cparams(("parallel",)),
    )(mem, g.reshape(1, d), w)


def _merge_kernel(x_ref, oa_ref, ob_ref, qm_ref, g0_ref, g1_ref, g2_ref, mkt_ref, mv_ref,
                  wa_ref, wb_ref, wm_ref, wo_ref, o_ref, om_ref):
    scale = HEAD_DIM ** -0.5
    for h in range(MEM_HEADS):
        cs = slice(h * HEAD_DIM, (h + 1) * HEAD_DIM)
        s = _dot(qm_ref[:, cs], mkt_ref[0, cs, :]) * scale
        p = jnp.exp(s - jnp.max(s, axis=-1, keepdims=True))
        l = jnp.sum(p, axis=-1, keepdims=True)
        om_ref[:, cs] = (_dot(p.astype(BF16), mv_ref[0, :, cs]) / l).astype(BF16)

    y = jax.nn.sigmoid(g0_ref[...].astype(F32)) * _dot(oa_ref[...], wa_ref[...])
    y += jax.nn.sigmoid(g1_ref[...].astype(F32)) * _dot(ob_ref[...], wb_ref[...])
    y += jax.nn.sigmoid(g2_ref[...].astype(F32)) * _dot(om_ref[...], wm_ref[...])
    o_ref[...] = x_ref[...] + _dot(y.astype(BF16), wo_ref[...])


def _merge(x, oa, ob, pa, gates, mkt, mv, wa, wb, wm, wo, batch, seq):
    n, d = x.shape
    m = mv.shape[1]
    tm = _pick(seq, 256)
    per_b = seq // tm
    qcol = (3 * MOBA_W + 4 * HGRN_W) // MEM_W
    row = lambda i: (i, 0)
    return pl.pallas_call(
        _merge_kernel,
        name="merge",
        grid=(n // tm,),
        in_specs=[
            pl.BlockSpec((tm, d), row),
            pl.BlockSpec((tm, MOBA_W), row),
            pl.BlockSpec((tm, HGRN_W), row),
            pl.BlockSpec((tm, MEM_W), lambda i: (i, qcol)),
            pl.BlockSpec((tm, d), lambda i: (i, 0)),
            pl.BlockSpec((tm, d), lambda i: (i, 1)),
            pl.BlockSpec((tm, d), lambda i: (i, 2)),
            pl.BlockSpec((1, MEM_W, m), lambda i: (i // per_b, 0, 0)),
            pl.BlockSpec((1, m, MEM_W), lambda i: (i // per_b, 0, 0)),
            _const_spec((MOBA_W, d)),
            _const_spec((HGRN_W, d)),
            _const_spec((MEM_W, d)),
            _const_spec((d, d)),
        ],
        out_specs=pl.BlockSpec((tm, d), row),
        out_shape=jax.ShapeDtypeStruct((n, d), F32),
        scratch_shapes=[pltpu.VMEM((tm, MEM_W), BF16)],
        compiler_params=_cparams(("parallel",)),
    )(x, oa, ob, pa, gates, gates, gates, mkt, mv, wa, wb, wm, wo)


def kernel(x, mem, ffn1_norm, ffn1_w1, ffn1_w3, ffn1_w2, mix_norm, w_in, hgrn_lb_logits,
           hgrn_out_norm, mem_norm, w_mem_kv, w_proj_moba, w_proj_hgrn, w_proj_mem, w_out,
           ffn2_norm, ffn2_w1, ffn2_w3, ffn2_w2, final_norm):
    batch, seq, d = x.shape
    depth = ffn1_w1.shape[0]
    assert seq % (MOBA_QTILE * MOBA_BLOCK) == 0
    assert w_in.shape[-1] == MIX_W + 3 * d
    bf = lambda a: a.astype(BF16)

    hs = jnp.arange(1, MOBA_HEADS + 1, dtype=F32)
    slopes = jnp.tile(jnp.exp2(-8.0 * hs / MOBA_HEADS), batch)
    slopes = jnp.broadcast_to(slopes[:, None, None], (batch * MOBA_HEADS, 1, HEAD_DIM))
    mix_scale = jnp.concatenate([jnp.full((MOBA_W,), HEAD_DIM ** -0.5, F32),
                                 jnp.ones((MIX_W - MOBA_W,), F32)])
    gate_scale = jnp.ones((3 * d,), F32)

    xs = x.reshape(batch * seq, d)
    for l in range(depth):
        last = l == depth - 1
        xs = _ffn(xs, ffn1_norm[l], bf(ffn1_w1[l]), bf(ffn1_w3[l]), bf(ffn1_w2[l]),
                  final_norm, final=False)

        w_l = w_in[l]
        pa = _normproj(xs, mix_norm[l], bf(w_l[:, :MIX_W]), mix_scale)
        gates = _normproj(xs, mix_norm[l], bf(w_l[:, MIX_W:]), gate_scale)

        kaug, vtb, kmean = _moba_prep(pa, slopes, batch, seq)
        oa = _moba_routed_attention(pa, kaug, vtb, kmean, slopes, batch, seq)
        ob = _hgrn(pa, hgrn_lb_logits, hgrn_out_norm[l], l, batch, seq)
        mkt, mv = _memkv(mem, mem_norm[l], bf(w_mem_kv[l]))
        xs = _merge(xs, oa, ob, pa, gates, mkt, mv, bf(w_proj_moba[l]), bf(w_proj_hgrn[l]),
                    bf(w_proj_mem[l]), bf(w_out[l]), batch, seq)

        xs = _ffn(xs, ffn2_norm[l], bf(ffn2_w1[l]), bf(ffn2_w3[l]), bf(ffn2_w2[l]),
                  final_norm, final=last)
    return xs.reshape(batch, seq, d)
```

```python
import functools

import jax
import jax.numpy as jnp
from jax import lax
from jax.experimental import pallas as pl
from jax.experimental.pallas import tpu as pltpu
from jax.experimental.pallas import tpu_sc as plsc

F32 = jnp.float32
BF16 = jnp.bfloat16

HEAD_DIM = 128
MOBA_HEADS = 8
MOBA_BLOCK = 256
MOBA_TOPK = 3
HGRN_HEADS = 4
MEM_HEADS = 4
RMS_EPS = 1e-6
NEG_INF = -1e30
F_MIN = 1e-20

MOBA_W = MOBA_HEADS * HEAD_DIM
HGRN_W = HGRN_HEADS * HEAD_DIM
MEM_W = MEM_HEADS * HEAD_DIM
MIX_W = 3 * MOBA_W + 4 * HGRN_W + MEM_W

VMEM_LIMIT_BYTES = 60 * 1024 * 1024

AUG_W = 2 * HEAD_DIM
AUG_BIAS = HEAD_DIM
VT_ROWS = HEAD_DIM + 16
MOBA_QTILE = 4

HGRN_SUB = 16
FFN_OUT_CHUNK = 512
SLOT_TILE = 512
ROUTED_TILES_PER_STEP = 8
SC_WINDOW = 128


def _cparams(sem):
    return pltpu.CompilerParams(dimension_semantics=sem, vmem_limit_bytes=VMEM_LIMIT_BYTES)


def _dot(a, b):
    return jnp.dot(a, b, preferred_element_type=F32)


def _dot_t0(a, b):
    return lax.dot_general(a, b, (((0,), (0,)), ((), ())), preferred_element_type=F32)


def _rms(xf, g):
    r = lax.rsqrt(jnp.mean(xf * xf, axis=-1, keepdims=True) + RMS_EPS)
    return xf * r * g


def _pick(n, want):
    if n <= want:
        return n
    t = (want // 128) * 128
    while t >= 128:
        if n % t == 0:
            return t
        t -= 128
    return n


def _const_spec(shape):
    nd = len(shape)
    return pl.BlockSpec(shape, lambda *_: (0,) * nd)


def _ffn_kernel(x_ref, g_ref, w1_ref, w3_ref, w2_ref, fg_ref, o_ref, h_ref, *, final):
    j = pl.program_id(1)

    @pl.when(j == 0)
    def _():
        xf = x_ref[...]
        h_ref[...] = _rms(xf, g_ref[...]).astype(BF16)
        o_ref[...] = xf

    h = h_ref[...]
    u = _dot(h, w1_ref[...])
    v = _dot(h, w3_ref[...])
    a = (0.5 * u * jax.nn.sigmoid(u) * v).astype(BF16)
    tn = FFN_OUT_CHUNK if o_ref.shape[1] % FFN_OUT_CHUNK == 0 else o_ref.shape[1]
    for c in range(o_ref.shape[1] // tn):
        cs = slice(c * tn, (c + 1) * tn)
        o_ref[:, cs] += _dot(a, w2_ref[:, cs])

    if final:
        @pl.when(j == pl.num_programs(1) - 1)
        def _():
            o_ref[...] = _rms(o_ref[...], fg_ref[...])


def _ffn(x, g, w1, w3, w2, final_g, *, final):
    n, d = x.shape
    dff = w1.shape[1]
    tm = _pick(n, 1024)
    tf = _pick(dff, 512)
    return pl.pallas_call(
        functools.partial(_ffn_kernel, final=final),
        name="ffn_final" if final else "ffn",
        grid=(n // tm, dff // tf),
        in_specs=[
            pl.BlockSpec((tm, d), lambda i, j: (i, 0), pipeline_mode=pl.Buffered(1)),
            pl.BlockSpec((1, d), lambda i, j: (0, 0)),
            pl.BlockSpec((d, tf), lambda i, j: (0, j)),
            pl.BlockSpec((d, tf), lambda i, j: (0, j)),
            pl.BlockSpec((tf, d), lambda i, j: (j, 0)),
            pl.BlockSpec((1, d), lambda i, j: (0, 0)),
        ],
        out_specs=pl.BlockSpec((tm, d), lambda i, j: (i, 0)),
        out_shape=jax.ShapeDtypeStruct((n, d), F32),
        scratch_shapes=[pltpu.VMEM((tm, d), BF16)],
        compiler_params=_cparams(("parallel", "arbitrary")),
    )(x, g.reshape(1, d), w1, w3, w2, final_g.reshape(1, d))


def _normproj_kernel(x_ref, g_ref, w_ref, s_ref, o_ref, h_ref, *, tn):
    h_ref[...] = _rms(x_ref[...], g_ref[...]).astype(BF16)
    for c in range(w_ref.shape[1] // tn):
        cs = slice(c * tn, (c + 1) * tn)
        o_ref[:, cs] = (_dot(h_ref[...], w_ref[:, cs]) * s_ref[:, cs]).astype(o_ref.dtype)


def _normproj(x, g, w, col_scale):
    n, d = x.shape
    nout = w.shape[1]
    tm = _pick(n, 512)
    tn = _pick(nout, 512)
    return pl.pallas_call(
        functools.partial(_normproj_kernel, tn=tn),
        name="normproj",
        grid=(n // tm,),
        in_specs=[
            pl.BlockSpec((tm, d), lambda i: (i, 0)),
            pl.BlockSpec((1, d), lambda i: (0, 0)),
            pl.BlockSpec((d, nout), lambda i: (0, 0), pipeline_mode=pl.Buffered(1)),
            pl.BlockSpec((1, nout), lambda i: (0, 0)),
        ],
        out_specs=pl.BlockSpec((tm, nout), lambda i: (i, 0)),
        out_shape=jax.ShapeDtypeStruct((n, nout), BF16),
        scratch_shapes=[pltpu.VMEM((tm, d), BF16)],
        compiler_params=_cparams(("parallel",)),
    )(x, g.reshape(1, d), w, col_scale.reshape(1, nout))


def _moba_prep_kernel(k_ref, v_ref, slope_ref, kaug_ref, vtb_ref, kmean_ref):
    g = pl.program_id(1)
    bs = MOBA_BLOCK
    nblk = k_ref.shape[0] // bs
    slope = slope_ref[0, 0:1, 0:1]
    lane = lax.broadcasted_iota(jnp.int32, (bs, AUG_W - HEAD_DIM), 1)
    row = lax.broadcasted_iota(jnp.int32, (bs, AUG_W - HEAD_DIM), 0).astype(F32)
    extra = jnp.where(lane == AUG_BIAS - HEAD_DIM, slope * row, 0.0).astype(BF16)
    orow = lax.broadcasted_iota(jnp.int32, (VT_ROWS - HEAD_DIM, bs), 0)
    ones_row = jnp.where(orow == 0, 1.0, 0.0).astype(BF16)
    for u in range(nblk):
        k = k_ref[u * bs:(u + 1) * bs, :]
        kmean_ref[0, pl.ds(g * nblk + u, 1), :] = jnp.mean(k.astype(F32), axis=0, keepdims=True)
        kaug_ref[0, u, :, 0:HEAD_DIM] = k
        kaug_ref[0, u, :, HEAD_DIM:AUG_W] = extra
        vt = v_ref[u * bs:(u + 1) * bs, :].astype(F32).T
        vtb_ref[0, u, 0:HEAD_DIM, :] = vt.astype(BF16)
        vtb_ref[0, u, HEAD_DIM:VT_ROWS, :] = ones_row


def _moba_prep(pa, slopes, batch, seq):
    nb = seq // MOBA_BLOCK
    grp = MOBA_QTILE
    ng = nb // grp
    bh = batch * MOBA_HEADS
    kcol = MOBA_W // HEAD_DIM
    vcol = 2 * MOBA_W // HEAD_DIM

    def kmap(n, g):
        return ((n // MOBA_HEADS) * ng + g, kcol + n % MOBA_HEADS)

    def vmap(n, g):
        return ((n // MOBA_HEADS) * ng + g, vcol + n % MOBA_HEADS)

    return pl.pallas_call(
        _moba_prep_kernel,
        name="moba_prep",
        grid=(bh, ng),
        in_specs=[
            pl.BlockSpec((grp * MOBA_BLOCK, HEAD_DIM), kmap),
            pl.BlockSpec((grp * MOBA_BLOCK, HEAD_DIM), vmap),
            pl.BlockSpec((1, 1, HEAD_DIM), lambda n, g: (n, 0, 0)),
        ],
        out_specs=[
            pl.BlockSpec((1, grp, MOBA_BLOCK, AUG_W), lambda n, g: (n, g, 0, 0)),
            pl.BlockSpec((1, grp, VT_ROWS, MOBA_BLOCK), lambda n, g: (n, g, 0, 0)),
            pl.BlockSpec((1, nb, HEAD_DIM), lambda n, g: (n, 0, 0)),
        ],
        out_shape=[
            jax.ShapeDtypeStruct((bh, nb, MOBA_BLOCK, AUG_W), BF16),
            jax.ShapeDtypeStruct((bh, nb, VT_ROWS, MOBA_BLOCK), BF16),
            jax.ShapeDtypeStruct((bh, nb, HEAD_DIM), F32),
        ],
        compiler_params=_cparams(("parallel", "arbitrary")),
    )(pa, pa, slopes)


def _moba_gate_kernel(q_ref, kmean_ref, qrow_ref, sel_ref, rank_ref, cnt_ref, before_ref):
    t = pl.program_id(1)
    bs = MOBA_BLOCK
    tq = q_ref.shape[0]
    nb = kmean_ref.shape[1]

    q = q_ref[...]
    qrow_ref[...] = q.astype(F32)
    qt = q.astype(F32).T.astype(BF16)

    lane = lax.broadcasted_iota(jnp.int32, (1, tq), 1)
    own = t * (tq // bs) + lane // bs
    km = kmean_ref[0]
    km_hi = km.astype(BF16)
    km_lo = (km - km_hi.astype(F32)).astype(BF16)
    gate = _dot(km_hi, qt) + _dot(km_lo, qt)
    blk = lax.broadcasted_iota(jnp.int32, (nb, tq), 0)
    gate = jnp.where(blk < own, gate, NEG_INF)

    @pl.when(t == 0)
    def _():
        cnt_ref[...] = jnp.zeros_like(cnt_ref)
        qi = lax.broadcasted_iota(jnp.int32, (tq, tq), 0)
        qj = lax.broadcasted_iota(jnp.int32, (tq, tq), 1)
        before_ref[...] = jnp.where(qi < qj, 1.0, 0.0).astype(BF16)

    run = cnt_ref[0][:, 0:1]
    before = before_ref[...]
    sels, ranks = [], []
    for r in range(MOBA_TOPK):
        mx = jnp.max(gate, axis=0, keepdims=True)
        first = jnp.min(jnp.where(gate == mx, blk, nb), axis=0, keepdims=True)
        hit = blk == first
        gate = jnp.where(hit, -jnp.inf, gate)
        valid = own > r
        oh = jnp.where(hit & valid, 1.0, 0.0)
        prior = _dot(oh.astype(BF16), before)
        ranks.append(jnp.sum(oh * (run + prior), axis=0, keepdims=True))
        run = run + jnp.sum(oh, axis=1, keepdims=True)
        sels.append(jnp.where(valid, first, -1))
    cnt_ref[0] = jnp.broadcast_to(run, cnt_ref.shape[1:])
    pad = jnp.zeros((8 - MOBA_TOPK, tq), jnp.int32)
    sel_ref[0] = jnp.concatenate(sels + [pad], axis=0)
    rank_ref[0] = jnp.concatenate([x.astype(jnp.int32) for x in ranks] + [pad], axis=0)


def _moba_gate(pa, kmean, batch, seq):
    nb = seq // MOBA_BLOCK
    tq = min(MOBA_QTILE * MOBA_BLOCK, seq)
    nt = seq // tq
    bh = batch * MOBA_HEADS

    def qmap(n, t):
        return ((n // MOBA_HEADS) * nt + t, n % MOBA_HEADS)

    return pl.pallas_call(
        _moba_gate_kernel,
        name="moba_gate",
        grid=(bh, nt),
        in_specs=[
            pl.BlockSpec((tq, HEAD_DIM), qmap),
            pl.BlockSpec((1, nb, HEAD_DIM), lambda n, t: (n, 0, 0)),
        ],
        out_specs=[
            pl.BlockSpec((tq, HEAD_DIM), lambda n, t: (n * nt + t, 0)),
            pl.BlockSpec((1, 8, tq), lambda n, t: (n, 0, t)),
            pl.BlockSpec((1, 8, tq), lambda n, t: (n, 0, t)),
            pl.BlockSpec((1, nb, HEAD_DIM), lambda n, t: (n, 0, 0)),
        ],
        out_shape=[
            jax.ShapeDtypeStruct((bh * seq, HEAD_DIM), F32),
            jax.ShapeDtypeStruct((bh, 8, seq), jnp.int32),
            jax.ShapeDtypeStruct((bh, 8, seq), jnp.int32),
            jax.ShapeDtypeStruct((bh, nb, HEAD_DIM), F32),
        ],
        scratch_shapes=[pltpu.VMEM((tq, tq), BF16)],
        compiler_params=_cparams(("parallel", "arbitrary")),
    )(pa, kmean)


def _moba_slot_kernel(sel_ref, rank_ref, base_ref, slot_ref, *, trash):
    nb = base_ref.shape[1]
    tq = sel_ref.shape[2]
    base = base_ref[0][:, 0:1]
    blk = lax.broadcasted_iota(jnp.int32, (nb, tq), 0)
    spare = trash + lax.broadcasted_iota(jnp.int32, (1, tq), 1) % SLOT_TILE
    rows = []
    for r in range(MOBA_TOPK):
        sel = sel_ref[0, r:r + 1, :]
        start = jnp.sum(jnp.where(blk == sel, base, 0.0), axis=0, keepdims=True)
        slot = start.astype(jnp.int32) + rank_ref[0, r:r + 1, :]
        rows.append(jnp.where(sel >= 0, slot, spare))
    rows.append(jnp.broadcast_to(spare, (8 - MOBA_TOPK, tq)))
    slot_ref[0] = jnp.concatenate(rows, axis=0)


def _moba_slots(sel, rank, base, trash):
    bh, _, seq = sel.shape
    nb = base.shape[1]
    tq = min(2048, seq)
    spec = pl.BlockSpec((1, 8, tq), lambda n, t: (n, 0, t))
    return pl.pallas_call(
        functools.partial(_moba_slot_kernel, trash=trash),
        name="moba_slots",
        grid=(bh, seq // tq),
        in_specs=[spec, spec, pl.BlockSpec((1, nb, HEAD_DIM), lambda n, t: (n, 0, 0))],
        out_specs=spec,
        out_shape=jax.ShapeDtypeStruct((bh, 8, seq), jnp.int32),
        compiler_params=_cparams(("parallel", "parallel")),
    )(sel, rank, base)


def _sc_mesh():
    return plsc.VectorSubcoreMesh(core_axis_name="core", subcore_axis_name="subcore")


def _sc_scatter_rows(rows, slots, n_out):
    bh8, seq = slots.shape
    bh = bh8 // 8
    nw = seq // SC_WINDOW

    @pl.kernel(out_type=jax.ShapeDtypeStruct((n_out, HEAD_DIM), rows.dtype), mesh=_sc_mesh(),
               scratch_types=[])
    def scatter(x_hbm, i_hbm, o_hbm):
        def body(x_vmem, i_vmem):
            pltpu.sync_copy(x_vmem, o_hbm.at[i_vmem.at[0]])

        pltpu.emit_pipeline(
            body,
            grid=(bh * MOBA_TOPK * nw,),
            in_specs=[
                pl.BlockSpec((SC_WINDOW, HEAD_DIM),
                             index_map=lambda i: ((i // (MOBA_TOPK * nw)) * nw + i % nw, 0)),
                pl.BlockSpec((1, SC_WINDOW),
                             index_map=lambda i: ((i // (MOBA_TOPK * nw)) * 8 + (i // nw) % MOBA_TOPK,
                                                  i % nw)),
            ],
            out_specs=[],
            core_axis_name=("core", "subcore"),
            dimension_semantics=(pltpu.PARALLEL,),
        )(x_hbm, i_hbm)

    return scatter(rows, slots)


def _sc_gather_rows(table, slots):
    bh8, seq = slots.shape
    bh = bh8 // 8
    nw = seq // SC_WINDOW
    n_out = bh * MOBA_TOPK * seq

    @pl.kernel(out_type=jax.ShapeDtypeStruct((n_out, HEAD_DIM), table.dtype), mesh=_sc_mesh())
    def gather(x_hbm, i_hbm, o_hbm):
        def body(i_vmem, o_vmem):
            pltpu.sync_copy(x_hbm.at[i_vmem.at[0]], o_vmem)

        pltpu.emit_pipeline(
            body,
            grid=(bh * MOBA_TOPK * nw,),
            in_specs=[
                pl.BlockSpec((1, SC_WINDOW),
                             index_map=lambda i: ((i // (MOBA_TOPK * nw)) * 8 + (i // nw) % MOBA_TOPK,
                                                  i % nw)),
            ],
            out_specs=[pl.BlockSpec((SC_WINDOW, HEAD_DIM), index_map=lambda i: (i, 0))],
            core_axis_name=("core", "subcore"),
            dimension_semantics=(pltpu.PARALLEL,),
        )(i_hbm, o_hbm)

    return gather(table, slots)


def _moba_routed_kernel(tb_ref, nt_ref, qs_ref, kaug_ref, vtb_ref, o_ref, sa_ref, sb_ref):
    n = pl.program_id(0)
    g = pl.program_id(1)
    tpg = qs_ref.shape[0] // SLOT_TILE
    tiles_cap = pl.num_programs(1) * tpg

    @pl.when(g * tpg < nt_ref[n])
    def _():
        row = lax.broadcasted_iota(jnp.int32, (AUG_W - HEAD_DIM, SLOT_TILE), 0)
        tail = jnp.where(row == AUG_BIAS - HEAD_DIM, 1.0, 0.0).astype(BF16)
        frow = lax.broadcasted_iota(jnp.int32, (HEAD_DIM, SLOT_TILE), 0)
        blocks = [tb_ref[n * tiles_cap + g * tpg + u] for u in range(tpg)]

        def scores(u):
            qt = qs_ref[u * SLOT_TILE:(u + 1) * SLOT_TILE, :].T.astype(BF16)
            return _dot(kaug_ref[0, blocks[u]], jnp.concatenate([qt, tail], axis=0))

        bufs = (sa_ref, sb_ref)
        sa_ref[...] = scores(0)
        for u in range(tpg):
            if u + 1 < tpg:
                bufs[(u + 1) % 2][...] = scores(u + 1)
            s = bufs[u % 2][...]
            m = jnp.max(s, axis=0, keepdims=True)
            p = jnp.exp(s - m).astype(BF16)
            acc = _dot(vtb_ref[0, blocks[u]], p)
            l = acc[HEAD_DIM:HEAD_DIM + 1]
            o = (acc[0:HEAD_DIM] / l).astype(BF16).astype(F32)
            lse = m + jnp.log(l)
            ob = lax.bitcast_convert_type(o, jnp.uint32)
            lb = lax.bitcast_convert_type(lse, jnp.uint32)
            extra = jnp.where(frow == 0, lb >> 16, jnp.where(frow == 1, lb & 0xFFFF, 0))
            packed = lax.bitcast_convert_type(ob | extra, F32)
            o_ref[u * SLOT_TILE:(u + 1) * SLOT_TILE, :] = packed.T


def _moba_routed(qs, kaug, vtb, tile_block, n_tiles, tiles_cap):
    bh, nb = kaug.shape[0], kaug.shape[1]
    tpg = ROUTED_TILES_PER_STEP
    steps = tiles_cap // tpg
    rows = tpg * SLOT_TILE

    def qmap(n, g, tb, nt):
        used = jnp.maximum((nt[n] + tpg - 1) // tpg, 1)
        return (n * steps + jnp.minimum(g, used - 1), 0)

    grid_spec = pltpu.PrefetchScalarGridSpec(
        num_scalar_prefetch=2,
        grid=(bh, steps),
        in_specs=[
            pl.BlockSpec((rows, HEAD_DIM), qmap),
            pl.BlockSpec((1, nb, MOBA_BLOCK, AUG_W), lambda n, g, tb, nt: (n, 0, 0, 0)),
            pl.BlockSpec((1, nb, VT_ROWS, MOBA_BLOCK), lambda n, g, tb, nt: (n, 0, 0, 0)),
        ],
        out_specs=pl.BlockSpec((rows, HEAD_DIM), qmap),
        scratch_shapes=[pltpu.VMEM((MOBA_BLOCK, SLOT_TILE), F32),
                        pltpu.VMEM((MOBA_BLOCK, SLOT_TILE), F32)],
    )
    return pl.pallas_call(
        _moba_routed_kernel,
        name="moba_routed",
        grid_spec=grid_spec,
        out_shape=jax.ShapeDtypeStruct((qs.shape[0], HEAD_DIM), F32),
        compiler_params=_cparams(("parallel", "arbitrary")),
    )(tile_block, n_tiles, qs, kaug, vtb)


def _moba_merge_kernel(q_ref, kaug_ref, vtb_ref, og_ref, sel_ref, slope_ref, o_ref):
    t = pl.program_id(1)
    bs = MOBA_BLOCK
    tq = q_ref.shape[0]
    slope = slope_ref[0, 0:1, 0:1]

    qt = q_ref[...].astype(F32).T.astype(BF16)
    lane = lax.broadcasted_iota(jnp.int32, (1, tq), 1)
    tpos = (t * tq + lane).astype(F32)
    lk = lax.broadcasted_iota(jnp.int32, (bs, bs), 0)
    lq = lax.broadcasted_iota(jnp.int32, (bs, bs), 1)
    dist = (lq - lk).astype(F32)
    accs, ms = [], []
    for u in range(tq // bs):
        s = _dot(kaug_ref[0, u, :, 0:HEAD_DIM], qt[:, u * bs:(u + 1) * bs])
        s = jnp.where(dist >= 0.0, s - slope * dist, NEG_INF)
        m_u = jnp.max(s, axis=0, keepdims=True)
        p = jnp.exp(s - m_u).astype(BF16)
        accs.append(_dot(vtb_ref[0, u], p))
        ms.append(m_u)
    acc = jnp.concatenate(accs, axis=1)
    l = acc[HEAD_DIM:HEAD_DIM + 1]
    parts = [acc[0:HEAD_DIM] / l]
    lses = [jnp.concatenate(ms, axis=1) + jnp.log(l)]

    for r in range(MOBA_TOPK):
        u = lax.bitcast_convert_type(og_ref[0, r, 0].T, jnp.uint32)
        lb = ((u[0:1] & 0xFFFF) << 16) | (u[1:2] & 0xFFFF)
        lse_r = lax.bitcast_convert_type(lb, F32)
        sel_r = sel_ref[0, r:r + 1, :]
        valid = sel_r >= 0
        lses.append(jnp.where(valid, lse_r + slope * ((sel_r * bs).astype(F32) - tpos), NEG_INF))
        hi_half = lax.bitcast_convert_type((u >> 16) << 16, F32)
        parts.append(jnp.where(valid, hi_half, 0.0))
    mx = functools.reduce(jnp.maximum, lses)
    ws = [jnp.exp(x - mx) for x in lses]
    num = functools.reduce(lambda a, b: a + b, [w * o for w, o in zip(ws, parts)])
    out_t = num / functools.reduce(lambda a, b: a + b, ws)
    o_ref[...] = out_t.T.astype(o_ref.dtype)


def _moba_merge(pa, kaug, vtb, og, sel, slopes, batch, seq):
    tq = MOBA_QTILE * MOBA_BLOCK
    nt = seq // tq
    bh = batch * MOBA_HEADS
    og = og.reshape(bh, MOBA_TOPK, nt, tq, HEAD_DIM)

    def qmap(n, t):
        return ((n // MOBA_HEADS) * nt + t, n % MOBA_HEADS)

    return pl.pallas_call(
        _moba_merge_kernel,
        name="moba_merge",
        grid=(bh, nt),
        in_specs=[
            pl.BlockSpec((tq, HEAD_DIM), qmap),
            pl.BlockSpec((1, MOBA_QTILE, MOBA_BLOCK, AUG_W), lambda n, t: (n, t, 0, 0)),
            pl.BlockSpec((1, MOBA_QTILE, VT_ROWS, MOBA_BLOCK), lambda n, t: (n, t, 0, 0)),
            pl.BlockSpec((1, MOBA_TOPK, 1, tq, HEAD_DIM), lambda n, t: (n, 0, t, 0, 0)),
            pl.BlockSpec((1, 8, tq), lambda n, t: (n, 0, t)),
            pl.BlockSpec((1, 1, HEAD_DIM), lambda n, t: (n, 0, 0)),
        ],
        out_specs=pl.BlockSpec((tq, HEAD_DIM), qmap),
        out_shape=jax.ShapeDtypeStruct((batch * seq, MOBA_W), BF16),
        compiler_params=_cparams(("parallel", "parallel")),
    )(pa, kaug, vtb, og, sel, slopes)


def _moba_routed_attention(pa, kaug, vtb, kmean, slopes, batch, seq, companion):
    nb = seq // MOBA_BLOCK
    bh = batch * MOBA_HEADS
    qrows, sel, rank, cnt = _moba_gate(pa, kmean, batch, seq)

    step_slots = SLOT_TILE * ROUTED_TILES_PER_STEP
    cap = -(-(MOBA_TOPK * seq + nb * SLOT_TILE) // step_slots) * step_slots
    tiles_cap = cap // SLOT_TILE
    counts = cnt[:, :, 0].astype(jnp.int32)
    padded = (counts + SLOT_TILE - 1) // SLOT_TILE * SLOT_TILE
    ends = jnp.cumsum(padded, axis=1)
    base = ends - padded + (jnp.arange(bh, dtype=jnp.int32) * cap)[:, None]
    n_tiles = ends[:, -1] // SLOT_TILE
    tile_idx = jnp.arange(tiles_cap, dtype=jnp.int32)
    tile_block = jnp.sum(ends[:, None, :] // SLOT_TILE <= tile_idx[None, :, None], axis=-1)
    tile_block = jnp.minimum(tile_block, nb - 1).astype(jnp.int32).reshape(bh * tiles_cap)
    base_b = jnp.broadcast_to(base.astype(F32)[:, :, None], (bh, nb, HEAD_DIM))

    n_rows = bh * cap + SLOT_TILE
    slots = _moba_slots(sel, rank, base_b, bh * cap).reshape(bh * 8, seq)
    qs = _sc_scatter_rows(qrows, slots, n_rows)
    qs, companion = lax.optimization_barrier((qs, companion))
    part = _moba_routed(qs, kaug, vtb, tile_block, n_tiles, tiles_cap)
    og = _sc_gather_rows(part, slots)
    return _moba_merge(pa, kaug, vtb, og, sel, slopes, batch, seq), companion


def _split3(x):
    hi = x.astype(BF16)
    r1 = x - hi.astype(F32)
    mid = r1.astype(BF16)
    lo = (r1 - mid.astype(F32)).astype(BF16)
    return hi, mid, lo


def _hgrn_kernel(qb_ref, fb_ref, ib_ref, gb_ref, lbl_ref, gn_ref, o_ref, s_ref, oi_ref, st_ref,
                 *, layer):
    tt = qb_ref.shape[0]
    sub = HGRN_SUB
    hd = HEAD_DIM

    @pl.when(pl.program_id(1) == 0)
    def _():
        s_ref[...] = jnp.zeros_like(s_ref)

    logits = lbl_ref[...]
    e = jnp.exp(logits - jnp.max(logits, axis=0, keepdims=True))
    pl_ = e / jnp.sum(e, axis=0, keepdims=True)
    lb = jnp.sum(pl_[0:layer + 1], axis=0, keepdims=True) - pl_[0:1]

    fb = fb_ref[...].astype(F32)
    t = jnp.exp(-jnp.abs(fb))
    r = 1.0 / (1.0 + t)
    sig_pos = jnp.where(fb >= 0, r, t * r)
    sig_neg = jnp.where(fb >= 0, t * r, r)
    f_gate = lb + (1.0 - lb) * sig_pos
    logf = jnp.log(jnp.maximum(f_gate, F_MIN))
    k_all = (1.0 - lb) * sig_neg
    qb = qb_ref[...].astype(F32)
    q_all = qb * jax.nn.sigmoid(qb)
    v_all = ib_ref[...].astype(F32)

    ri = lax.broadcasted_iota(jnp.int32, (tt, tt), 0)
    ci = lax.broadcasted_iota(jnp.int32, (tt, tt), 1)
    same = (ri // sub) == (ci // sub)
    tri = jnp.where(same & (ci <= ri), 1.0, 0.0).astype(BF16)
    hi, mid, lo = _split3(logf)
    gl = _dot(tri, hi) + _dot(tri, mid) + _dot(tri, lo)

    half = sub // 2
    nsub = tt // sub
    nv = tt // half
    lane_sum = jnp.ones((hd, hd), BF16)
    row_in = lax.broadcasted_iota(jnp.int32, (nv, half, hd), 1)

    def pair_terms(qx, gx, kx, vx, causal):
        n = qx.shape[0]
        out = jnp.zeros_like(qx)
        for rho in range(half):
            kr = pltpu.roll(kx, rho, 1) if rho else kx
            vr = pltpu.roll(vx, rho, 1) if rho else vx
            gr = pltpu.roll(gx[1], rho, 1) if rho else gx[1]
            prod = qx * kr * jnp.exp(gx[0] - gr)
            if causal and rho:
                prod = jnp.where(row_in >= rho, prod, 0.0)
            a = _dot(prod.reshape(n * half, hd).astype(BF16), lane_sum)
            out = out + a.reshape(n, half, hd) * vr
        return out

    for h in range(HGRN_HEADS):
        cs = slice(h * hd, (h + 1) * hd)
        q = q_all[:, cs]
        k = k_all[:, cs]
        v = v_all[:, cs]
        g = gl[:, cs]

        q3, k3, v3, g3 = (a.reshape(nv, half, hd) for a in (q, k, v, g))
        od = pair_terms(q3, (g3, g3), k3, v3, True).reshape(nsub, 2, half, hd)
        q4, k4, v4, g4 = (a.reshape(nsub, 2, half, hd) for a in (q, k, v, g))
        oh = pair_terms(q4[:, 1], (g4[:, 1], g4[:, 0]), k4[:, 0], v4[:, 0], False)
        o_diag = jnp.stack([od[:, 0], od[:, 1] + oh], axis=1).reshape(tt, hd)

        gs = g.reshape(nsub, sub, hd)
        g_end = gs[:, sub - 1:sub, :]
        qd = (q * jnp.exp(g)).astype(BF16).reshape(nsub, sub, hd)
        kd = (k.reshape(nsub, sub, hd) * jnp.exp(g_end - gs)).astype(BF16)
        vb = v.astype(BF16).reshape(nsub, sub, hd)
        dec = jnp.exp(g_end)
        upd = [_dot_t0(vb[c], kd[c]) for c in range(nsub)]
        st = s_ref[h]
        for c in range(nsub):
            st_ref[c] = st.astype(BF16)
            st = st * dec[c] + upd[c]
        s_ref[h] = st
        for c in range(nsub):
            oi_ref[c * sub:(c + 1) * sub, cs] = lax.dot_general(
                qd[c], st_ref[c], (((1,), (1,)), ((), ())), preferred_element_type=F32)

        o = oi_ref[:, cs] + o_diag
        rr = lax.rsqrt(jnp.mean(o * o, axis=-1, keepdims=True) + RMS_EPS)
        gate = jax.nn.sigmoid(gb_ref[:, cs].astype(F32))
        o_ref[:, cs] = (o * rr * gn_ref[...] * gate).astype(o_ref.dtype)


def _hgrn(pa, lb_logits, out_norm, layer, batch, seq):
    tt = min(256, seq)
    nt = seq // tt
    c0 = 3 * MOBA_W // HGRN_W
    nl = lb_logits.shape[0]

    def cmap(off):
        return lambda b, t: (b * nt + t, c0 + off)

    return pl.pallas_call(
        functools.partial(_hgrn_kernel, layer=layer),
        name="hgrn",
        grid=(batch, nt),
        in_specs=[
            pl.BlockSpec((tt, HGRN_W), cmap(0)),
            pl.BlockSpec((tt, HGRN_W), cmap(1)),
            pl.BlockSpec((tt, HGRN_W), cmap(2)),
            pl.BlockSpec((tt, HGRN_W), cmap(3)),
            pl.BlockSpec((nl, HGRN_W), lambda b, t: (0, 0)),
            pl.BlockSpec((1, HEAD_DIM), lambda b, t: (0, 0)),
        ],
        out_specs=pl.BlockSpec((tt, HGRN_W), lambda b, t: (b * nt + t, 0)),
        out_shape=jax.ShapeDtypeStruct((batch * seq, HGRN_W), BF16),
        scratch_shapes=[pltpu.VMEM((HGRN_HEADS, HEAD_DIM, HEAD_DIM), F32),
                        pltpu.VMEM((tt, HGRN_W), F32),
                        pltpu.VMEM((tt // HGRN_SUB, HEAD_DIM, HEAD_DIM), BF16)],
        compiler_params=_cparams(("parallel", "arbitrary")),
    )(pa, pa, pa, pa, lb_logits, out_norm.reshape(1, HEAD_DIM))


def _memkv_kernel(mem_ref, g_ref, w_ref, kt_ref, v_ref):
    h = _rms(mem_ref[0], g_ref[...]).astype(BF16)
    kv = _dot(h, w_ref[...])
    kt_ref[0] = kv[:, 0:MEM_W].T.astype(BF16)
    v_ref[0] = kv[:, MEM_W:].astype(BF16)


def _memkv(mem, g, w):
    b, m, d = mem.shape
    return pl.pallas_call(
        _memkv_kernel,
        name="memkv",
        grid=(b,),
        in_specs=[
            pl.BlockSpec((1, m, d), lambda i: (i, 0, 0)),
            pl.BlockSpec((1, d), lambda i: (0, 0)),
            pl.BlockSpec((d, 2 * MEM_W), lambda i: (0, 0)),
        ],
        out_specs=[
            pl.BlockSpec((1, MEM_W, m), lambda i: (i, 0, 0)),
            pl.BlockSpec((1, m, MEM_W), lambda i: (i, 0, 0)),
        ],
        out_shape=[
            jax.ShapeDtypeStruct((b, MEM_W, m), BF16),
            jax.ShapeDtypeStruct((b, m, MEM_W), BF16),
        ],
        compiler_params=_cparams(("parallel",)),
    )(mem, g.reshape(1, d), w)


def _merge_kernel(x_ref, oa_ref, ob_ref, qm_ref, g0_ref, g1_ref, g2_ref, mkt_ref, mv_ref,
                  wa_ref, wb_ref, wm_ref, wo_ref, o_ref, om_ref):
    scale = HEAD_DIM ** -0.5
    for h in range(MEM_HEADS):
        cs = slice(h * HEAD_DIM, (h + 1) * HEAD_DIM)
        s = _dot(qm_ref[:, cs], mkt_ref[0, cs, :]) * scale
        p = jnp.exp(s - jnp.max(s, axis=-1, keepdims=True))
        l = jnp.sum(p, axis=-1, keepdims=True)
        om_ref[:, cs] = (_dot(p.astype(BF16), mv_ref[0, :, cs]) / l).astype(BF16)

    y = jax.nn.sigmoid(g0_ref[...].astype(F32)) * _dot(oa_ref[...], wa_ref[...])
    y += jax.nn.sigmoid(g1_ref[...].astype(F32)) * _dot(ob_ref[...], wb_ref[...])
    y += jax.nn.sigmoid(g2_ref[...].astype(F32)) * _dot(om_ref[...], wm_ref[...])
    o_ref[...] = x_ref[...] + _dot(y.astype(BF16), wo_ref[...])


def _merge(x, oa, ob, pa, gates, mkt, mv, wa, wb, wm, wo, batch, seq):
    n, d = x.shape
    m = mv.shape[1]
    tm = _pick(seq, 256)
    per_b = seq // tm
    qcol = (3 * MOBA_W + 4 * HGRN_W) // MEM_W
    row = lambda i: (i, 0)
    return pl.pallas_call(
        _merge_kernel,
        name="merge",
        grid=(n // tm,),
        in_specs=[
            pl.BlockSpec((tm, d), row),
            pl.BlockSpec((tm, MOBA_W), row),
            pl.BlockSpec((tm, HGRN_W), row),
            pl.BlockSpec((tm, MEM_W), lambda i: (i, qcol)),
            pl.BlockSpec((tm, d), lambda i: (i, 0)),
            pl.BlockSpec((tm, d), lambda i: (i, 1)),
            pl.BlockSpec((tm, d), lambda i: (i, 2)),
            pl.BlockSpec((1, MEM_W, m), lambda i: (i // per_b, 0, 0)),
            pl.BlockSpec((1, m, MEM_W), lambda i: (i // per_b, 0, 0)),
            _const_spec((MOBA_W, d)),
            _const_spec((HGRN_W, d)),
            _const_spec((MEM_W, d)),
            _const_spec((d, d)),
        ],
        out_specs=pl.BlockSpec((tm, d), row),
        out_shape=jax.ShapeDtypeStruct((n, d), F32),
        scratch_shapes=[pltpu.VMEM((tm, MEM_W), BF16)],
        compiler_params=_cparams(("parallel",)),
    )(x, oa, ob, pa, gates, gates, gates, mkt, mv, wa, wb, wm, wo)


def kernel(x, mem, ffn1_norm, ffn1_w1, ffn1_w3, ffn1_w2, mix_norm, w_in, hgrn_lb_logits,
           hgrn_out_norm, mem_norm, w_mem_kv, w_proj_moba, w_proj_hgrn, w_proj_mem, w_out,
           ffn2_norm, ffn2_w1, ffn2_w3, ffn2_w2, final_norm):
    batch, seq, d = x.shape
    depth = ffn1_w1.shape[0]
    assert seq % (MOBA_QTILE * MOBA_BLOCK) == 0
    assert w_in.shape[-1] == MIX_W + 3 * d
    bf = lambda a: a.astype(BF16)

    hs = jnp.arange(1, MOBA_HEADS + 1, dtype=F32)
    slopes = jnp.tile(jnp.exp2(-8.0 * hs / MOBA_HEADS), batch)
    slopes = jnp.broadcast_to(slopes[:, None, None], (batch * MOBA_HEADS, 1, HEAD_DIM))
    mix_scale = jnp.concatenate([jnp.full((MOBA_W,), HEAD_DIM ** -0.5, F32),
                                 jnp.ones((MIX_W - MOBA_W,), F32)])
    gate_scale = jnp.ones((3 * d,), F32)

    xs = x.reshape(batch * seq, d)
    for l in range(depth):
        last = l == depth - 1
        xs = _ffn(xs, ffn1_norm[l], bf(ffn1_w1[l]), bf(ffn1_w3[l]), bf(ffn1_w2[l]),
                  final_norm, final=False)

        w_l = w_in[l]
        pa = _normproj(xs, mix_norm[l], bf(w_l[:, :MIX_W]), mix_scale)
        gates = _normproj(xs, mix_norm[l], bf(w_l[:, MIX_W:]), gate_scale)

        kaug, vtb, kmean = _moba_prep(pa, slopes, batch, seq)
        ob = _hgrn(pa, hgrn_lb_logits, hgrn_out_norm[l], l, batch, seq)
        oa, ob = _moba_routed_attention(pa, kaug, vtb, kmean, slopes, batch, seq, ob)
        mkt, mv = _memkv(mem, mem_norm[l], bf(w_mem_kv[l]))
        xs = _merge(xs, oa, ob, pa, gates, mkt, mv, bf(w_proj_moba[l]), bf(w_proj_hgrn[l]),
                    bf(w_proj_mem[l]), bf(w_out[l]), batch, seq)

        xs = _ffn(xs, ffn2_norm[l], bf(ffn2_w1[l]), bf(ffn2_w3[l]), bf(ffn2_w2[l]),
                  final_norm, final=last)
    return xs.reshape(batch, seq, d)
```

```python
import functools

import jax
import jax.numpy as jnp
from jax import lax
from jax.experimental import pallas as pl
from jax.experimental.pallas import tpu as pltpu
from jax.experimental.pallas import tpu_sc as plsc

F32 = jnp.float32
BF16 = jnp.bfloat16

HEAD_DIM = 128
MOBA_HEADS = 8
MOBA_BLOCK = 256
MOBA_TOPK = 3
HGRN_HEADS = 4
MEM_HEADS = 4
RMS_EPS = 1e-6
NEG_INF = -1e30
F_MIN = 1e-20

MOBA_W = MOBA_HEADS * HEAD_DIM
HGRN_W = HGRN_HEADS * HEAD_DIM
MEM_W = MEM_HEADS * HEAD_DIM
MIX_W = 3 * MOBA_W + 4 * HGRN_W + MEM_W

VMEM_LIMIT_BYTES = 60 * 1024 * 1024

AUG_W = 2 * HEAD_DIM
AUG_BIAS = HEAD_DIM
VT_ROWS = HEAD_DIM + 16
MOBA_QTILE = 4

HGRN_SUB = 16
FFN_OUT_CHUNK = 512
SLOT_TILE = 512
ROUTED_TILES_PER_STEP = 8
SC_WINDOW = 128


def _cparams(sem):
    return pltpu.CompilerParams(dimension_semantics=sem, vmem_limit_bytes=VMEM_LIMIT_BYTES)


def _dot(a, b):
    return jnp.dot(a, b, preferred_element_type=F32)


def _dot_t0(a, b):
    return lax.dot_general(a, b, (((0,), (0,)), ((), ())), preferred_element_type=F32)


def _rms(xf, g):
    r = lax.rsqrt(jnp.mean(xf * xf, axis=-1, keepdims=True) + RMS_EPS)
    return xf * r * g


def _pick(n, want):
    if n <= want:
        return n
    t = (want // 128) * 128
    while t >= 128:
        if n % t == 0:
            return t
        t -= 128
    return n


def _const_spec(shape):
    nd = len(shape)
    return pl.BlockSpec(shape, lambda *_: (0,) * nd)


def _ffn_kernel(x_ref, g_ref, w1_ref, w3_ref, w2_ref, fg_ref, o_ref, h_ref, *, final):
    j = pl.program_id(1)

    @pl.when(j == 0)
    def _():
        xf = x_ref[...]
        h_ref[...] = _rms(xf, g_ref[...]).astype(BF16)
        o_ref[...] = xf

    h = h_ref[...]
    u = _dot(h, w1_ref[...])
    v = _dot(h, w3_ref[...])
    a = (0.5 * u * jax.nn.sigmoid(u) * v).astype(BF16)
    tn = FFN_OUT_CHUNK if o_ref.shape[1] % FFN_OUT_CHUNK == 0 else o_ref.shape[1]
    for c in range(o_ref.shape[1] // tn):
        cs = slice(c * tn, (c + 1) * tn)
        o_ref[:, cs] += _dot(a, w2_ref[:, cs])

    if final:
        @pl.when(j == pl.num_programs(1) - 1)
        def _():
            o_ref[...] = _rms(o_ref[...], fg_ref[...])


def _ffn(x, g, w1, w3, w2, final_g, *, final):
    n, d = x.shape
    dff = w1.shape[1]
    tm = _pick(n, 1024)
    tf = _pick(dff, 512)
    return pl.pallas_call(
        functools.partial(_ffn_kernel, final=final),
        name="ffn_final" if final else "ffn",
        grid=(n // tm, dff // tf),
        in_specs=[
            pl.BlockSpec((tm, d), lambda i, j: (i, 0), pipeline_mode=pl.Buffered(1)),
            pl.BlockSpec((1, d), lambda i, j: (0, 0)),
            pl.BlockSpec((d, tf), lambda i, j: (0, j)),
            pl.BlockSpec((d, tf), lambda i, j: (0, j)),
            pl.BlockSpec((tf, d), lambda i, j: (j, 0)),
            pl.BlockSpec((1, d), lambda i, j: (0, 0)),
        ],
        out_specs=pl.BlockSpec((tm, d), lambda i, j: (i, 0)),
        out_shape=jax.ShapeDtypeStruct((n, d), F32),
        scratch_shapes=[pltpu.VMEM((tm, d), BF16)],
        compiler_params=_cparams(("parallel", "arbitrary")),
    )(x, g.reshape(1, d), w1, w3, w2, final_g.reshape(1, d))


def _normproj_kernel(x_ref, g_ref, w_ref, s_ref, o_ref, h_ref, *, tn):
    h_ref[...] = _rms(x_ref[...], g_ref[...]).astype(BF16)
    for c in range(w_ref.shape[1] // tn):
        cs = slice(c * tn, (c + 1) * tn)
        o_ref[:, cs] = (_dot(h_ref[...], w_ref[:, cs]) * s_ref[:, cs]).astype(o_ref.dtype)


def _normproj(x, g, w, col_scale):
    n, d = x.shape
    nout = w.shape[1]
    tm = _pick(n, 512)
    tn = _pick(nout, 512)
    return pl.pallas_call(
        functools.partial(_normproj_kernel, tn=tn),
        name="normproj",
        grid=(n // tm,),
        in_specs=[
            pl.BlockSpec((tm, d), lambda i: (i, 0)),
            pl.BlockSpec((1, d), lambda i: (0, 0)),
            pl.BlockSpec((d, nout), lambda i: (0, 0), pipeline_mode=pl.Buffered(1)),
            pl.BlockSpec((1, nout), lambda i: (0, 0)),
        ],
        out_specs=pl.BlockSpec((tm, nout), lambda i: (i, 0)),
        out_shape=jax.ShapeDtypeStruct((n, nout), BF16),
        scratch_shapes=[pltpu.VMEM((tm, d), BF16)],
        compiler_params=_cparams(("parallel",)),
    )(x, g.reshape(1, d), w, col_scale.reshape(1, nout))


def _moba_prep_kernel(k_ref, v_ref, slope_ref, kaug_ref, vtb_ref, kmean_ref):
    g = pl.program_id(1)
    bs = MOBA_BLOCK
    nblk = k_ref.shape[0] // bs
    slope = slope_ref[0, 0:1, 0:1]
    lane = lax.broadcasted_iota(jnp.int32, (bs, AUG_W - HEAD_DIM), 1)
    row = lax.broadcasted_iota(jnp.int32, (bs, AUG_W - HEAD_DIM), 0).astype(F32)
    extra = jnp.where(lane == AUG_BIAS - HEAD_DIM, slope * row, 0.0).astype(BF16)
    orow = lax.broadcasted_iota(jnp.int32, (VT_ROWS - HEAD_DIM, bs), 0)
    ones_row = jnp.where(orow == 0, 1.0, 0.0).astype(BF16)
    for u in range(nblk):
        k = k_ref[u * bs:(u + 1) * bs, :]
        kmean_ref[0, pl.ds(g * nblk + u, 1), :] = jnp.mean(k.astype(F32), axis=0, keepdims=True)
        kaug_ref[0, u, :, 0:HEAD_DIM] = k
        kaug_ref[0, u, :, HEAD_DIM:AUG_W] = extra
        vt = v_ref[u * bs:(u + 1) * bs, :].astype(F32).T
        vtb_ref[0, u, 0:HEAD_DIM, :] = vt.astype(BF16)
        vtb_ref[0, u, HEAD_DIM:VT_ROWS, :] = ones_row


def _moba_prep(pa, slopes, batch, seq):
    nb = seq // MOBA_BLOCK
    grp = MOBA_QTILE
    ng = nb // grp
    bh = batch * MOBA_HEADS
    kcol = MOBA_W // HEAD_DIM
    vcol = 2 * MOBA_W // HEAD_DIM

    def kmap(n, g):
        return ((n // MOBA_HEADS) * ng + g, kcol + n % MOBA_HEADS)

    def vmap(n, g):
        return ((n // MOBA_HEADS) * ng + g, vcol + n % MOBA_HEADS)

    return pl.pallas_call(
        _moba_prep_kernel,
        name="moba_prep",
        grid=(bh, ng),
        in_specs=[
            pl.BlockSpec((grp * MOBA_BLOCK, HEAD_DIM), kmap),
            pl.BlockSpec((grp * MOBA_BLOCK, HEAD_DIM), vmap),
            pl.BlockSpec((1, 1, HEAD_DIM), lambda n, g: (n, 0, 0)),
        ],
        out_specs=[
            pl.BlockSpec((1, grp, MOBA_BLOCK, AUG_W), lambda n, g: (n, g, 0, 0)),
            pl.BlockSpec((1, grp, VT_ROWS, MOBA_BLOCK), lambda n, g: (n, g, 0, 0)),
            pl.BlockSpec((1, nb, HEAD_DIM), lambda n, g: (n, 0, 0)),
        ],
        out_shape=[
            jax.ShapeDtypeStruct((bh, nb, MOBA_BLOCK, AUG_W), BF16),
            jax.ShapeDtypeStruct((bh, nb, VT_ROWS, MOBA_BLOCK), BF16),
            jax.ShapeDtypeStruct((bh, nb, HEAD_DIM), F32),
        ],
        compiler_params=_cparams(("parallel", "arbitrary")),
    )(pa, pa, slopes)


def _moba_gate_kernel(q_ref, kmean_ref, qrow_ref, sel_ref, rank_ref, cnt_ref):
    t = pl.program_id(1)
    bs = MOBA_BLOCK
    tq = q_ref.shape[0]
    nb = kmean_ref.shape[1]

    q = q_ref[...]
    qrow_ref[...] = q.astype(F32)
    qt = q.astype(F32).T.astype(BF16)

    lane = lax.broadcasted_iota(jnp.int32, (1, tq), 1)
    own = t * (tq // bs) + lane // bs
    km = kmean_ref[0]
    km_hi = km.astype(BF16)
    km_lo = (km - km_hi.astype(F32)).astype(BF16)
    gate = _dot(km_hi, qt) + _dot(km_lo, qt)
    blk = lax.broadcasted_iota(jnp.int32, (nb, tq), 0)
    gate = jnp.where(blk < own, gate, NEG_INF)

    @pl.when(t == 0)
    def _():
        cnt_ref[...] = jnp.zeros_like(cnt_ref)

    run = cnt_ref[0][:, 0:1]
    qi = lax.broadcasted_iota(jnp.int32, (bs, bs), 0)
    qj = lax.broadcasted_iota(jnp.int32, (bs, bs), 1)
    before = jnp.where(qi < qj, 1.0, 0.0).astype(BF16)
    sels, ranks = [], []
    for r in range(MOBA_TOPK):
        mx = jnp.max(gate, axis=0, keepdims=True)
        first = jnp.min(jnp.where(gate == mx, blk, nb), axis=0, keepdims=True)
        hit = blk == first
        gate = jnp.where(hit, -jnp.inf, gate)
        valid = own > r
        oh = jnp.where(hit & valid, 1.0, 0.0)
        parts = []
        for c in range(tq // bs):
            ohc = oh[:, c * bs:(c + 1) * bs]
            prior = _dot(ohc.astype(BF16), before)
            parts.append(jnp.sum(ohc * (run + prior), axis=0, keepdims=True))
            run = run + jnp.sum(ohc, axis=1, keepdims=True)
        ranks.append(jnp.concatenate(parts, axis=1))
        sels.append(jnp.where(valid, first, -1))
    cnt_ref[0] = jnp.broadcast_to(run, cnt_ref.shape[1:])
    pad = jnp.zeros((8 - MOBA_TOPK, tq), jnp.int32)
    sel_ref[0] = jnp.concatenate(sels + [pad], axis=0)
    rank_ref[0] = jnp.concatenate([x.astype(jnp.int32) for x in ranks] + [pad], axis=0)


def _moba_gate(pa, kmean, batch, seq):
    nb = seq // MOBA_BLOCK
    tq = min(MOBA_QTILE * MOBA_BLOCK, seq)
    nt = seq // tq
    bh = batch * MOBA_HEADS

    def qmap(n, t):
        return ((n // MOBA_HEADS) * nt + t, n % MOBA_HEADS)

    return pl.pallas_call(
        _moba_gate_kernel,
        name="moba_gate",
        grid=(bh, nt),
        in_specs=[
            pl.BlockSpec((tq, HEAD_DIM), qmap),
            pl.BlockSpec((1, nb, HEAD_DIM), lambda n, t: (n, 0, 0)),
        ],
        out_specs=[
            pl.BlockSpec((tq, HEAD_DIM), lambda n, t: (n * nt + t, 0)),
            pl.BlockSpec((1, 8, tq), lambda n, t: (n, 0, t)),
            pl.BlockSpec((1, 8, tq), lambda n, t: (n, 0, t)),
            pl.BlockSpec((1, nb, HEAD_DIM), lambda n, t: (n, 0, 0)),
        ],
        out_shape=[
            jax.ShapeDtypeStruct((bh * seq, HEAD_DIM), F32),
            jax.ShapeDtypeStruct((bh, 8, seq), jnp.int32),
            jax.ShapeDtypeStruct((bh, 8, seq), jnp.int32),
            jax.ShapeDtypeStruct((bh, nb, HEAD_DIM), F32),
        ],
        compiler_params=_cparams(("parallel", "arbitrary")),
    )(pa, kmean)


def _moba_slot_kernel(sel_ref, rank_ref, base_ref, slot_ref, *, trash):
    nb = base_ref.shape[1]
    tq = sel_ref.shape[2]
    base = base_ref[0][:, 0:1]
    blk = lax.broadcasted_iota(jnp.int32, (nb, tq), 0)
    spare = trash + lax.broadcasted_iota(jnp.int32, (1, tq), 1) % SLOT_TILE
    rows = []
    for r in range(MOBA_TOPK):
        sel = sel_ref[0, r:r + 1, :]
        start = jnp.sum(jnp.where(blk == sel, base, 0.0), axis=0, keepdims=True)
        slot = start.astype(jnp.int32) + rank_ref[0, r:r + 1, :]
        rows.append(jnp.where(sel >= 0, slot, spare))
    rows.append(jnp.broadcast_to(spare, (8 - MOBA_TOPK, tq)))
    slot_ref[0] = jnp.concatenate(rows, axis=0)


def _moba_slots(sel, rank, base, trash):
    bh, _, seq = sel.shape
    nb = base.shape[1]
    tq = min(2048, seq)
    spec = pl.BlockSpec((1, 8, tq), lambda n, t: (n, 0, t))
    return pl.pallas_call(
        functools.partial(_moba_slot_kernel, trash=trash),
        name="moba_slots",
        grid=(bh, seq // tq),
        in_specs=[spec, spec, pl.BlockSpec((1, nb, HEAD_DIM), lambda n, t: (n, 0, 0))],
        out_specs=spec,
        out_shape=jax.ShapeDtypeStruct((bh, 8, seq), jnp.int32),
        compiler_params=_cparams(("parallel", "parallel")),
    )(sel, rank, base)


def _sc_mesh():
    return plsc.VectorSubcoreMesh(core_axis_name="core", subcore_axis_name="subcore")


def _sc_scatter_rows(rows, slots, n_out):
    bh8, seq = slots.shape
    bh = bh8 // 8
    nw = seq // SC_WINDOW

    @pl.kernel(out_type=jax.ShapeDtypeStruct((n_out, HEAD_DIM), rows.dtype), mesh=_sc_mesh(),
               scratch_types=[])
    def scatter(x_hbm, i_hbm, o_hbm):
        def body(x_vmem, i_vmem):
            pltpu.sync_copy(x_vmem, o_hbm.at[i_vmem.at[0]])

        pltpu.emit_pipeline(
            body,
            grid=(bh * MOBA_TOPK * nw,),
            in_specs=[
                pl.BlockSpec((SC_WINDOW, HEAD_DIM),
                             index_map=lambda i: ((i // (MOBA_TOPK * nw)) * nw + i % nw, 0)),
                pl.BlockSpec((1, SC_WINDOW),
                             index_map=lambda i: ((i // (MOBA_TOPK * nw)) * 8 + (i // nw) % MOBA_TOPK,
                                                  i % nw)),
            ],
            out_specs=[],
            core_axis_name=("core", "subcore"),
            dimension_semantics=(pltpu.PARALLEL,),
        )(x_hbm, i_hbm)

    return scatter(rows, slots)


def _sc_gather_rows(table, slots):
    bh8, seq = slots.shape
    bh = bh8 // 8
    nw = seq // SC_WINDOW
    n_out = bh * MOBA_TOPK * seq

    @pl.kernel(out_type=jax.ShapeDtypeStruct((n_out, HEAD_DIM), table.dtype), mesh=_sc_mesh())
    def gather(x_hbm, i_hbm, o_hbm):
        def body(i_vmem, o_vmem):
            pltpu.sync_copy(x_hbm.at[i_vmem.at[0]], o_vmem)

        pltpu.emit_pipeline(
            body,
            grid=(bh * MOBA_TOPK * nw,),
            in_specs=[
                pl.BlockSpec((1, SC_WINDOW),
                             index_map=lambda i: ((i // (MOBA_TOPK * nw)) * 8 + (i // nw) % MOBA_TOPK,
                                                  i % nw)),
            ],
            out_specs=[pl.BlockSpec((SC_WINDOW, HEAD_DIM), index_map=lambda i: (i, 0))],
            core_axis_name=("core", "subcore"),
            dimension_semantics=(pltpu.PARALLEL,),
        )(i_hbm, o_hbm)

    return gather(table, slots)


def _moba_routed_kernel(tb_ref, nt_ref, qs_ref, kaug_ref, vtb_ref, o_ref, sa_ref, sb_ref, sc_ref):
    n = pl.program_id(0)
    g = pl.program_id(1)
    tpg = qs_ref.shape[0] // SLOT_TILE
    tiles_cap = pl.num_programs(1) * tpg

    @pl.when(g * tpg < nt_ref[n])
    def _():
        row = lax.broadcasted_iota(jnp.int32, (AUG_W - HEAD_DIM, SLOT_TILE), 0)
        tail = jnp.where(row == AUG_BIAS - HEAD_DIM, 1.0, 0.0).astype(BF16)
        frow = lax.broadcasted_iota(jnp.int32, (HEAD_DIM, SLOT_TILE), 0)
        blocks = [tb_ref[n * tiles_cap + g * tpg + u] for u in range(tpg)]

        def scores(u):
            qt = qs_ref[u * SLOT_TILE:(u + 1) * SLOT_TILE, :].T.astype(BF16)
            return _dot(kaug_ref[0, blocks[u]], jnp.concatenate([qt, tail], axis=0))

        bufs = (sa_ref, sb_ref, sc_ref)
        sa_ref[...] = scores(0)
        sb_ref[...] = scores(1)
        for u in range(tpg):
            if u + 2 < tpg:
                bufs[(u + 2) % 3][...] = scores(u + 2)
            s = bufs[u % 3][...]
            m = jnp.max(s, axis=0, keepdims=True)
            p = jnp.exp(s - m).astype(BF16)
            acc = _dot(vtb_ref[0, blocks[u]], p)
            l = acc[HEAD_DIM:HEAD_DIM + 1]
            o = (acc[0:HEAD_DIM] / l).astype(BF16).astype(F32)
            lse = m + jnp.log(l)
            ob = lax.bitcast_convert_type(o, jnp.uint32)
            lb = lax.bitcast_convert_type(lse, jnp.uint32)
            extra = jnp.where(frow == 0, lb >> 16, jnp.where(frow == 1, lb & 0xFFFF, 0))
            packed = lax.bitcast_convert_type(ob | extra, F32)
            o_ref[u * SLOT_TILE:(u + 1) * SLOT_TILE, :] = packed.T


def _moba_routed(qs, kaug, vtb, tile_block, n_tiles, tiles_cap):
    bh, nb = kaug.shape[0], kaug.shape[1]
    tpg = ROUTED_TILES_PER_STEP
    steps = tiles_cap // tpg
    rows = tpg * SLOT_TILE

    def qmap(n, g, tb, nt):
        used = jnp.maximum((nt[n] + tpg - 1) // tpg, 1)
        return (n * steps + jnp.minimum(g, used - 1), 0)

    grid_spec = pltpu.PrefetchScalarGridSpec(
        num_scalar_prefetch=2,
        grid=(bh, steps),
        in_specs=[
            pl.BlockSpec((rows, HEAD_DIM), qmap),
            pl.BlockSpec((1, nb, MOBA_BLOCK, AUG_W), lambda n, g, tb, nt: (n, 0, 0, 0)),
            pl.BlockSpec((1, nb, VT_ROWS, MOBA_BLOCK), lambda n, g, tb, nt: (n, 0, 0, 0)),
        ],
        out_specs=pl.BlockSpec((rows, HEAD_DIM), qmap),
        scratch_shapes=[pltpu.VMEM((MOBA_BLOCK, SLOT_TILE), F32)] * 3,
    )
    return pl.pallas_call(
        _moba_routed_kernel,
        name="moba_routed",
        grid_spec=grid_spec,
        out_shape=jax.ShapeDtypeStruct((qs.shape[0], HEAD_DIM), F32),
        compiler_params=_cparams(("parallel", "arbitrary")),
    )(tile_block, n_tiles, qs, kaug, vtb)


def _moba_merge_kernel(q_ref, kaug_ref, vtb_ref, og_ref, sel_ref, slope_ref, o_ref):
    t = pl.program_id(1)
    bs = MOBA_BLOCK
    tq = q_ref.shape[0]
    slope = slope_ref[0, 0:1, 0:1]

    qt = q_ref[...].astype(F32).T.astype(BF16)
    lane = lax.broadcasted_iota(jnp.int32, (1, tq), 1)
    tpos = (t * tq + lane).astype(F32)
    lk = lax.broadcasted_iota(jnp.int32, (bs, bs), 0)
    lq = lax.broadcasted_iota(jnp.int32, (bs, bs), 1)
    dist = (lq - lk).astype(F32)
    accs, ms = [], []
    for u in range(tq // bs):
        s = _dot(kaug_ref[0, u, :, 0:HEAD_DIM], qt[:, u * bs:(u + 1) * bs])
        s = jnp.where(dist >= 0.0, s - slope * dist, NEG_INF)
        m_u = jnp.max(s, axis=0, keepdims=True)
        p = jnp.exp(s - m_u).astype(BF16)
        accs.append(_dot(vtb_ref[0, u], p))
        ms.append(m_u)
    acc = jnp.concatenate(accs, axis=1)
    l = acc[HEAD_DIM:HEAD_DIM + 1]
    parts = [acc[0:HEAD_DIM] / l]
    lses = [jnp.concatenate(ms, axis=1) + jnp.log(l)]

    for r in range(MOBA_TOPK):
        u = lax.bitcast_convert_type(og_ref[0, r, 0].T, jnp.uint32)
        lb = ((u[0:1] & 0xFFFF) << 16) | (u[1:2] & 0xFFFF)
        lse_r = lax.bitcast_convert_type(lb, F32)
        sel_r = sel_ref[0, r:r + 1, :]
        valid = sel_r >= 0
        lses.append(jnp.where(valid, lse_r + slope * ((sel_r * bs).astype(F32) - tpos), NEG_INF))
        hi_half = lax.bitcast_convert_type((u >> 16) << 16, F32)
        parts.append(jnp.where(valid, hi_half, 0.0))
    mx = functools.reduce(jnp.maximum, lses)
    ws = [jnp.exp(x - mx) for x in lses]
    num = functools.reduce(lambda a, b: a + b, [w * o for w, o in zip(ws, parts)])
    out_t = num / functools.reduce(lambda a, b: a + b, ws)
    o_ref[...] = out_t.T.astype(o_ref.dtype)


def _moba_merge(pa, kaug, vtb, og, sel, slopes, batch, seq):
    tq = MOBA_QTILE * MOBA_BLOCK
    nt = seq // tq
    bh = batch * MOBA_HEADS
    og = og.reshape(bh, MOBA_TOPK, nt, tq, HEAD_DIM)

    def qmap(n, t):
        return ((n // MOBA_HEADS) * nt + t, n % MOBA_HEADS)

    return pl.pallas_call(
        _moba_merge_kernel,
        name="moba_merge",
        grid=(bh, nt),
        in_specs=[
            pl.BlockSpec((tq, HEAD_DIM), qmap),
            pl.BlockSpec((1, MOBA_QTILE, MOBA_BLOCK, AUG_W), lambda n, t: (n, t, 0, 0)),
            pl.BlockSpec((1, MOBA_QTILE, VT_ROWS, MOBA_BLOCK), lambda n, t: (n, t, 0, 0)),
            pl.BlockSpec((1, MOBA_TOPK, 1, tq, HEAD_DIM), lambda n, t: (n, 0, t, 0, 0)),
            pl.BlockSpec((1, 8, tq), lambda n, t: (n, 0, t)),
            pl.BlockSpec((1, 1, HEAD_DIM), lambda n, t: (n, 0, 0)),
        ],
        out_specs=pl.BlockSpec((tq, HEAD_DIM), qmap),
        out_shape=jax.ShapeDtypeStruct((batch * seq, MOBA_W), BF16),
        compiler_params=_cparams(("parallel", "parallel")),
    )(pa, kaug, vtb, og, sel, slopes)


def _moba_routed_attention(pa, kaug, vtb, kmean, slopes, batch, seq, companion):
    nb = seq // MOBA_BLOCK
    bh = batch * MOBA_HEADS
    qrows, sel, rank, cnt = _moba_gate(pa, kmean, batch, seq)

    step_slots = SLOT_TILE * ROUTED_TILES_PER_STEP
    cap = -(-(MOBA_TOPK * seq + nb * SLOT_TILE) // step_slots) * step_slots
    tiles_cap = cap // SLOT_TILE
    counts = cnt[:, :, 0].astype(jnp.int32)
    padded = (counts + SLOT_TILE - 1) // SLOT_TILE * SLOT_TILE
    ends = jnp.cumsum(padded, axis=1)
    base = ends - padded + (jnp.arange(bh, dtype=jnp.int32) * cap)[:, None]
    n_tiles = ends[:, -1] // SLOT_TILE
    tile_idx = jnp.arange(tiles_cap, dtype=jnp.int32)
    tile_block = jnp.sum(ends[:, None, :] // SLOT_TILE <= tile_idx[None, :, None], axis=-1)
    tile_block = jnp.minimum(tile_block, nb - 1).astype(jnp.int32).reshape(bh * tiles_cap)
    base_b = jnp.broadcast_to(base.astype(F32)[:, :, None], (bh, nb, HEAD_DIM))

    n_rows = bh * cap + SLOT_TILE
    slots = _moba_slots(sel, rank, base_b, bh * cap).reshape(bh * 8, seq)
    qs = _sc_scatter_rows(qrows, slots, n_rows)
    qs, companion = lax.optimization_barrier((qs, companion))
    part = _moba_routed(qs, kaug, vtb, tile_block, n_tiles, tiles_cap)
    og = _sc_gather_rows(part, slots)
    return _moba_merge(pa, kaug, vtb, og, sel, slopes, batch, seq), companion


def _split3(x):
    hi = x.astype(BF16)
    r1 = x - hi.astype(F32)
    mid = r1.astype(BF16)
    lo = (r1 - mid.astype(F32)).astype(BF16)
    return hi, mid, lo


def _hgrn_kernel(qb_ref, fb_ref, ib_ref, gb_ref, lbl_ref, gn_ref, o_ref, s_ref, oi_ref, st_ref,
                 *, layer):
    tt = qb_ref.shape[0]
    sub = HGRN_SUB
    hd = HEAD_DIM

    @pl.when(pl.program_id(1) == 0)
    def _():
        s_ref[...] = jnp.zeros_like(s_ref)

    logits = lbl_ref[...]
    e = jnp.exp(logits - jnp.max(logits, axis=0, keepdims=True))
    pl_ = e / jnp.sum(e, axis=0, keepdims=True)
    lb = jnp.sum(pl_[0:layer + 1], axis=0, keepdims=True) - pl_[0:1]

    fb = fb_ref[...].astype(F32)
    t = jnp.exp(-jnp.abs(fb))
    r = 1.0 / (1.0 + t)
    sig_pos = jnp.where(fb >= 0, r, t * r)
    sig_neg = jnp.where(fb >= 0, t * r, r)
    f_gate = lb + (1.0 - lb) * sig_pos
    logf = jnp.log(jnp.maximum(f_gate, F_MIN))
    k_all = (1.0 - lb) * sig_neg
    qb = qb_ref[...].astype(F32)
    q_all = qb * jax.nn.sigmoid(qb)
    v_all = ib_ref[...].astype(F32)

    ri = lax.broadcasted_iota(jnp.int32, (tt, tt), 0)
    ci = lax.broadcasted_iota(jnp.int32, (tt, tt), 1)
    same = (ri // sub) == (ci // sub)
    tri = jnp.where(same & (ci <= ri), 1.0, 0.0).astype(BF16)
    hi, mid, lo = _split3(logf)
    gl = _dot(tri, hi) + _dot(tri, mid) + _dot(tri, lo)

    half = sub // 2
    nsub = tt // sub
    nv = tt // half
    lane_sum = jnp.ones((hd, hd), BF16)
    row_in = lax.broadcasted_iota(jnp.int32, (nv, half, hd), 1)

    def pair_terms(qx, gx, kx, vx, causal):
        n = qx.shape[0]
        out = jnp.zeros_like(qx)
        for rho in range(half):
            kr = pltpu.roll(kx, rho, 1) if rho else kx
            vr = pltpu.roll(vx, rho, 1) if rho else vx
            gr = pltpu.roll(gx[1], rho, 1) if rho else gx[1]
            prod = qx * kr * jnp.exp(gx[0] - gr)
            if causal and rho:
                prod = jnp.where(row_in >= rho, prod, 0.0)
            a = _dot(prod.reshape(n * half, hd).astype(BF16), lane_sum)
            out = out + a.reshape(n, half, hd) * vr
        return out

    for h in range(HGRN_HEADS):
        cs = slice(h * hd, (h + 1) * hd)
        q = q_all[:, cs]
        k = k_all[:, cs]
        v = v_all[:, cs]
        g = gl[:, cs]

        q3, k3, v3, g3 = (a.reshape(nv, half, hd) for a in (q, k, v, g))
        od = pair_terms(q3, (g3, g3), k3, v3, True).reshape(nsub, 2, half, hd)
        q4, k4, v4, g4 = (a.reshape(nsub, 2, half, hd) for a in (q, k, v, g))
        oh = pair_terms(q4[:, 1], (g4[:, 1], g4[:, 0]), k4[:, 0], v4[:, 0], False)
        o_diag = jnp.stack([od[:, 0], od[:, 1] + oh], axis=1).reshape(tt, hd)

        gs = g.reshape(nsub, sub, hd)
        g_end = gs[:, sub - 1:sub, :]
        qd = (q * jnp.exp(g)).astype(BF16).reshape(nsub, sub, hd)
        kd = (k.reshape(nsub, sub, hd) * jnp.exp(g_end - gs)).astype(BF16)
        vb = v.astype(BF16).reshape(nsub, sub, hd)
        dec = jnp.exp(g_end)
        upd = [_dot_t0(vb[c], kd[c]) for c in range(nsub)]
        st = s_ref[h]
        for c in range(nsub):
            st_ref[c] = st.astype(BF16)
            st = st * dec[c] + upd[c]
        s_ref[h] = st
        for c in range(nsub):
            oi_ref[c * sub:(c + 1) * sub, cs] = lax.dot_general(
                qd[c], st_ref[c], (((1,), (1,)), ((), ())), preferred_element_type=F32)

        o = oi_ref[:, cs] + o_diag
        rr = lax.rsqrt(jnp.mean(o * o, axis=-1, keepdims=True) + RMS_EPS)
        gate = jax.nn.sigmoid(gb_ref[:, cs].astype(F32))
        o_ref[:, cs] = (o * rr * gn_ref[...] * gate).astype(o_ref.dtype)


def _hgrn(pa, lb_logits, out_norm, layer, batch, seq):
    tt = min(256, seq)
    nt = seq // tt
    c0 = 3 * MOBA_W // HGRN_W
    nl = lb_logits.shape[0]

    def cmap(off):
        return lambda b, t: (b * nt + t, c0 + off)

    return pl.pallas_call(
        functools.partial(_hgrn_kernel, layer=layer),
        name="hgrn",
        grid=(batch, nt),
        in_specs=[
            pl.BlockSpec((tt, HGRN_W), cmap(0)),
            pl.BlockSpec((tt, HGRN_W), cmap(1)),
            pl.BlockSpec((tt, HGRN_W), cmap(2)),
            pl.BlockSpec((tt, HGRN_W), cmap(3)),
            pl.BlockSpec((nl, HGRN_W), lambda b, t: (0, 0)),
            pl.BlockSpec((1, HEAD_DIM), lambda b, t: (0, 0)),
        ],
        out_specs=pl.BlockSpec((tt, HGRN_W), lambda b, t: (b * nt + t, 0)),
        out_shape=jax.ShapeDtypeStruct((batch * seq, HGRN_W), BF16),
        scratch_shapes=[pltpu.VMEM((HGRN_HEADS, HEAD_DIM, HEAD_DIM), F32),
                        pltpu.VMEM((tt, HGRN_W), F32),
                        pltpu.VMEM((tt // HGRN_SUB, HEAD_DIM, HEAD_DIM), BF16)],
        compiler_params=_cparams(("parallel", "arbitrary")),
    )(pa, pa, pa, pa, lb_logits, out_norm.reshape(1, HEAD_DIM))


def _memkv_kernel(mem_ref, g_ref, w_ref, kt_ref, v_ref):
    h = _rms(mem_ref[0], g_ref[...]).astype(BF16)
    kv = _dot(h, w_ref[...])
    kt_ref[0] = kv[:, 0:MEM_W].T.astype(BF16)
    v_ref[0] = kv[:, MEM_W:].astype(BF16)


def _memkv(mem, g, w):
    b, m, d = mem.shape
    return pl.pallas_call(
        _memkv_kernel,
        name="memkv",
        grid=(b,),
        in_specs=[
            pl.BlockSpec((1, m, d), lambda i: (i, 0, 0)),
            pl.BlockSpec((1, d), lambda i: (0, 0)),
            pl.BlockSpec((d, 2 * MEM_W), lambda i: (0, 0)),
        ],
        out_specs=[
            pl.BlockSpec((1, MEM_W, m), lambda i: (i, 0, 0)),
            pl.BlockSpec((1, m, MEM_W), lambda i: (i, 0, 0)),
        ],
        out_shape=[
            jax.ShapeDtypeStruct((b, MEM_W, m), BF16),
            jax.ShapeDtypeStruct((b, m, MEM_W), BF16),
        ],
        compiler_params=_cparams(("parallel",)),
    )(mem, g.reshape(1, d), w)


def _merge_kernel(x_ref, oa_ref, ob_ref, qm_ref, g0_ref, g1_ref, g2_ref, mkt_ref, mv_ref,
                  wa_ref, wb_ref, wm_ref, wo_ref, o_ref, om_ref):
    scale = HEAD_DIM ** -0.5
    for h in range(MEM_HEADS):
        cs = slice(h * HEAD_DIM, (h + 1) * HEAD_DIM)
        s = _dot(qm_ref[:, cs], mkt_ref[0, cs, :]) * scale
        p = jnp.exp(s - jnp.max(s, axis=-1, keepdims=True))
        l = jnp.sum(p, axis=-1, keepdims=True)
        om_ref[:, cs] = (_dot(p.astype(BF16), mv_ref[0, :, cs]) / l).astype(BF16)

    y = jax.nn.sigmoid(g0_ref[...].astype(F32)) * _dot(oa_ref[...], wa_ref[...])
    y += jax.nn.sigmoid(g1_ref[...].astype(F32)) * _dot(ob_ref[...], wb_ref[...])
    y += jax.nn.sigmoid(g2_ref[...].astype(F32)) * _dot(om_ref[...], wm_ref[...])
    o_ref[...] = x_ref[...] + _dot(y.astype(BF16), wo_ref[...])


def _merge(x, oa, ob, pa, gates, mkt, mv, wa, wb, wm, wo, batch, seq):
    n, d = x.shape
    m = mv.shape[1]
    tm = _pick(seq, 256)
    per_b = seq // tm
    qcol = (3 * MOBA_W + 4 * HGRN_W) // MEM_W
    row = lambda i: (i, 0)
    return pl.pallas_call(
        _merge_kernel,
        name="merge",
        grid=(n // tm,),
        in_specs=[
            pl.BlockSpec((tm, d), row),
            pl.BlockSpec((tm, MOBA_W), row),
            pl.BlockSpec((tm, HGRN_W), row),
            pl.BlockSpec((tm, MEM_W), lambda i: (i, qcol)),
            pl.BlockSpec((tm, d), lambda i: (i, 0)),
            pl.BlockSpec((tm, d), lambda i: (i, 1)),
            pl.BlockSpec((tm, d), lambda i: (i, 2)),
            pl.BlockSpec((1, MEM_W, m), lambda i: (i // per_b, 0, 0)),
            pl.BlockSpec((1, m, MEM_W), lambda i: (i // per_b, 0, 0)),
            _const_spec((MOBA_W, d)),
            _const_spec((HGRN_W, d)),
            _const_spec((MEM_W, d)),
            _const_spec((d, d)),
        ],
        out_specs=pl.BlockSpec((tm, d), row),
        out_shape=jax.ShapeDtypeStruct((n, d), F32),
        scratch_shapes=[pltpu.VMEM((tm, MEM_W), BF16)],
        compiler_params=_cparams(("parallel",)),
    )(x, oa, ob, pa, gates, gates, gates, mkt, mv, wa, wb, wm, wo)


def kernel(x, mem, ffn1_norm, ffn1_w1, ffn1_w3, ffn1_w2, mix_norm, w_in, hgrn_lb_logits,
           hgrn_out_norm, mem_norm, w_mem_kv, w_proj_moba, w_proj_hgrn, w_proj_mem, w_out,
           ffn2_norm, ffn2_w1, ffn2_w3, ffn2_w2, final_norm):
    batch, seq, d = x.shape
    depth = ffn1_w1.shape[0]
    assert seq % (MOBA_QTILE * MOBA_BLOCK) == 0
    assert w_in.shape[-1] == MIX_W + 3 * d
    bf = lambda a: a.astype(BF16)

    hs = jnp.arange(1, MOBA_HEADS + 1, dtype=F32)
    slopes = jnp.tile(jnp.exp2(-8.0 * hs / MOBA_HEADS), batch)
    slopes = jnp.broadcast_to(slopes[:, None, None], (batch * MOBA_HEADS, 1, HEAD_DIM))
    mix_scale = jnp.concatenate([jnp.full((MOBA_W,), HEAD_DIM ** -0.5, F32),
                                 jnp.ones((MIX_W - MOBA_W,), F32)])
    gate_scale = jnp.ones((3 * d,), F32)

    xs = x.reshape(batch * seq, d)
    for l in range(depth):
        last = l == depth - 1
        xs = _ffn(xs, ffn1_norm[l], bf(ffn1_w1[l]), bf(ffn1_w3[l]), bf(ffn1_w2[l]),
                  final_norm, final=False)

        w_l = w_in[l]
        pa = _normproj(xs, mix_norm[l], bf(w_l[:, :MIX_W]), mix_scale)
        gates = _normproj(xs, mix_norm[l], bf(w_l[:, MIX_W:]), gate_scale)

        kaug, vtb, kmean = _moba_prep(pa, slopes, batch, seq)
        ob = _hgrn(pa, hgrn_lb_logits, hgrn_out_norm[l], l, batch, seq)
        oa, ob = _moba_routed_attention(pa, kaug, vtb, kmean, slopes, batch, seq, ob)
        mkt, mv = _memkv(mem, mem_norm[l], bf(w_mem_kv[l]))
        xs = _merge(xs, oa, ob, pa, gates, mkt, mv, bf(w_proj_moba[l]), bf(w_proj_hgrn[l]),
                    bf(w_proj_mem[l]), bf(w_out[l]), batch, seq)

        xs = _ffn(xs, ffn2_norm[l], bf(ffn2_w1[l]), bf(ffn2_w3[l]), bf(ffn2_w2[l]),
                  final_norm, final=last)
    return xs.reshape(batch, seq, d)
```

```python
import functools

import jax
import jax.numpy as jnp
from jax import lax
from jax.experimental import pallas as pl
from jax.experimental.pallas import tpu as pltpu
from jax.experimental.pallas import tpu_sc as plsc

F32 = jnp.float32
BF16 = jnp.bfloat16

HEAD_DIM = 128
MOBA_HEADS = 8
MOBA_BLOCK = 256
MOBA_TOPK = 3
HGRN_HEADS = 4
MEM_HEADS = 4
RMS_EPS = 1e-6
NEG_INF = -1e30
F_MIN = 1e-20

MOBA_W = MOBA_HEADS * HEAD_DIM
HGRN_W = HGRN_HEADS * HEAD_DIM
MEM_W = MEM_HEADS * HEAD_DIM
MIX_W = 3 * MOBA_W + 4 * HGRN_W + MEM_W

VMEM_LIMIT_BYTES = 60 * 1024 * 1024

AUG_W = 2 * HEAD_DIM
AUG_BIAS = HEAD_DIM
VT_ROWS = HEAD_DIM + 16
MOBA_QTILE = 4

HGRN_SUB = 16
FFN_OUT_CHUNK = 512
SLOT_TILE = 512
ROUTED_TILES_PER_STEP = 8
SC_WINDOW = 128


def _cparams(sem):
    return pltpu.CompilerParams(dimension_semantics=sem, vmem_limit_bytes=VMEM_LIMIT_BYTES)


def _dot(a, b):
    return jnp.dot(a, b, preferred_element_type=F32)


def _dot_t0(a, b):
    return lax.dot_general(a, b, (((0,), (0,)), ((), ())), preferred_element_type=F32)


def _rms(xf, g):
    r = lax.rsqrt(jnp.mean(xf * xf, axis=-1, keepdims=True) + RMS_EPS)
    return xf * r * g


def _pick(n, want):
    if n <= want:
        return n
    t = (want // 128) * 128
    while t >= 128:
        if n % t == 0:
            return t
        t -= 128
    return n


def _const_spec(shape):
    nd = len(shape)
    return pl.BlockSpec(shape, lambda *_: (0,) * nd)


def _ffn_kernel(x_ref, g_ref, w1_ref, w3_ref, w2_ref, fg_ref, o_ref, h_ref, *, final):
    j = pl.program_id(1)

    @pl.when(j == 0)
    def _():
        xf = x_ref[...]
        h_ref[...] = _rms(xf, g_ref[...]).astype(BF16)
        o_ref[...] = xf

    h = h_ref[...]
    u = _dot(h, w1_ref[...])
    v = _dot(h, w3_ref[...])
    a = (0.5 * u * jax.nn.sigmoid(u) * v).astype(BF16)
    tn = FFN_OUT_CHUNK if o_ref.shape[1] % FFN_OUT_CHUNK == 0 else o_ref.shape[1]
    for c in range(o_ref.shape[1] // tn):
        cs = slice(c * tn, (c + 1) * tn)
        o_ref[:, cs] += _dot(a, w2_ref[:, cs])

    if final:
        @pl.when(j == pl.num_programs(1) - 1)
        def _():
            o_ref[...] = _rms(o_ref[...], fg_ref[...])


def _ffn(x, g, w1, w3, w2, final_g, *, final):
    n, d = x.shape
    dff = w1.shape[1]
    tm = _pick(n, 1024)
    tf = _pick(dff, 512)
    return pl.pallas_call(
        functools.partial(_ffn_kernel, final=final),
        name="ffn_final" if final else "ffn",
        grid=(n // tm, dff // tf),
        in_specs=[
            pl.BlockSpec((tm, d), lambda i, j: (i, 0)),
            pl.BlockSpec((1, d), lambda i, j: (0, 0)),
            pl.BlockSpec((d, tf), lambda i, j: (0, j)),
            pl.BlockSpec((d, tf), lambda i, j: (0, j)),
            pl.BlockSpec((tf, d), lambda i, j: (j, 0)),
            pl.BlockSpec((1, d), lambda i, j: (0, 0)),
        ],
        out_specs=pl.BlockSpec((tm, d), lambda i, j: (i, 0)),
        out_shape=jax.ShapeDtypeStruct((n, d), F32),
        scratch_shapes=[pltpu.VMEM((tm, d), BF16)],
        compiler_params=_cparams(("parallel", "arbitrary")),
    )(x, g.reshape(1, d), w1, w3, w2, final_g.reshape(1, d))


def _normproj_kernel(x_ref, g_ref, w_ref, s_ref, o_ref, h_ref, *, tn):
    h_ref[...] = _rms(x_ref[...], g_ref[...]).astype(BF16)
    for c in range(w_ref.shape[1] // tn):
        cs = slice(c * tn, (c + 1) * tn)
        o_ref[:, cs] = (_dot(h_ref[...], w_ref[:, cs]) * s_ref[:, cs]).astype(o_ref.dtype)


def _normproj(x, g, w, col_scale):
    n, d = x.shape
    nout = w.shape[1]
    tm = _pick(n, 512)
    tn = _pick(nout, 512)
    return pl.pallas_call(
        functools.partial(_normproj_kernel, tn=tn),
        name="normproj",
        grid=(n // tm,),
        in_specs=[
            pl.BlockSpec((tm, d), lambda i: (i, 0)),
            pl.BlockSpec((1, d), lambda i: (0, 0)),
            pl.BlockSpec((d, nout), lambda i: (0, 0), pipeline_mode=pl.Buffered(1)),
            pl.BlockSpec((1, nout), lambda i: (0, 0)),
        ],
        out_specs=pl.BlockSpec((tm, nout), lambda i: (i, 0)),
        out_shape=jax.ShapeDtypeStruct((n, nout), BF16),
        scratch_shapes=[pltpu.VMEM((tm, d), BF16)],
        compiler_params=_cparams(("parallel",)),
    )(x, g.reshape(1, d), w, col_scale.reshape(1, nout))


def _moba_prep_kernel(k_ref, v_ref, slope_ref, kaug_ref, vtb_ref, kmean_ref):
    g = pl.program_id(1)
    bs = MOBA_BLOCK
    nblk = k_ref.shape[0] // bs
    slope = slope_ref[0, 0:1, 0:1]
    lane = lax.broadcasted_iota(jnp.int32, (bs, AUG_W - HEAD_DIM), 1)
    row = lax.broadcasted_iota(jnp.int32, (bs, AUG_W - HEAD_DIM), 0).astype(F32)
    extra = jnp.where(lane == AUG_BIAS - HEAD_DIM, slope * row, 0.0).astype(BF16)
    orow = lax.broadcasted_iota(jnp.int32, (VT_ROWS - HEAD_DIM, bs), 0)
    ones_row = jnp.where(orow == 0, 1.0, 0.0).astype(BF16)
    for u in range(nblk):
        k = k_ref[u * bs:(u + 1) * bs, :]
        kmean_ref[0, pl.ds(g * nblk + u, 1), :] = jnp.mean(k.astype(F32), axis=0, keepdims=True)
        kaug_ref[0, u, :, 0:HEAD_DIM] = k
        kaug_ref[0, u, :, HEAD_DIM:AUG_W] = extra
        vt = v_ref[u * bs:(u + 1) * bs, :].astype(F32).T
        vtb_ref[0, u, 0:HEAD_DIM, :] = vt.astype(BF16)
        vtb_ref[0, u, HEAD_DIM:VT_ROWS, :] = ones_row


def _moba_prep(pa, slopes, batch, seq):
    nb = seq // MOBA_BLOCK
    grp = MOBA_QTILE
    ng = nb // grp
    bh = batch * MOBA_HEADS
    kcol = MOBA_W // HEAD_DIM
    vcol = 2 * MOBA_W // HEAD_DIM

    def kmap(n, g):
        return ((n // MOBA_HEADS) * ng + g, kcol + n % MOBA_HEADS)

    def vmap(n, g):
        return ((n // MOBA_HEADS) * ng + g, vcol + n % MOBA_HEADS)

    return pl.pallas_call(
        _moba_prep_kernel,
        name="moba_prep",
        grid=(bh, ng),
        in_specs=[
            pl.BlockSpec((grp * MOBA_BLOCK, HEAD_DIM), kmap),
            pl.BlockSpec((grp * MOBA_BLOCK, HEAD_DIM), vmap),
            pl.BlockSpec((1, 1, HEAD_DIM), lambda n, g: (n, 0, 0)),
        ],
        out_specs=[
            pl.BlockSpec((1, grp, MOBA_BLOCK, AUG_W), lambda n, g: (n, g, 0, 0)),
            pl.BlockSpec((1, grp, VT_ROWS, MOBA_BLOCK), lambda n, g: (n, g, 0, 0)),
            pl.BlockSpec((1, nb, HEAD_DIM), lambda n, g: (n, 0, 0)),
        ],
        out_shape=[
            jax.ShapeDtypeStruct((bh, nb, MOBA_BLOCK, AUG_W), BF16),
            jax.ShapeDtypeStruct((bh, nb, VT_ROWS, MOBA_BLOCK), BF16),
            jax.ShapeDtypeStruct((bh, nb, HEAD_DIM), F32),
        ],
        compiler_params=_cparams(("parallel", "arbitrary")),
    )(pa, pa, slopes)


def _moba_gate_kernel(q_ref, kmean_ref, qrow_ref, sel_ref, rank_ref, cnt_ref):
    t = pl.program_id(1)
    bs = MOBA_BLOCK
    tq = q_ref.shape[0]
    nb = kmean_ref.shape[1]

    q = q_ref[...]
    qrow_ref[...] = q.astype(F32)
    qt = q.astype(F32).T.astype(BF16)

    lane = lax.broadcasted_iota(jnp.int32, (1, tq), 1)
    own = t * (tq // bs) + lane // bs
    km = kmean_ref[0]
    km_hi = km.astype(BF16)
    km_lo = (km - km_hi.astype(F32)).astype(BF16)
    gate = _dot(km_hi, qt) + _dot(km_lo, qt)
    blk = lax.broadcasted_iota(jnp.int32, (nb, tq), 0)
    gate = jnp.where(blk < own, gate, NEG_INF)

    @pl.when(t == 0)
    def _():
        cnt_ref[...] = jnp.zeros_like(cnt_ref)

    run = cnt_ref[0][:, 0:1]
    qi = lax.broadcasted_iota(jnp.int32, (bs, bs), 0)
    qj = lax.broadcasted_iota(jnp.int32, (bs, bs), 1)
    before = jnp.where(qi < qj, 1.0, 0.0).astype(BF16)
    sels, ranks = [], []
    for r in range(MOBA_TOPK):
        mx = jnp.max(gate, axis=0, keepdims=True)
        first = jnp.min(jnp.where(gate == mx, blk, nb), axis=0, keepdims=True)
        hit = blk == first
        gate = jnp.where(hit, -jnp.inf, gate)
        valid = own > r
        oh = jnp.where(hit & valid, 1.0, 0.0)
        parts = []
        for c in range(tq // bs):
            ohc = oh[:, c * bs:(c + 1) * bs]
            prior = _dot(ohc.astype(BF16), before)
            parts.append(jnp.sum(ohc * (run + prior), axis=0, keepdims=True))
            run = run + jnp.sum(ohc, axis=1, keepdims=True)
        ranks.append(jnp.concatenate(parts, axis=1))
        sels.append(jnp.where(valid, first, -1))
    cnt_ref[0] = jnp.broadcast_to(run, cnt_ref.shape[1:])
    pad = jnp.zeros((8 - MOBA_TOPK, tq), jnp.int32)
    sel_ref[0] = jnp.concatenate(sels + [pad], axis=0)
    rank_ref[0] = jnp.concatenate([x.astype(jnp.int32) for x in ranks] + [pad], axis=0)


def _moba_gate(pa, kmean, batch, seq):
    nb = seq // MOBA_BLOCK
    tq = min(MOBA_QTILE * MOBA_BLOCK, seq)
    nt = seq // tq
    bh = batch * MOBA_HEADS

    def qmap(n, t):
        return ((n // MOBA_HEADS) * nt + t, n % MOBA_HEADS)

    return pl.pallas_call(
        _moba_gate_kernel,
        name="moba_gate",
        grid=(bh, nt),
        in_specs=[
            pl.BlockSpec((tq, HEAD_DIM), qmap),
            pl.BlockSpec((1, nb, HEAD_DIM), lambda n, t: (n, 0, 0)),
        ],
        out_specs=[
            pl.BlockSpec((tq, HEAD_DIM), lambda n, t: (n * nt + t, 0)),
            pl.BlockSpec((1, 8, tq), lambda n, t: (n, 0, t)),
            pl.BlockSpec((1, 8, tq), lambda n, t: (n, 0, t)),
            pl.BlockSpec((1, nb, HEAD_DIM), lambda n, t: (n, 0, 0)),
        ],
        out_shape=[
            jax.ShapeDtypeStruct((bh * seq, HEAD_DIM), F32),
            jax.ShapeDtypeStruct((bh, 8, seq), jnp.int32),
            jax.ShapeDtypeStruct((bh, 8, seq), jnp.int32),
            jax.ShapeDtypeStruct((bh, nb, HEAD_DIM), F32),
        ],
        compiler_params=_cparams(("parallel", "arbitrary")),
    )(pa, kmean)


def _moba_slot_kernel(sel_ref, rank_ref, base_ref, slot_ref, *, trash):
    nb = base_ref.shape[1]
    tq = sel_ref.shape[2]
    base = base_ref[0][:, 0:1]
    blk = lax.broadcasted_iota(jnp.int32, (nb, tq), 0)
    spare = trash + lax.broadcasted_iota(jnp.int32, (1, tq), 1) % SLOT_TILE
    rows = []
    for r in range(MOBA_TOPK):
        sel = sel_ref[0, r:r + 1, :]
        start = jnp.sum(jnp.where(blk == sel, base, 0.0), axis=0, keepdims=True)
        slot = start.astype(jnp.int32) + rank_ref[0, r:r + 1, :]
        rows.append(jnp.where(sel >= 0, slot, spare))
    rows.append(jnp.broadcast_to(spare, (8 - MOBA_TOPK, tq)))
    slot_ref[0] = jnp.concatenate(rows, axis=0)


def _moba_slots(sel, rank, base, trash):
    bh, _, seq = sel.shape
    nb = base.shape[1]
    tq = min(2048, seq)
    spec = pl.BlockSpec((1, 8, tq), lambda n, t: (n, 0, t))
    return pl.pallas_call(
        functools.partial(_moba_slot_kernel, trash=trash),
        name="moba_slots",
        grid=(bh, seq // tq),
        in_specs=[spec, spec, pl.BlockSpec((1, nb, HEAD_DIM), lambda n, t: (n, 0, 0))],
        out_specs=spec,
        out_shape=jax.ShapeDtypeStruct((bh, 8, seq), jnp.int32),
        compiler_params=_cparams(("parallel", "parallel")),
    )(sel, rank, base)


def _sc_mesh():
    return plsc.VectorSubcoreMesh(core_axis_name="core", subcore_axis_name="subcore")


def _sc_scatter_rows(rows, slots, n_out):
    bh8, seq = slots.shape
    bh = bh8 // 8
    nw = seq // SC_WINDOW

    @pl.kernel(out_type=jax.ShapeDtypeStruct((n_out, HEAD_DIM), rows.dtype), mesh=_sc_mesh(),
               scratch_types=[])
    def scatter(x_hbm, i_hbm, o_hbm):
        def body(x_vmem, i_vmem):
            pltpu.sync_copy(x_vmem, o_hbm.at[i_vmem.at[0]])

        pltpu.emit_pipeline(
            body,
            grid=(bh * MOBA_TOPK * nw,),
            in_specs=[
                pl.BlockSpec((SC_WINDOW, HEAD_DIM),
                             index_map=lambda i: ((i // (MOBA_TOPK * nw)) * nw + i % nw, 0)),
                pl.BlockSpec((1, SC_WINDOW),
                             index_map=lambda i: ((i // (MOBA_TOPK * nw)) * 8 + (i // nw) % MOBA_TOPK,
                                                  i % nw)),
            ],
            out_specs=[],
            core_axis_name=("core", "subcore"),
            dimension_semantics=(pltpu.PARALLEL,),
        )(x_hbm, i_hbm)

    return scatter(rows, slots)


def _sc_gather_rows(table, slots):
    bh8, seq = slots.shape
    bh = bh8 // 8
    nw = seq // SC_WINDOW
    n_out = bh * MOBA_TOPK * seq

    @pl.kernel(out_type=jax.ShapeDtypeStruct((n_out, HEAD_DIM), table.dtype), mesh=_sc_mesh())
    def gather(x_hbm, i_hbm, o_hbm):
        def body(i_vmem, o_vmem):
            pltpu.sync_copy(x_hbm.at[i_vmem.at[0]], o_vmem)

        pltpu.emit_pipeline(
            body,
            grid=(bh * MOBA_TOPK * nw,),
            in_specs=[
                pl.BlockSpec((1, SC_WINDOW),
                             index_map=lambda i: ((i // (MOBA_TOPK * nw)) * 8 + (i // nw) % MOBA_TOPK,
                                                  i % nw)),
            ],
            out_specs=[pl.BlockSpec((SC_WINDOW, HEAD_DIM), index_map=lambda i: (i, 0))],
            core_axis_name=("core", "subcore"),
            dimension_semantics=(pltpu.PARALLEL,),
        )(i_hbm, o_hbm)

    return gather(table, slots)


def _moba_routed_kernel(tb_ref, nt_ref, qs_ref, kaug_ref, vtb_ref, o_ref, sa_ref, sb_ref, sc_ref):
    n = pl.program_id(0)
    g = pl.program_id(1)
    tpg = qs_ref.shape[0] // SLOT_TILE
    tiles_cap = pl.num_programs(1) * tpg

    @pl.when(g * tpg < nt_ref[n])
    def _():
        row = lax.broadcasted_iota(jnp.int32, (AUG_W - HEAD_DIM, SLOT_TILE), 0)
        tail = jnp.where(row == AUG_BIAS - HEAD_DIM, 1.0, 0.0).astype(BF16)
        frow = lax.broadcasted_iota(jnp.int32, (HEAD_DIM, SLOT_TILE), 0)
        blocks = [tb_ref[n * tiles_cap + g * tpg + u] for u in range(tpg)]

        def scores(u):
            qt = qs_ref[u * SLOT_TILE:(u + 1) * SLOT_TILE, :].T.astype(BF16)
            return _dot(kaug_ref[0, blocks[u]], jnp.concatenate([qt, tail], axis=0))

        bufs = (sa_ref, sb_ref, sc_ref)
        sa_ref[...] = scores(0)
        sb_ref[...] = scores(1)
        for u in range(tpg):
            if u + 2 < tpg:
                bufs[(u + 2) % 3][...] = scores(u + 2)
            s = bufs[u % 3][...]
            m = jnp.max(s, axis=0, keepdims=True)
            p = jnp.exp(s - m).astype(BF16)
            acc = _dot(vtb_ref[0, blocks[u]], p)
            l = acc[HEAD_DIM:HEAD_DIM + 1]
            o = (acc[0:HEAD_DIM] / l).astype(BF16).astype(F32)
            lse = m + jnp.log(l)
            ob = lax.bitcast_convert_type(o, jnp.uint32)
            lb = lax.bitcast_convert_type(lse, jnp.uint32)
            extra = jnp.where(frow == 0, lb >> 16, jnp.where(frow == 1, lb & 0xFFFF, 0))
            packed = lax.bitcast_convert_type(ob | extra, F32)
            o_ref[u * SLOT_TILE:(u + 1) * SLOT_TILE, :] = packed.T


def _moba_routed(qs, kaug, vtb, tile_block, n_tiles, tiles_cap):
    bh, nb = kaug.shape[0], kaug.shape[1]
    tpg = ROUTED_TILES_PER_STEP
    steps = tiles_cap // tpg
    rows = tpg * SLOT_TILE

    def qmap(n, g, tb, nt):
        used = jnp.maximum((nt[n] + tpg - 1) // tpg, 1)
        return (n * steps + jnp.minimum(g, used - 1), 0)

    grid_spec = pltpu.PrefetchScalarGridSpec(
        num_scalar_prefetch=2,
        grid=(bh, steps),
        in_specs=[
            pl.BlockSpec((rows, HEAD_DIM), qmap),
            pl.BlockSpec((1, nb, MOBA_BLOCK, AUG_W), lambda n, g, tb, nt: (n, 0, 0, 0)),
            pl.BlockSpec((1, nb, VT_ROWS, MOBA_BLOCK), lambda n, g, tb, nt: (n, 0, 0, 0)),
        ],
        out_specs=pl.BlockSpec((rows, HEAD_DIM), qmap),
        scratch_shapes=[pltpu.VMEM((MOBA_BLOCK, SLOT_TILE), F32)] * 3,
    )
    return pl.pallas_call(
        _moba_routed_kernel,
        name="moba_routed",
        grid_spec=grid_spec,
        out_shape=jax.ShapeDtypeStruct((qs.shape[0], HEAD_DIM), F32),
        compiler_params=_cparams(("parallel", "arbitrary")),
    )(tile_block, n_tiles, qs, kaug, vtb)


def _moba_merge_kernel(q_ref, kaug_ref, vtb_ref, og_ref, sel_ref, slope_ref, o_ref,
                       sa_ref, sb_ref, sc_ref):
    t = pl.program_id(1)
    bs = MOBA_BLOCK
    tq = q_ref.shape[0]
    slope = slope_ref[0, 0:1, 0:1]

    qt = q_ref[...].astype(F32).T.astype(BF16)
    lane = lax.broadcasted_iota(jnp.int32, (1, tq), 1)
    tpos = (t * tq + lane).astype(F32)
    lk = lax.broadcasted_iota(jnp.int32, (bs, bs), 0)
    lq = lax.broadcasted_iota(jnp.int32, (bs, bs), 1)
    dist = (lq - lk).astype(F32)
    nq = tq // bs

    def scores(u):
        return _dot(kaug_ref[0, u, :, 0:HEAD_DIM], qt[:, u * bs:(u + 1) * bs])

    sbufs = (sa_ref, sb_ref, sc_ref)
    for u in range(min(2, nq)):
        sbufs[u][...] = scores(u)
    accs, ms = [], []
    for u in range(nq):
        if u + 2 < nq:
            sbufs[(u + 2) % 3][...] = scores(u + 2)
        s = sbufs[u % 3][...]
        s = jnp.where(dist >= 0.0, s - slope * dist, NEG_INF)
        m_u = jnp.max(s, axis=0, keepdims=True)
        p = jnp.exp(s - m_u).astype(BF16)
        accs.append(_dot(vtb_ref[0, u], p))
        ms.append(m_u)
    acc = jnp.concatenate(accs, axis=1)
    l = acc[HEAD_DIM:HEAD_DIM + 1]
    parts = [acc[0:HEAD_DIM] / l]
    lses = [jnp.concatenate(ms, axis=1) + jnp.log(l)]

    for r in range(MOBA_TOPK):
        u = lax.bitcast_convert_type(og_ref[0, r, 0].T, jnp.uint32)
        lb = ((u[0:1] & 0xFFFF) << 16) | (u[1:2] & 0xFFFF)
        lse_r = lax.bitcast_convert_type(lb, F32)
        sel_r = sel_ref[0, r:r + 1, :]
        valid = sel_r >= 0
        lses.append(jnp.where(valid, lse_r + slope * ((sel_r * bs).astype(F32) - tpos), NEG_INF))
        hi_half = lax.bitcast_convert_type((u >> 16) << 16, F32)
        parts.append(jnp.where(valid, hi_half, 0.0))
    mx = functools.reduce(jnp.maximum, lses)
    ws = [jnp.exp(x - mx) for x in lses]
    num = functools.reduce(lambda a, b: a + b, [w * o for w, o in zip(ws, parts)])
    out_t = num / functools.reduce(lambda a, b: a + b, ws)
    o_ref[...] = out_t.T.astype(o_ref.dtype)


def _moba_merge(pa, kaug, vtb, og, sel, slopes, batch, seq):
    tq = MOBA_QTILE * MOBA_BLOCK
    nt = seq // tq
    bh = batch * MOBA_HEADS
    og = og.reshape(bh, MOBA_TOPK, nt, tq, HEAD_DIM)

    def qmap(n, t):
        return ((n // MOBA_HEADS) * nt + t, n % MOBA_HEADS)

    return pl.pallas_call(
        _moba_merge_kernel,
        name="moba_merge",
        grid=(bh, nt),
        in_specs=[
            pl.BlockSpec((tq, HEAD_DIM), qmap),
            pl.BlockSpec((1, MOBA_QTILE, MOBA_BLOCK, AUG_W), lambda n, t: (n, t, 0, 0)),
            pl.BlockSpec((1, MOBA_QTILE, VT_ROWS, MOBA_BLOCK), lambda n, t: (n, t, 0, 0)),
            pl.BlockSpec((1, MOBA_TOPK, 1, tq, HEAD_DIM), lambda n, t: (n, 0, t, 0, 0)),
            pl.BlockSpec((1, 8, tq), lambda n, t: (n, 0, t)),
            pl.BlockSpec((1, 1, HEAD_DIM), lambda n, t: (n, 0, 0)),
        ],
        out_specs=pl.BlockSpec((tq, HEAD_DIM), qmap),
        out_shape=jax.ShapeDtypeStruct((batch * seq, MOBA_W), BF16),
        scratch_shapes=[pltpu.VMEM((MOBA_BLOCK, MOBA_BLOCK), F32)] * 3,
        compiler_params=_cparams(("parallel", "parallel")),
    )(pa, kaug, vtb, og, sel, slopes)


def _moba_routed_attention(pa, kaug, vtb, kmean, slopes, batch, seq, companion):
    nb = seq // MOBA_BLOCK
    bh = batch * MOBA_HEADS
    qrows, sel, rank, cnt = _moba_gate(pa, kmean, batch, seq)

    step_slots = SLOT_TILE * ROUTED_TILES_PER_STEP
    cap = -(-(MOBA_TOPK * seq + nb * SLOT_TILE) // step_slots) * step_slots
    tiles_cap = cap // SLOT_TILE
    counts = cnt[:, :, 0].astype(jnp.int32)
    padded = (counts + SLOT_TILE - 1) // SLOT_TILE * SLOT_TILE
    ends = jnp.cumsum(padded, axis=1)
    base = ends - padded + (jnp.arange(bh, dtype=jnp.int32) * cap)[:, None]
    n_tiles = ends[:, -1] // SLOT_TILE
    tile_idx = jnp.arange(tiles_cap, dtype=jnp.int32)
    tile_block = jnp.sum(ends[:, None, :] // SLOT_TILE <= tile_idx[None, :, None], axis=-1)
    tile_block = jnp.minimum(tile_block, nb - 1).astype(jnp.int32).reshape(bh * tiles_cap)
    base_b = jnp.broadcast_to(base.astype(F32)[:, :, None], (bh, nb, HEAD_DIM))

    n_rows = bh * cap + SLOT_TILE
    slots = _moba_slots(sel, rank, base_b, bh * cap).reshape(bh * 8, seq)
    qs = _sc_scatter_rows(qrows, slots, n_rows)
    qs, companion = lax.optimization_barrier((qs, companion))
    part = _moba_routed(qs, kaug, vtb, tile_block, n_tiles, tiles_cap)
    og = _sc_gather_rows(part, slots)
    return _moba_merge(pa, kaug, vtb, og, sel, slopes, batch, seq), companion


def _split3(x):
    hi = x.astype(BF16)
    r1 = x - hi.astype(F32)
    mid = r1.astype(BF16)
    lo = (r1 - mid.astype(F32)).astype(BF16)
    return hi, mid, lo


def _hgrn_kernel(qb_ref, fb_ref, ib_ref, gb_ref, lbl_ref, gn_ref, o_ref, s_ref, oi_ref, st_ref,
                 *, layer):
    tt = qb_ref.shape[0]
    sub = HGRN_SUB
    hd = HEAD_DIM

    @pl.when(pl.program_id(1) == 0)
    def _():
        s_ref[...] = jnp.zeros_like(s_ref)

    logits = lbl_ref[...]
    e = jnp.exp(logits - jnp.max(logits, axis=0, keepdims=True))
    pl_ = e / jnp.sum(e, axis=0, keepdims=True)
    lb = jnp.sum(pl_[0:layer + 1], axis=0, keepdims=True) - pl_[0:1]

    fb = fb_ref[...].astype(F32)
    t = jnp.exp(-jnp.abs(fb))
    r = 1.0 / (1.0 + t)
    sig_pos = jnp.where(fb >= 0, r, t * r)
    sig_neg = jnp.where(fb >= 0, t * r, r)
    f_gate = lb + (1.0 - lb) * sig_pos
    logf = jnp.log(jnp.maximum(f_gate, F_MIN))
    k_all = (1.0 - lb) * sig_neg
    qb = qb_ref[...].astype(F32)
    q_all = qb * jax.nn.sigmoid(qb)
    v_all = ib_ref[...].astype(F32)

    ri = lax.broadcasted_iota(jnp.int32, (tt, tt), 0)
    ci = lax.broadcasted_iota(jnp.int32, (tt, tt), 1)
    same = (ri // sub) == (ci // sub)
    tri = jnp.where(same & (ci <= ri), 1.0, 0.0).astype(BF16)
    hi, mid, lo = _split3(logf)
    gl = _dot(tri, hi) + _dot(tri, mid) + _dot(tri, lo)

    half = sub // 2
    nsub = tt // sub
    nv = tt // half
    lane_sum = jnp.ones((hd, hd), BF16)
    row_in = lax.broadcasted_iota(jnp.int32, (nv, half, hd), 1)

    def pair_terms(qx, gx, kx, vx, causal):
        n = qx.shape[0]
        out = jnp.zeros_like(qx)
        for rho in range(half):
            kr = pltpu.roll(kx, rho, 1) if rho else kx
            vr = pltpu.roll(vx, rho, 1) if rho else vx
            gr = pltpu.roll(gx[1], rho, 1) if rho else gx[1]
            prod = qx * kr * jnp.exp(gx[0] - gr)
            if causal and rho:
                prod = jnp.where(row_in >= rho, prod, 0.0)
            a = _dot(prod.reshape(n * half, hd).astype(BF16), lane_sum)
            out = out + a.reshape(n, half, hd) * vr
        return out

    for h in range(HGRN_HEADS):
        cs = slice(h * hd, (h + 1) * hd)
        q = q_all[:, cs]
        k = k_all[:, cs]
        v = v_all[:, cs]
        g = gl[:, cs]

        q3, k3, v3, g3 = (a.reshape(nv, half, hd) for a in (q, k, v, g))
        od = pair_terms(q3, (g3, g3), k3, v3, True).reshape(nsub, 2, half, hd)
        q4, k4, v4, g4 = (a.reshape(nsub, 2, half, hd) for a in (q, k, v, g))
        oh = pair_terms(q4[:, 1], (g4[:, 1], g4[:, 0]), k4[:, 0], v4[:, 0], False)
        o_diag = jnp.stack([od[:, 0], od[:, 1] + oh], axis=1).reshape(tt, hd)

        gs = g.reshape(nsub, sub, hd)
        g_end = gs[:, sub - 1:sub, :]
        qd = (q * jnp.exp(g)).astype(BF16).reshape(nsub, sub, hd)
        kd = (k.reshape(nsub, sub, hd) * jnp.exp(g_end - gs)).astype(BF16)
        vb = v.astype(BF16).reshape(nsub, sub, hd)
        dec = jnp.exp(g_end)
        upd = [_dot_t0(vb[c], kd[c]) for c in range(nsub)]
        st = s_ref[h]
        for c in range(nsub):
            st_ref[c] = st.astype(BF16)
            st = st * dec[c] + upd[c]
        s_ref[h] = st
        for c in range(nsub):
            oi_ref[c * sub:(c + 1) * sub, cs] = lax.dot_general(
                qd[c], st_ref[c], (((1,), (1,)), ((), ())), preferred_element_type=F32)

        o = oi_ref[:, cs] + o_diag
        rr = lax.rsqrt(jnp.mean(o * o, axis=-1, keepdims=True) + RMS_EPS)
        gate = jax.nn.sigmoid(gb_ref[:, cs].astype(F32))
        o_ref[:, cs] = (o * rr * gn_ref[...] * gate).astype(o_ref.dtype)


def _hgrn(pa, lb_logits, out_norm, layer, batch, seq):
    tt = min(256, seq)
    nt = seq // tt
    c0 = 3 * MOBA_W // HGRN_W
    nl = lb_logits.shape[0]

    def cmap(off):
        return lambda b, t: (b * nt + t, c0 + off)

    return pl.pallas_call(
        functools.partial(_hgrn_kernel, layer=layer),
        name="hgrn",
        grid=(batch, nt),
        in_specs=[
            pl.BlockSpec((tt, HGRN_W), cmap(0)),
            pl.BlockSpec((tt, HGRN_W), cmap(1)),
            pl.BlockSpec((tt, HGRN_W), cmap(2)),
            pl.BlockSpec((tt, HGRN_W), cmap(3)),
            pl.BlockSpec((nl, HGRN_W), lambda b, t: (0, 0)),
            pl.BlockSpec((1, HEAD_DIM), lambda b, t: (0, 0)),
        ],
        out_specs=pl.BlockSpec((tt, HGRN_W), lambda b, t: (b * nt + t, 0)),
        out_shape=jax.ShapeDtypeStruct((batch * seq, HGRN_W), BF16),
        scratch_shapes=[pltpu.VMEM((HGRN_HEADS, HEAD_DIM, HEAD_DIM), F32),
                        pltpu.VMEM((tt, HGRN_W), F32),
                        pltpu.VMEM((tt // HGRN_SUB, HEAD_DIM, HEAD_DIM), BF16)],
        compiler_params=_cparams(("parallel", "arbitrary")),
    )(pa, pa, pa, pa, lb_logits, out_norm.reshape(1, HEAD_DIM))


def _memkv_kernel(mem_ref, g_ref, w_ref, kt_ref, v_ref):
    h = _rms(mem_ref[0], g_ref[...]).astype(BF16)
    kv = _dot(h, w_ref[...])
    kt_ref[0] = kv[:, 0:MEM_W].T.astype(BF16)
    v_ref[0] = kv[:, MEM_W:].astype(BF16)


def _memkv(mem, g, w):
    b, m, d = mem.shape
    return pl.pallas_call(
        _memkv_kernel,
        name="memkv",
        grid=(b,),
        in_specs=[
            pl.BlockSpec((1, m, d), lambda i: (i, 0, 0)),
            pl.BlockSpec((1, d), lambda i: (0, 0)),
            pl.BlockSpec((d, 2 * MEM_W), lambda i: (0, 0)),
        ],
        out_specs=[
            pl.BlockSpec((1, MEM_W, m), lambda i: (i, 0, 0)),
            pl.BlockSpec((1, m, MEM_W), lambda i: (i, 0, 0)),
        ],
        out_shape=[
            jax.ShapeDtypeStruct((b, MEM_W, m), BF16),
            jax.ShapeDtypeStruct((b, m, MEM_W), BF16),
        ],
        compiler_params=_cparams(("parallel",)),
    )(mem, g.reshape(1, d), w)


def _merge_kernel(x_ref, oa_ref, ob_ref, qm_ref, g0_ref, g1_ref, g2_ref, mkt_ref, mv_ref,
                  wa_ref, wb_ref, wm_ref, wo_ref, o_ref, om_ref):
    scale = HEAD_DIM ** -0.5
    for h in range(MEM_HEADS):
        cs = slice(h * HEAD_DIM, (h + 1) * HEAD_DIM)
        s = _dot(qm_ref[:, cs], mkt_ref[0, cs, :]) * scale
        p = jnp.exp(s - jnp.max(s, axis=-1, keepdims=True))
        l = jnp.sum(p, axis=-1, keepdims=True)
        om_ref[:, cs] = (_dot(p.astype(BF16), mv_ref[0, :, cs]) / l).astype(BF16)

    y = jax.nn.sigmoid(g0_ref[...].astype(F32)) * _dot(oa_ref[...], wa_ref[...])
    y += jax.nn.sigmoid(g1_ref[...].astype(F32)) * _dot(ob_ref[...], wb_ref[...])
    y += jax.nn.sigmoid(g2_ref[...].astype(F32)) * _dot(om_ref[...], wm_ref[...])
    o_ref[...] = x_ref[...] + _dot(y.astype(BF16), wo_ref[...])


def _merge(x, oa, ob, pa, gates, mkt, mv, wa, wb, wm, wo, batch, seq):
    n, d = x.shape
    m = mv.shape[1]
    tm = _pick(seq, 256)
    per_b = seq // tm
    qcol = (3 * MOBA_W + 4 * HGRN_W) // MEM_W
    row = lambda i: (i, 0)
    return pl.pallas_call(
        _merge_kernel,
        name="merge",
        grid=(n // tm,),
        in_specs=[
            pl.BlockSpec((tm, d), row),
            pl.BlockSpec((tm, MOBA_W), row),
            pl.BlockSpec((tm, HGRN_W), row),
            pl.BlockSpec((tm, MEM_W), lambda i: (i, qcol)),
            pl.BlockSpec((tm, d), lambda i: (i, 0)),
            pl.BlockSpec((tm, d), lambda i: (i, 1)),
            pl.BlockSpec((tm, d), lambda i: (i, 2)),
            pl.BlockSpec((1, MEM_W, m), lambda i: (i // per_b, 0, 0)),
            pl.BlockSpec((1, m, MEM_W), lambda i: (i // per_b, 0, 0)),
            _const_spec((MOBA_W, d)),
            _const_spec((HGRN_W, d)),
            _const_spec((MEM_W, d)),
            _const_spec((d, d)),
        ],
        out_specs=pl.BlockSpec((tm, d), row),
        out_shape=jax.ShapeDtypeStruct((n, d), F32),
        scratch_shapes=[pltpu.VMEM((tm, MEM_W), BF16)],
        compiler_params=_cparams(("parallel",)),
    )(x, oa, ob, pa, gates, gates, gates, mkt, mv, wa, wb, wm, wo)


def kernel(x, mem, ffn1_norm, ffn1_w1, ffn1_w3, ffn1_w2, mix_norm, w_in, hgrn_lb_logits,
           hgrn_out_norm, mem_norm, w_mem_kv, w_proj_moba, w_proj_hgrn, w_proj_mem, w_out,
           ffn2_norm, ffn2_w1, ffn2_w3, ffn2_w2, final_norm):
    batch, seq, d = x.shape
    depth = ffn1_w1.shape[0]
    assert seq % (MOBA_QTILE * MOBA_BLOCK) == 0
    assert w_in.shape[-1] == MIX_W + 3 * d
    bf = lambda a: a.astype(BF16)

    hs = jnp.arange(1, MOBA_HEADS + 1, dtype=F32)
    slopes = jnp.tile(jnp.exp2(-8.0 * hs / MOBA_HEADS), batch)
    slopes = jnp.broadcast_to(slopes[:, None, None], (batch * MOBA_HEADS, 1, HEAD_DIM))
    mix_scale = jnp.concatenate([jnp.full((MOBA_W,), HEAD_DIM ** -0.5, F32),
                                 jnp.ones((MIX_W - MOBA_W,), F32)])
    gate_scale = jnp.ones((3 * d,), F32)

    xs = x.reshape(batch * seq, d)
    for l in range(depth):
        last = l == depth - 1
        xs = _ffn(xs, ffn1_norm[l], bf(ffn1_w1[l]), bf(ffn1_w3[l]), bf(ffn1_w2[l]),
                  final_norm, final=False)

        w_l = w_in[l]
        pa = _normproj(xs, mix_norm[l], bf(w_l[:, :MIX_W]), mix_scale)
        gates = _normproj(xs, mix_norm[l], bf(w_l[:, MIX_W:]), gate_scale)

        kaug, vtb, kmean = _moba_prep(pa, slopes, batch, seq)
        ob = _hgrn(pa, hgrn_lb_logits, hgrn_out_norm[l], l, batch, seq)
        oa, ob = _moba_routed_attention(pa, kaug, vtb, kmean, slopes, batch, seq, ob)
        mkt, mv = _memkv(mem, mem_norm[l], bf(w_mem_kv[l]))
        xs = _merge(xs, oa, ob, pa, gates, mkt, mv, bf(w_proj_moba[l]), bf(w_proj_hgrn[l]),
                    bf(w_proj_mem[l]), bf(w_out[l]), batch, seq)

        xs = _ffn(xs, ffn2_norm[l], bf(ffn2_w1[l]), bf(ffn2_w3[l]), bf(ffn2_w2[l]),
                  final_norm, final=last)
    return xs.reshape(batch, seq, d)
```

```python
import functools

import jax
import jax.numpy as jnp
from jax import lax
from jax.experimental import pallas as pl
from jax.experimental.pallas import tpu as pltpu
from jax.experimental.pallas import tpu_sc as plsc

F32 = jnp.float32
BF16 = jnp.bfloat16

HEAD_DIM = 128
MOBA_HEADS = 8
MOBA_BLOCK = 256
MOBA_TOPK = 3
HGRN_HEADS = 4
MEM_HEADS = 4
RMS_EPS = 1e-6
NEG_INF = -1e30
F_MIN = 1e-20

MOBA_W = MOBA_HEADS * HEAD_DIM
HGRN_W = HGRN_HEADS * HEAD_DIM
MEM_W = MEM_HEADS * HEAD_DIM
MIX_W = 3 * MOBA_W + 4 * HGRN_W + MEM_W

VMEM_LIMIT_BYTES = 60 * 1024 * 1024

AUG_W = 2 * HEAD_DIM
AUG_BIAS = HEAD_DIM
VT_ROWS = HEAD_DIM + 16
MOBA_QTILE = 4

HGRN_SUB = 16
HGRN_TT = 256
FFN_TM = 1024
FFN_TF = 512
FFN_OUT_CHUNK = 512
PROJ_TM = 512
PROJ_TN = 512
MERGE_TM = 256
SLOTS_TQ = 2048
CAST_BLOCK_BYTES = 8 * 1024 * 1024
SLOT_TILE = 512
ROUTED_TILES_PER_STEP = 8
SC_WINDOW = 128


def _cparams(sem):
    return pltpu.CompilerParams(dimension_semantics=sem, vmem_limit_bytes=VMEM_LIMIT_BYTES)


def _dot(a, b):
    return jnp.dot(a, b, preferred_element_type=F32)


def _dot_t0(a, b):
    return lax.dot_general(a, b, (((0,), (0,)), ((), ())), preferred_element_type=F32)


def _rms(xf, g):
    r = lax.rsqrt(jnp.mean(xf * xf, axis=-1, keepdims=True) + RMS_EPS)
    return xf * r * g


def _pick(n, want):
    if n <= want:
        return n
    t = (want // 128) * 128
    while t >= 128:
        if n % t == 0:
            return t
        t -= 128
    return n


def _const_spec(shape):
    nd = len(shape)
    return pl.BlockSpec(shape, lambda *_: (0,) * nd)


def _cast_kernel(w_ref, *o_refs, splits):
    lo = 0
    for o_ref, width in zip(o_refs, splits):
        o_ref[...] = w_ref[0, :, lo:lo + width].astype(BF16)
        lo += width


def _cast_layer(w_stack, layer, splits=None):
    _, rows, cols = w_stack.shape
    splits = (cols,) if splits is None else tuple(splits)
    assert sum(splits) == cols
    tr = rows
    while tr * cols * 4 > CAST_BLOCK_BYTES and tr % 32 == 0:
        tr //= 2
    outs = pl.pallas_call(
        functools.partial(_cast_kernel, splits=splits),
        name="cast",
        grid=(rows // tr,),
        in_specs=[pl.BlockSpec((1, tr, cols), lambda i: (layer, i, 0))],
        out_specs=[pl.BlockSpec((tr, w), lambda i: (i, 0)) for w in splits],
        out_shape=[jax.ShapeDtypeStruct((rows, w), BF16) for w in splits],
        compiler_params=_cparams(("parallel",)),
    )(w_stack)
    return outs[0] if len(splits) == 1 else outs


def _ffn_kernel(x_ref, g_ref, w1_ref, w3_ref, w2_ref, fg_ref, o_ref, h_ref, *, final):
    j = pl.program_id(1)

    @pl.when(j == 0)
    def _():
        xf = x_ref[...]
        h_ref[...] = _rms(xf, g_ref[...]).astype(BF16)
        o_ref[...] = xf

    h = h_ref[...]
    u = _dot(h, w1_ref[...])
    v = _dot(h, w3_ref[...])
    a = (0.5 * u * jax.nn.sigmoid(u) * v).astype(BF16)
    tn = FFN_OUT_CHUNK if o_ref.shape[1] % FFN_OUT_CHUNK == 0 else o_ref.shape[1]
    for c in range(o_ref.shape[1] // tn):
        cs = slice(c * tn, (c + 1) * tn)
        o_ref[:, cs] += _dot(a, w2_ref[:, cs])

    if final:
        @pl.when(j == pl.num_programs(1) - 1)
        def _():
            o_ref[...] = _rms(o_ref[...], fg_ref[...])


def _ffn(x, g, w1, w3, w2, final_g, *, final):
    n, d = x.shape
    dff = w1.shape[1]
    tm = _pick(n, FFN_TM)
    tf = _pick(dff, FFN_TF)
    return pl.pallas_call(
        functools.partial(_ffn_kernel, final=final),
        name="ffn_final" if final else "ffn",
        grid=(n // tm, dff // tf),
        in_specs=[
            pl.BlockSpec((tm, d), lambda i, j: (i, 0)),
            pl.BlockSpec((1, d), lambda i, j: (0, 0)),
            pl.BlockSpec((d, tf), lambda i, j: (0, j)),
            pl.BlockSpec((d, tf), lambda i, j: (0, j)),
            pl.BlockSpec((tf, d), lambda i, j: (j, 0)),
            pl.BlockSpec((1, d), lambda i, j: (0, 0)),
        ],
        out_specs=pl.BlockSpec((tm, d), lambda i, j: (i, 0)),
        out_shape=jax.ShapeDtypeStruct((n, d), F32),
        scratch_shapes=[pltpu.VMEM((tm, d), BF16)],
        compiler_params=_cparams(("parallel", "arbitrary")),
    )(x, g.reshape(1, d), w1, w3, w2, final_g.reshape(1, d))


def _normproj_kernel(x_ref, g_ref, w_ref, s_ref, o_ref, h_ref, *, tn):
    h_ref[...] = _rms(x_ref[...], g_ref[...]).astype(BF16)
    for c in range(w_ref.shape[1] // tn):
        cs = slice(c * tn, (c + 1) * tn)
        o_ref[:, cs] = (_dot(h_ref[...], w_ref[:, cs]) * s_ref[:, cs]).astype(o_ref.dtype)


def _normproj(x, g, w, col_scale):
    n, d = x.shape
    nout = w.shape[1]
    tm = _pick(n, PROJ_TM)
    tn = _pick(nout, PROJ_TN)
    return pl.pallas_call(
        functools.partial(_normproj_kernel, tn=tn),
        name="normproj",
        grid=(n // tm,),
        in_specs=[
            pl.BlockSpec((tm, d), lambda i: (i, 0)),
            pl.BlockSpec((1, d), lambda i: (0, 0)),
            pl.BlockSpec((d, nout), lambda i: (0, 0), pipeline_mode=pl.Buffered(1)),
            pl.BlockSpec((1, nout), lambda i: (0, 0)),
        ],
        out_specs=pl.BlockSpec((tm, nout), lambda i: (i, 0)),
        out_shape=jax.ShapeDtypeStruct((n, nout), BF16),
        scratch_shapes=[pltpu.VMEM((tm, d), BF16)],
        compiler_params=_cparams(("parallel",)),
    )(x, g.reshape(1, d), w, col_scale.reshape(1, nout))


def _moba_prep_kernel(k_ref, v_ref, slope_ref, kaug_ref, vtb_ref, kmean_ref):
    g = pl.program_id(1)
    bs = MOBA_BLOCK
    nblk = k_ref.shape[0] // bs
    slope = slope_ref[0, 0:1, 0:1]
    lane = lax.broadcasted_iota(jnp.int32, (bs, AUG_W - HEAD_DIM), 1)
    row = lax.broadcasted_iota(jnp.int32, (bs, AUG_W - HEAD_DIM), 0).astype(F32)
    extra = jnp.where(lane == AUG_BIAS - HEAD_DIM, slope * row, 0.0).astype(BF16)
    orow = lax.broadcasted_iota(jnp.int32, (VT_ROWS - HEAD_DIM, bs), 0)
    ones_row = jnp.where(orow == 0, 1.0, 0.0).astype(BF16)
    for u in range(nblk):
        k = k_ref[u * bs:(u + 1) * bs, :]
        kmean_ref[0, pl.ds(g * nblk + u, 1), :] = jnp.mean(k.astype(F32), axis=0, keepdims=True)
        kaug_ref[0, u, :, 0:HEAD_DIM] = k
        kaug_ref[0, u, :, HEAD_DIM:AUG_W] = extra
        vt = v_ref[u * bs:(u + 1) * bs, :].astype(F32).T
        vtb_ref[0, u, 0:HEAD_DIM, :] = vt.astype(BF16)
        vtb_ref[0, u, HEAD_DIM:VT_ROWS, :] = ones_row


def _moba_prep(pa, slopes, batch, seq):
    nb = seq // MOBA_BLOCK
    grp = MOBA_QTILE
    ng = nb // grp
    bh = batch * MOBA_HEADS
    kcol = MOBA_W // HEAD_DIM
    vcol = 2 * MOBA_W // HEAD_DIM

    def kmap(n, g):
        return ((n // MOBA_HEADS) * ng + g, kcol + n % MOBA_HEADS)

    def vmap(n, g):
        return ((n // MOBA_HEADS) * ng + g, vcol + n % MOBA_HEADS)

    return pl.pallas_call(
        _moba_prep_kernel,
        name="moba_prep",
        grid=(bh, ng),
        in_specs=[
            pl.BlockSpec((grp * MOBA_BLOCK, HEAD_DIM), kmap),
            pl.BlockSpec((grp * MOBA_BLOCK, HEAD_DIM), vmap),
            pl.BlockSpec((1, 1, HEAD_DIM), lambda n, g: (n, 0, 0)),
        ],
        out_specs=[
            pl.BlockSpec((1, grp, MOBA_BLOCK, AUG_W), lambda n, g: (n, g, 0, 0)),
            pl.BlockSpec((1, grp, VT_ROWS, MOBA_BLOCK), lambda n, g: (n, g, 0, 0)),
            pl.BlockSpec((1, nb, HEAD_DIM), lambda n, g: (n, 0, 0)),
        ],
        out_shape=[
            jax.ShapeDtypeStruct((bh, nb, MOBA_BLOCK, AUG_W), BF16),
            jax.ShapeDtypeStruct((bh, nb, VT_ROWS, MOBA_BLOCK), BF16),
            jax.ShapeDtypeStruct((bh, nb, HEAD_DIM), F32),
        ],
        compiler_params=_cparams(("parallel", "arbitrary")),
    )(pa, pa, slopes)


def _moba_gate_kernel(q_ref, kmean_ref, qrow_ref, sel_ref, rank_ref, cnt_ref):
    t = pl.program_id(1)
    bs = MOBA_BLOCK
    tq = q_ref.shape[0]
    nb = kmean_ref.shape[1]

    q = q_ref[...]
    qrow_ref[...] = q.astype(F32)
    qt = q.astype(F32).T.astype(BF16)

    lane = lax.broadcasted_iota(jnp.int32, (1, tq), 1)
    own = t * (tq // bs) + lane // bs
    km = kmean_ref[0]
    km_hi = km.astype(BF16)
    km_lo = (km - km_hi.astype(F32)).astype(BF16)
    gate = _dot(km_hi, qt) + _dot(km_lo, qt)
    blk = lax.broadcasted_iota(jnp.int32, (nb, tq), 0)
    gate = jnp.where(blk < own, gate, NEG_INF)

    @pl.when(t == 0)
    def _():
        cnt_ref[...] = jnp.zeros_like(cnt_ref)

    run = cnt_ref[0][:, 0:1]
    qi = lax.broadcasted_iota(jnp.int32, (bs, bs), 0)
    qj = lax.broadcasted_iota(jnp.int32, (bs, bs), 1)
    before = jnp.where(qi < qj, 1.0, 0.0).astype(BF16)
    sels, ranks = [], []
    for r in range(MOBA_TOPK):
        mx = jnp.max(gate, axis=0, keepdims=True)
        first = jnp.min(jnp.where(gate == mx, blk, nb), axis=0, keepdims=True)
        hit = blk == first
        gate = jnp.where(hit, -jnp.inf, gate)
        valid = own > r
        oh = jnp.where(hit & valid, 1.0, 0.0)
        parts = []
        for c in range(tq // bs):
            ohc = oh[:, c * bs:(c + 1) * bs]
            prior = _dot(ohc.astype(BF16), before)
            parts.append(jnp.sum(ohc * (run + prior), axis=0, keepdims=True))
            run = run + jnp.sum(ohc, axis=1, keepdims=True)
        ranks.append(jnp.concatenate(parts, axis=1))
        sels.append(jnp.where(valid, first, -1))
    cnt_ref[0] = jnp.broadcast_to(run, cnt_ref.shape[1:])
    pad = jnp.zeros((8 - MOBA_TOPK, tq), jnp.int32)
    sel_ref[0] = jnp.concatenate(sels + [pad], axis=0)
    rank_ref[0] = jnp.concatenate([x.astype(jnp.int32) for x in ranks] + [pad], axis=0)


def _moba_gate(pa, kmean, batch, seq):
    nb = seq // MOBA_BLOCK
    tq = min(MOBA_QTILE * MOBA_BLOCK, seq)
    nt = seq // tq
    bh = batch * MOBA_HEADS

    def qmap(n, t):
        return ((n // MOBA_HEADS) * nt + t, n % MOBA_HEADS)

    return pl.pallas_call(
        _moba_gate_kernel,
        name="moba_gate",
        grid=(bh, nt),
        in_specs=[
            pl.BlockSpec((tq, HEAD_DIM), qmap),
            pl.BlockSpec((1, nb, HEAD_DIM), lambda n, t: (n, 0, 0)),
        ],
        out_specs=[
            pl.BlockSpec((tq, HEAD_DIM), lambda n, t: (n * nt + t, 0)),
            pl.BlockSpec((1, 8, tq), lambda n, t: (n, 0, t)),
            pl.BlockSpec((1, 8, tq), lambda n, t: (n, 0, t)),
            pl.BlockSpec((1, nb, HEAD_DIM), lambda n, t: (n, 0, 0)),
        ],
        out_shape=[
            jax.ShapeDtypeStruct((bh * seq, HEAD_DIM), F32),
            jax.ShapeDtypeStruct((bh, 8, seq), jnp.int32),
            jax.ShapeDtypeStruct((bh, 8, seq), jnp.int32),
            jax.ShapeDtypeStruct((bh, nb, HEAD_DIM), F32),
        ],
        compiler_params=_cparams(("parallel", "arbitrary")),
    )(pa, kmean)


def _moba_slot_kernel(sel_ref, rank_ref, base_ref, slot_ref, *, trash):
    nb = base_ref.shape[1]
    tq = sel_ref.shape[2]
    base = base_ref[0][:, 0:1]
    blk = lax.broadcasted_iota(jnp.int32, (nb, tq), 0)
    spare = trash + lax.broadcasted_iota(jnp.int32, (1, tq), 1) % SLOT_TILE
    rows = []
    for r in range(MOBA_TOPK):
        sel = sel_ref[0, r:r + 1, :]
        start = jnp.sum(jnp.where(blk == sel, base, 0.0), axis=0, keepdims=True)
        slot = start.astype(jnp.int32) + rank_ref[0, r:r + 1, :]
        rows.append(jnp.where(sel >= 0, slot, spare))
    rows.append(jnp.broadcast_to(spare, (8 - MOBA_TOPK, tq)))
    slot_ref[0] = jnp.concatenate(rows, axis=0)


def _moba_slots(sel, rank, base, trash):
    bh, _, seq = sel.shape
    nb = base.shape[1]
    tq = min(SLOTS_TQ, seq)
    spec = pl.BlockSpec((1, 8, tq), lambda n, t: (n, 0, t))
    return pl.pallas_call(
        functools.partial(_moba_slot_kernel, trash=trash),
        name="moba_slots",
        grid=(bh, seq // tq),
        in_specs=[spec, spec, pl.BlockSpec((1, nb, HEAD_DIM), lambda n, t: (n, 0, 0))],
        out_specs=spec,
        out_shape=jax.ShapeDtypeStruct((bh, 8, seq), jnp.int32),
        compiler_params=_cparams(("parallel", "parallel")),
    )(sel, rank, base)


def _sc_mesh():
    return plsc.VectorSubcoreMesh(core_axis_name="core", subcore_axis_name="subcore")


def _sc_scatter_rows(rows, slots, n_out):
    bh8, seq = slots.shape
    bh = bh8 // 8
    nw = seq // SC_WINDOW

    @pl.kernel(out_type=jax.ShapeDtypeStruct((n_out, HEAD_DIM), rows.dtype), mesh=_sc_mesh(),
               scratch_types=[])
    def scatter(x_hbm, i_hbm, o_hbm):
        def body(x_vmem, i_vmem):
            pltpu.sync_copy(x_vmem, o_hbm.at[i_vmem.at[0]])

        pltpu.emit_pipeline(
            body,
            grid=(bh * MOBA_TOPK * nw,),
            in_specs=[
                pl.BlockSpec((SC_WINDOW, HEAD_DIM),
                             index_map=lambda i: ((i // (MOBA_TOPK * nw)) * nw + i % nw, 0)),
                pl.BlockSpec((1, SC_WINDOW),
                             index_map=lambda i: ((i // (MOBA_TOPK * nw)) * 8 + (i // nw) % MOBA_TOPK,
                                                  i % nw)),
            ],
            out_specs=[],
            core_axis_name=("core", "subcore"),
            dimension_semantics=(pltpu.PARALLEL,),
        )(x_hbm, i_hbm)

    return scatter(rows, slots)


def _sc_gather_rows(table, slots):
    bh8, seq = slots.shape
    bh = bh8 // 8
    nw = seq // SC_WINDOW
    n_out = bh * MOBA_TOPK * seq

    @pl.kernel(out_type=jax.ShapeDtypeStruct((n_out, HEAD_DIM), table.dtype), mesh=_sc_mesh())
    def gather(x_hbm, i_hbm, o_hbm):
        def body(i_vmem, o_vmem):
            pltpu.sync_copy(x_hbm.at[i_vmem.at[0]], o_vmem)

        pltpu.emit_pipeline(
            body,
            grid=(bh * MOBA_TOPK * nw,),
            in_specs=[
                pl.BlockSpec((1, SC_WINDOW),
                             index_map=lambda i: ((i // (MOBA_TOPK * nw)) * 8 + (i // nw) % MOBA_TOPK,
                                                  i % nw)),
            ],
            out_specs=[pl.BlockSpec((SC_WINDOW, HEAD_DIM), index_map=lambda i: (i, 0))],
            core_axis_name=("core", "subcore"),
            dimension_semantics=(pltpu.PARALLEL,),
        )(i_hbm, o_hbm)

    return gather(table, slots)


def _moba_routed_kernel(tb_ref, nt_ref, qs_ref, kaug_ref, vtb_ref, o_ref, sa_ref, sb_ref, sc_ref):
    n = pl.program_id(0)
    g = pl.program_id(1)
    tpg = qs_ref.shape[0] // SLOT_TILE
    tiles_cap = pl.num_programs(1) * tpg

    @pl.when(g * tpg < nt_ref[n])
    def _():
        row = lax.broadcasted_iota(jnp.int32, (AUG_W - HEAD_DIM, SLOT_TILE), 0)
        tail = jnp.where(row == AUG_BIAS - HEAD_DIM, 1.0, 0.0).astype(BF16)
        frow = lax.broadcasted_iota(jnp.int32, (HEAD_DIM, SLOT_TILE), 0)
        blocks = [tb_ref[n * tiles_cap + g * tpg + u] for u in range(tpg)]

        def scores(u):
            qt = qs_ref[u * SLOT_TILE:(u + 1) * SLOT_TILE, :].T.astype(BF16)
            return _dot(kaug_ref[0, blocks[u]], jnp.concatenate([qt, tail], axis=0))

        bufs = (sa_ref, sb_ref, sc_ref)
        sa_ref[...] = scores(0)
        sb_ref[...] = scores(1)
        for u in range(tpg):
            if u + 2 < tpg:
                bufs[(u + 2) % 3][...] = scores(u + 2)
            s = bufs[u % 3][...]
            m = jnp.max(s, axis=0, keepdims=True)
            p = jnp.exp(s - m).astype(BF16)
            acc = _dot(vtb_ref[0, blocks[u]], p)
            l = acc[HEAD_DIM:HEAD_DIM + 1]
            o = (acc[0:HEAD_DIM] / l).astype(BF16).astype(F32)
            lse = m + jnp.log(l)
            ob = lax.bitcast_convert_type(o, jnp.uint32)
            lb = lax.bitcast_convert_type(lse, jnp.uint32)
            extra = jnp.where(frow == 0, lb >> 16, jnp.where(frow == 1, lb & 0xFFFF, 0))
            packed = lax.bitcast_convert_type(ob | extra, F32)
            o_ref[u * SLOT_TILE:(u + 1) * SLOT_TILE, :] = packed.T


def _moba_routed(qs, kaug, vtb, tile_block, n_tiles, tiles_cap):
    bh, nb = kaug.shape[0], kaug.shape[1]
    tpg = ROUTED_TILES_PER_STEP
    steps = tiles_cap // tpg
    rows = tpg * SLOT_TILE

    def qmap(n, g, tb, nt):
        used = jnp.maximum((nt[n] + tpg - 1) // tpg, 1)
        return (n * steps + jnp.minimum(g, used - 1), 0)

    grid_spec = pltpu.PrefetchScalarGridSpec(
        num_scalar_prefetch=2,
        grid=(bh, steps),
        in_specs=[
            pl.BlockSpec((rows, HEAD_DIM), qmap),
            pl.BlockSpec((1, nb, MOBA_BLOCK, AUG_W), lambda n, g, tb, nt: (n, 0, 0, 0)),
            pl.BlockSpec((1, nb, VT_ROWS, MOBA_BLOCK), lambda n, g, tb, nt: (n, 0, 0, 0)),
        ],
        out_specs=pl.BlockSpec((rows, HEAD_DIM), qmap),
        scratch_shapes=[pltpu.VMEM((MOBA_BLOCK, SLOT_TILE), F32)] * 3,
    )
    return pl.pallas_call(
        _moba_routed_kernel,
        name="moba_routed",
        grid_spec=grid_spec,
        out_shape=jax.ShapeDtypeStruct((qs.shape[0], HEAD_DIM), F32),
        compiler_params=_cparams(("parallel", "arbitrary")),
    )(tile_block, n_tiles, qs, kaug, vtb)


def _moba_merge_kernel(q_ref, kaug_ref, vtb_ref, og_ref, sel_ref, slope_ref, o_ref,
                       sa_ref, sb_ref, sc_ref):
    t = pl.program_id(1)
    bs = MOBA_BLOCK
    tq = q_ref.shape[0]
    slope = slope_ref[0, 0:1, 0:1]

    qt = q_ref[...].astype(F32).T.astype(BF16)
    lane = lax.broadcasted_iota(jnp.int32, (1, tq), 1)
    tpos = (t * tq + lane).astype(F32)
    lk = lax.broadcasted_iota(jnp.int32, (bs, bs), 0)
    lq = lax.broadcasted_iota(jnp.int32, (bs, bs), 1)
    dist = (lq - lk).astype(F32)
    nq = tq // bs

    def scores(u):
        return _dot(kaug_ref[0, u, :, 0:HEAD_DIM], qt[:, u * bs:(u + 1) * bs])

    sbufs = (sa_ref, sb_ref, sc_ref)
    for u in range(min(2, nq)):
        sbufs[u][...] = scores(u)
    accs, ms = [], []
    for u in range(nq):
        if u + 2 < nq:
            sbufs[(u + 2) % 3][...] = scores(u + 2)
        s = sbufs[u % 3][...]
        s = jnp.where(dist >= 0.0, s - slope * dist, NEG_INF)
        m_u = jnp.max(s, axis=0, keepdims=True)
        p = jnp.exp(s - m_u).astype(BF16)
        accs.append(_dot(vtb_ref[0, u], p))
        ms.append(m_u)
    acc = jnp.concatenate(accs, axis=1)
    l = acc[HEAD_DIM:HEAD_DIM + 1]
    parts = [acc[0:HEAD_DIM] / l]
    lses = [jnp.concatenate(ms, axis=1) + jnp.log(l)]

    for r in range(MOBA_TOPK):
        u = lax.bitcast_convert_type(og_ref[0, r, 0].T, jnp.uint32)
        lb = ((u[0:1] & 0xFFFF) << 16) | (u[1:2] & 0xFFFF)
        lse_r = lax.bitcast_convert_type(lb, F32)
        sel_r = sel_ref[0, r:r + 1, :]
        valid = sel_r >= 0
        lses.append(jnp.where(valid, lse_r + slope * ((sel_r * bs).astype(F32) - tpos), NEG_INF))
        hi_half = lax.bitcast_convert_type((u >> 16) << 16, F32)
        parts.append(jnp.where(valid, hi_half, 0.0))
    mx = functools.reduce(jnp.maximum, lses)
    ws = [jnp.exp(x - mx) for x in lses]
    num = functools.reduce(lambda a, b: a + b, [w * o for w, o in zip(ws, parts)])
    out_t = num / functools.reduce(lambda a, b: a + b, ws)
    o_ref[...] = out_t.T.astype(o_ref.dtype)


def _moba_merge(pa, kaug, vtb, og, sel, slopes, batch, seq):
    tq = MOBA_QTILE * MOBA_BLOCK
    nt = seq // tq
    bh = batch * MOBA_HEADS
    og = og.reshape(bh, MOBA_TOPK, nt, tq, HEAD_DIM)

    def qmap(n, t):
        return ((n // MOBA_HEADS) * nt + t, n % MOBA_HEADS)

    return pl.pallas_call(
        _moba_merge_kernel,
        name="moba_merge",
        grid=(bh, nt),
        in_specs=[
            pl.BlockSpec((tq, HEAD_DIM), qmap),
            pl.BlockSpec((1, MOBA_QTILE, MOBA_BLOCK, AUG_W), lambda n, t: (n, t, 0, 0)),
            pl.BlockSpec((1, MOBA_QTILE, VT_ROWS, MOBA_BLOCK), lambda n, t: (n, t, 0, 0)),
            pl.BlockSpec((1, MOBA_TOPK, 1, tq, HEAD_DIM), lambda n, t: (n, 0, t, 0, 0)),
            pl.BlockSpec((1, 8, tq), lambda n, t: (n, 0, t)),
            pl.BlockSpec((1, 1, HEAD_DIM), lambda n, t: (n, 0, 0)),
        ],
        out_specs=pl.BlockSpec((tq, HEAD_DIM), qmap),
        out_shape=jax.ShapeDtypeStruct((batch * seq, MOBA_W), BF16),
        scratch_shapes=[pltpu.VMEM((MOBA_BLOCK, MOBA_BLOCK), F32)] * 3,
        compiler_params=_cparams(("parallel", "parallel")),
    )(pa, kaug, vtb, og, sel, slopes)


def _moba_routed_attention(pa, kaug, vtb, kmean, slopes, batch, seq, companion):
    nb = seq // MOBA_BLOCK
    bh = batch * MOBA_HEADS
    qrows, sel, rank, cnt = _moba_gate(pa, kmean, batch, seq)

    step_slots = SLOT_TILE * ROUTED_TILES_PER_STEP
    cap = -(-(MOBA_TOPK * seq + nb * SLOT_TILE) // step_slots) * step_slots
    tiles_cap = cap // SLOT_TILE
    counts = cnt[:, :, 0].astype(jnp.int32)
    padded = (counts + SLOT_TILE - 1) // SLOT_TILE * SLOT_TILE
    ends = jnp.cumsum(padded, axis=1)
    base = ends - padded + (jnp.arange(bh, dtype=jnp.int32) * cap)[:, None]
    n_tiles = ends[:, -1] // SLOT_TILE
    tile_idx = jnp.arange(tiles_cap, dtype=jnp.int32)
    tile_block = jnp.sum(ends[:, None, :] // SLOT_TILE <= tile_idx[None, :, None], axis=-1)
    tile_block = jnp.minimum(tile_block, nb - 1).astype(jnp.int32).reshape(bh * tiles_cap)
    base_b = jnp.broadcast_to(base.astype(F32)[:, :, None], (bh, nb, HEAD_DIM))

    n_rows = bh * cap + SLOT_TILE
    slots = _moba_slots(sel, rank, base_b, bh * cap).reshape(bh * 8, seq)
    qs = _sc_scatter_rows(qrows, slots, n_rows)
    qs, companion = lax.optimization_barrier((qs, companion))
    part = _moba_routed(qs, kaug, vtb, tile_block, n_tiles, tiles_cap)
    og = _sc_gather_rows(part, slots)
    return _moba_merge(pa, kaug, vtb, og, sel, slopes, batch, seq), companion


def _split3(x):
    hi = x.astype(BF16)
    r1 = x - hi.astype(F32)
    mid = r1.astype(BF16)
    lo = (r1 - mid.astype(F32)).astype(BF16)
    return hi, mid, lo


def _hgrn_kernel(qb_ref, fb_ref, ib_ref, gb_ref, lbl_ref, gn_ref, o_ref, s_ref, oi_ref, st_ref,
                 *, layer):
    tt = qb_ref.shape[0]
    sub = HGRN_SUB
    hd = HEAD_DIM

    @pl.when(pl.program_id(1) == 0)
    def _():
        s_ref[...] = jnp.zeros_like(s_ref)

    logits = lbl_ref[...]
    e = jnp.exp(logits - jnp.max(logits, axis=0, keepdims=True))
    pl_ = e / jnp.sum(e, axis=0, keepdims=True)
    lb = jnp.sum(pl_[0:layer + 1], axis=0, keepdims=True) - pl_[0:1]

    fb = fb_ref[...].astype(F32)
    t = jnp.exp(-jnp.abs(fb))
    r = 1.0 / (1.0 + t)
    sig_pos = jnp.where(fb >= 0, r, t * r)
    sig_neg = jnp.where(fb >= 0, t * r, r)
    f_gate = lb + (1.0 - lb) * sig_pos
    logf = jnp.log(jnp.maximum(f_gate, F_MIN))
    k_all = (1.0 - lb) * sig_neg
    qb = qb_ref[...].astype(F32)
    q_all = qb * jax.nn.sigmoid(qb)
    v_all = ib_ref[...].astype(F32)

    ri = lax.broadcasted_iota(jnp.int32, (tt, tt), 0)
    ci = lax.broadcasted_iota(jnp.int32, (tt, tt), 1)
    same = (ri // sub) == (ci // sub)
    tri = jnp.where(same & (ci <= ri), 1.0, 0.0).astype(BF16)
    hi, mid, lo = _split3(logf)
    gl = _dot(tri, hi) + _dot(tri, mid) + _dot(tri, lo)

    half = sub // 2
    nsub = tt // sub
    nv = tt // half
    lane_sum = jnp.ones((hd, hd), BF16)
    row_in = lax.broadcasted_iota(jnp.int32, (nv, half, hd), 1)

    def pair_terms(qx, gx, kx, vx, causal):
        n = qx.shape[0]
        out = jnp.zeros_like(qx)
        for rho in range(half):
            kr = pltpu.roll(kx, rho, 1) if rho else kx
            vr = pltpu.roll(vx, rho, 1) if rho else vx
            gr = pltpu.roll(gx[1], rho, 1) if rho else gx[1]
            prod = qx * kr * jnp.exp(gx[0] - gr)
            if causal and rho:
                prod = jnp.where(row_in >= rho, prod, 0.0)
            a = _dot(prod.reshape(n * half, hd).astype(BF16), lane_sum)
            out = out + a.reshape(n, half, hd) * vr
        return out

    for h in range(HGRN_HEADS):
        cs = slice(h * hd, (h + 1) * hd)
        q = q_all[:, cs]
        k = k_all[:, cs]
        v = v_all[:, cs]
        g = gl[:, cs]

        q3, k3, v3, g3 = (a.reshape(nv, half, hd) for a in (q, k, v, g))
        od = pair_terms(q3, (g3, g3), k3, v3, True).reshape(nsub, 2, half, hd)
        q4, k4, v4, g4 = (a.reshape(nsub, 2, half, hd) for a in (q, k, v, g))
        oh = pair_terms(q4[:, 1], (g4[:, 1], g4[:, 0]), k4[:, 0], v4[:, 0], False)
        o_diag = jnp.stack([od[:, 0], od[:, 1] + oh], axis=1).reshape(tt, hd)

        gs = g.reshape(nsub, sub, hd)
        g_end = gs[:, sub - 1:sub, :]
        qd = (q * jnp.exp(g)).astype(BF16).reshape(nsub, sub, hd)
        kd = (k.reshape(nsub, sub, hd) * jnp.exp(g_end - gs)).astype(BF16)
        vb = v.astype(BF16).reshape(nsub, sub, hd)
        dec = jnp.exp(g_end)
        upd = [_dot_t0(vb[c], kd[c]) for c in range(nsub)]
        st = s_ref[h]
        for c in range(nsub):
            st_ref[c] = st.astype(BF16)
            st = st * dec[c] + upd[c]
        s_ref[h] = st
        for c in range(nsub):
            oi_ref[c * sub:(c + 1) * sub, cs] = lax.dot_general(
                qd[c], st_ref[c], (((1,), (1,)), ((), ())), preferred_element_type=F32)

        o = oi_ref[:, cs] + o_diag
        rr = lax.rsqrt(jnp.mean(o * o, axis=-1, keepdims=True) + RMS_EPS)
        gate = jax.nn.sigmoid(gb_ref[:, cs].astype(F32))
        o_ref[:, cs] = (o * rr * gn_ref[...] * gate).astype(o_ref.dtype)


def _hgrn(pa, lb_logits, out_norm, layer, batch, seq):
    tt = min(HGRN_TT, seq)
    nt = seq // tt
    c0 = 3 * MOBA_W // HGRN_W
    nl = lb_logits.shape[0]

    def cmap(off):
        return lambda b, t: (b * nt + t, c0 + off)

    return pl.pallas_call(
        functools.partial(_hgrn_kernel, layer=layer),
        name="hgrn",
        grid=(batch, nt),
        in_specs=[
            pl.BlockSpec((tt, HGRN_W), cmap(0)),
            pl.BlockSpec((tt, HGRN_W), cmap(1)),
            pl.BlockSpec((tt, HGRN_W), cmap(2)),
            pl.BlockSpec((tt, HGRN_W), cmap(3)),
            pl.BlockSpec((nl, HGRN_W), lambda b, t: (0, 0)),
            pl.BlockSpec((1, HEAD_DIM), lambda b, t: (0, 0)),
        ],
        out_specs=pl.BlockSpec((tt, HGRN_W), lambda b, t: (b * nt + t, 0)),
        out_shape=jax.ShapeDtypeStruct((batch * seq, HGRN_W), BF16),
        scratch_shapes=[pltpu.VMEM((HGRN_HEADS, HEAD_DIM, HEAD_DIM), F32),
                        pltpu.VMEM((tt, HGRN_W), F32),
                        pltpu.VMEM((tt // HGRN_SUB, HEAD_DIM, HEAD_DIM), BF16)],
        compiler_params=_cparams(("parallel", "arbitrary")),
    )(pa, pa, pa, pa, lb_logits, out_norm.reshape(1, HEAD_DIM))


def _memkv_kernel(mem_ref, g_ref, w_ref, kt_ref, v_ref):
    h = _rms(mem_ref[0], g_ref[...]).astype(BF16)
    kv = _dot(h, w_ref[...])
    kt_ref[0] = kv[:, 0:MEM_W].T.astype(BF16)
    v_ref[0] = kv[:, MEM_W:].astype(BF16)


def _memkv(mem, g, w):
    b, m, d = mem.shape
    return pl.pallas_call(
        _memkv_kernel,
        name="memkv",
        grid=(b,),
        in_specs=[
            pl.BlockSpec((1, m, d), lambda i: (i, 0, 0)),
            pl.BlockSpec((1, d), lambda i: (0, 0)),
            pl.BlockSpec((d, 2 * MEM_W), lambda i: (0, 0)),
        ],
        out_specs=[
            pl.BlockSpec((1, MEM_W, m), lambda i: (i, 0, 0)),
            pl.BlockSpec((1, m, MEM_W), lambda i: (i, 0, 0)),
        ],
        out_shape=[
            jax.ShapeDtypeStruct((b, MEM_W, m), BF16),
            jax.ShapeDtypeStruct((b, m, MEM_W), BF16),
        ],
        compiler_params=_cparams(("parallel",)),
    )(mem, g.reshape(1, d), w)


def _merge_kernel(x_ref, oa_ref, ob_ref, qm_ref, g0_ref, g1_ref, g2_ref, mkt_ref, mv_ref,
                  wa_ref, wb_ref, wm_ref, wo_ref, o_ref, om_ref):
    scale = HEAD_DIM ** -0.5
    for h in range(MEM_HEADS):
        cs = slice(h * HEAD_DIM, (h + 1) * HEAD_DIM)
        s = _dot(qm_ref[:, cs], mkt_ref[0, cs, :]) * scale
        p = jnp.exp(s - jnp.max(s, axis=-1, keepdims=True))
        l = jnp.sum(p, axis=-1, keepdims=True)
        om_ref[:, cs] = (_dot(p.astype(BF16), mv_ref[0, :, cs]) / l).astype(BF16)

    y = jax.nn.sigmoid(g0_ref[...].astype(F32)) * _dot(oa_ref[...], wa_ref[...])
    y += jax.nn.sigmoid(g1_ref[...].astype(F32)) * _dot(ob_ref[...], wb_ref[...])
    y += jax.nn.sigmoid(g2_ref[...].astype(F32)) * _dot(om_ref[...], wm_ref[...])
    o_ref[...] = x_ref[...] + _dot(y.astype(BF16), wo_ref[...])


def _merge(x, oa, ob, pa, gates, mkt, mv, wa, wb, wm, wo, batch, seq):
    n, d = x.shape
    m = mv.shape[1]
    tm = _pick(seq, MERGE_TM)
    per_b = seq // tm
    qcol = (3 * MOBA_W + 4 * HGRN_W) // MEM_W
    row = lambda i: (i, 0)
    return pl.pallas_call(
        _merge_kernel,
        name="merge",
        grid=(n // tm,),
        in_specs=[
            pl.BlockSpec((tm, d), row),
            pl.BlockSpec((tm, MOBA_W), row),
            pl.BlockSpec((tm, HGRN_W), row),
            pl.BlockSpec((tm, MEM_W), lambda i: (i, qcol)),
            pl.BlockSpec((tm, d), lambda i: (i, 0)),
            pl.BlockSpec((tm, d), lambda i: (i, 1)),
            pl.BlockSpec((tm, d), lambda i: (i, 2)),
            pl.BlockSpec((1, MEM_W, m), lambda i: (i // per_b, 0, 0)),
            pl.BlockSpec((1, m, MEM_W), lambda i: (i // per_b, 0, 0)),
            _const_spec((MOBA_W, d)),
            _const_spec((HGRN_W, d)),
            _const_spec((MEM_W, d)),
            _const_spec((d, d)),
        ],
        out_specs=pl.BlockSpec((tm, d), row),
        out_shape=jax.ShapeDtypeStruct((n, d), F32),
        scratch_shapes=[pltpu.VMEM((tm, MEM_W), BF16)],
        compiler_params=_cparams(("parallel",)),
    )(x, oa, ob, pa, gates, gates, gates, mkt, mv, wa, wb, wm, wo)


def kernel(x, mem, ffn1_norm, ffn1_w1, ffn1_w3, ffn1_w2, mix_norm, w_in, hgrn_lb_logits,
           hgrn_out_norm, mem_norm, w_mem_kv, w_proj_moba, w_proj_hgrn, w_proj_mem, w_out,
           ffn2_norm, ffn2_w1, ffn2_w3, ffn2_w2, final_norm):
    batch, seq, d = x.shape
    depth = ffn1_w1.shape[0]
    assert seq % (MOBA_QTILE * MOBA_BLOCK) == 0
    assert w_in.shape[-1] == MIX_W + 3 * d
    hs = jnp.arange(1, MOBA_HEADS + 1, dtype=F32)
    slopes = jnp.tile(jnp.exp2(-8.0 * hs / MOBA_HEADS), batch)
    slopes = jnp.broadcast_to(slopes[:, None, None], (batch * MOBA_HEADS, 1, HEAD_DIM))
    mix_scale = jnp.concatenate([jnp.full((MOBA_W,), HEAD_DIM ** -0.5, F32),
                                 jnp.ones((MIX_W - MOBA_W,), F32)])
    gate_scale = jnp.ones((3 * d,), F32)

    xs = x.reshape(batch * seq, d)
    for l in range(depth):
        last = l == depth - 1
        bf = functools.partial(_cast_layer, layer=l)
        xs = _ffn(xs, ffn1_norm[l], bf(ffn1_w1), bf(ffn1_w3), bf(ffn1_w2), final_norm, final=False)

        w_mix, w_gates = bf(w_in, splits=(MIX_W, 3 * d))
        pa = _normproj(xs, mix_norm[l], w_mix, mix_scale)
        gates = _normproj(xs, mix_norm[l], w_gates, gate_scale)

        kaug, vtb, kmean = _moba_prep(pa, slopes, batch, seq)
        ob = _hgrn(pa, hgrn_lb_logits, hgrn_out_norm[l], l, batch, seq)
        oa, ob = _moba_routed_attention(pa, kaug, vtb, kmean, slopes, batch, seq, ob)
        mkt, mv = _memkv(mem, mem_norm[l], bf(w_mem_kv))
        xs = _merge(xs, oa, ob, pa, gates, mkt, mv, bf(w_proj_moba), bf(w_proj_hgrn),
                    bf(w_proj_mem), bf(w_out), batch, seq)

        xs = _ffn(xs, ffn2_norm[l], bf(ffn2_w1), bf(ffn2_w3), bf(ffn2_w2), final_norm, final=last)
    return xs.reshape(batch, seq, d)
```

```python
import functools

import jax
import jax.numpy as jnp
from jax import lax
from jax.experimental import pallas as pl
from jax.experimental.pallas import tpu as pltpu
from jax.experimental.pallas import tpu_sc as plsc

F32 = jnp.float32
BF16 = jnp.bfloat16

HEAD_DIM = 128
MOBA_HEADS = 8
MOBA_BLOCK = 256
MOBA_TOPK = 3
HGRN_HEADS = 4
MEM_HEADS = 4
RMS_EPS = 1e-6
NEG_INF = -1e30
F_MIN = 1e-20

MOBA_W = MOBA_HEADS * HEAD_DIM
HGRN_W = HGRN_HEADS * HEAD_DIM
MEM_W = MEM_HEADS * HEAD_DIM
MIX_W = 3 * MOBA_W + 4 * HGRN_W + MEM_W

VMEM_LIMIT_BYTES = 60 * 1024 * 1024

LANE = 128
BF16_ROWS = 16
VT_ROWS = HEAD_DIM + BF16_ROWS
MOBA_QTILE = 4

HGRN_SUB = 16
HGRN_TT = 256
FFN_TM = 1024
FFN_TF = 512
FFN_OUT_CHUNK = 512
PROJ_TM = 512
PROJ_TN = 512
MERGE_TM = 256
SLOTS_TQ = 2048
CAST_BLOCK_BYTES = 8 * 1024 * 1024
SLOT_TILE = 512
ROUTED_TILES_PER_STEP = 8
SC_WINDOW = 128


def _cparams(sem):
    return pltpu.CompilerParams(dimension_semantics=sem, vmem_limit_bytes=VMEM_LIMIT_BYTES)


def _dot(a, b):
    return jnp.dot(a, b, preferred_element_type=F32)


def _dot_t0(a, b):
    return lax.dot_general(a, b, (((0,), (0,)), ((), ())), preferred_element_type=F32)


def _rms(xf, g):
    r = lax.rsqrt(jnp.mean(xf * xf, axis=-1, keepdims=True) + RMS_EPS)
    return xf * r * g


def _pick(n, want):
    if n <= want:
        return n
    t = (want // LANE) * LANE
    while t >= LANE:
        if n % t == 0:
            return t
        t -= LANE
    return n


def _const_spec(shape):
    nd = len(shape)
    return pl.BlockSpec(shape, lambda *_: (0,) * nd)


def _cast_kernel(w_ref, *o_refs, splits):
    lo = 0
    for o_ref, width in zip(o_refs, splits):
        o_ref[...] = w_ref[0, :, lo:lo + width].astype(BF16)
        lo += width


def _cast_layer(w_stack, layer, splits=None):
    _, rows, cols = w_stack.shape
    splits = (cols,) if splits is None else tuple(splits)
    assert sum(splits) == cols
    tr = rows
    while tr * cols * 4 > CAST_BLOCK_BYTES and tr % (2 * BF16_ROWS) == 0:
        tr //= 2
    outs = pl.pallas_call(
        functools.partial(_cast_kernel, splits=splits),
        name="cast",
        grid=(rows // tr,),
        in_specs=[pl.BlockSpec((1, tr, cols), lambda i: (layer, i, 0))],
        out_specs=[pl.BlockSpec((tr, w), lambda i: (i, 0)) for w in splits],
        out_shape=[jax.ShapeDtypeStruct((rows, w), BF16) for w in splits],
        compiler_params=_cparams(("parallel",)),
    )(w_stack)
    return outs[0] if len(splits) == 1 else outs


def _ffn_kernel(x_ref, g_ref, w1_ref, w3_ref, w2_ref, fg_ref, o_ref, h_ref, *, final):
    j = pl.program_id(1)

    @pl.when(j == 0)
    def _():
        xf = x_ref[...]
        h_ref[...] = _rms(xf, g_ref[...]).astype(BF16)
        o_ref[...] = xf

    h = h_ref[...]
    u = _dot(h, w1_ref[...])
    v = _dot(h, w3_ref[...])
    a = (0.5 * u * jax.nn.sigmoid(u) * v).astype(BF16)
    tn = FFN_OUT_CHUNK if o_ref.shape[1] % FFN_OUT_CHUNK == 0 else o_ref.shape[1]
    for c in range(o_ref.shape[1] // tn):
        cs = slice(c * tn, (c + 1) * tn)
        o_ref[:, cs] += _dot(a, w2_ref[:, cs])

    if final:
        @pl.when(j == pl.num_programs(1) - 1)
        def _():
            o_ref[...] = _rms(o_ref[...], fg_ref[...])


def _ffn(x, g, w1, w3, w2, final_g, *, final):
    n, d = x.shape
    dff = w1.shape[1]
    tm = _pick(n, FFN_TM)
    tf = _pick(dff, FFN_TF)
    return pl.pallas_call(
        functools.partial(_ffn_kernel, final=final),
        name="ffn_final" if final else "ffn",
        grid=(n // tm, dff // tf),
        in_specs=[
            pl.BlockSpec((tm, d), lambda i, j: (i, 0)),
            pl.BlockSpec((1, d), lambda i, j: (0, 0)),
            pl.BlockSpec((d, tf), lambda i, j: (0, j)),
            pl.BlockSpec((d, tf), lambda i, j: (0, j)),
            pl.BlockSpec((tf, d), lambda i, j: (j, 0)),
            pl.BlockSpec((1, d), lambda i, j: (0, 0)),
        ],
        out_specs=pl.BlockSpec((tm, d), lambda i, j: (i, 0)),
        out_shape=jax.ShapeDtypeStruct((n, d), F32),
        scratch_shapes=[pltpu.VMEM((tm, d), BF16)],
        compiler_params=_cparams(("parallel", "arbitrary")),
    )(x, g.reshape(1, d), w1, w3, w2, final_g.reshape(1, d))


def _normproj_kernel(x_ref, g_ref, w_ref, s_ref, o_ref, h_ref, *, tn):
    h_ref[...] = _rms(x_ref[...], g_ref[...]).astype(BF16)
    for c in range(w_ref.shape[1] // tn):
        cs = slice(c * tn, (c + 1) * tn)
        o_ref[:, cs] = (_dot(h_ref[...], w_ref[:, cs]) * s_ref[:, cs]).astype(o_ref.dtype)


def _normproj(x, g, w, col_scale):
    n, d = x.shape
    nout = w.shape[1]
    tm = _pick(n, PROJ_TM)
    tn = _pick(nout, PROJ_TN)
    return pl.pallas_call(
        functools.partial(_normproj_kernel, tn=tn),
        name="normproj",
        grid=(n // tm,),
        in_specs=[
            pl.BlockSpec((tm, d), lambda i: (i, 0)),
            pl.BlockSpec((1, d), lambda i: (0, 0)),
            pl.BlockSpec((d, nout), lambda i: (0, 0), pipeline_mode=pl.Buffered(1)),
            pl.BlockSpec((1, nout), lambda i: (0, 0)),
        ],
        out_specs=pl.BlockSpec((tm, nout), lambda i: (i, 0)),
        out_shape=jax.ShapeDtypeStruct((n, nout), BF16),
        scratch_shapes=[pltpu.VMEM((tm, d), BF16)],
        compiler_params=_cparams(("parallel",)),
    )(x, g.reshape(1, d), w, col_scale.reshape(1, nout))


def _moba_prep_kernel(k_ref, v_ref, vtb_ref, kmean_ref):
    g = pl.program_id(1)
    bs = MOBA_BLOCK
    nblk = k_ref.shape[0] // bs
    orow = lax.broadcasted_iota(jnp.int32, (VT_ROWS - HEAD_DIM, bs), 0)
    ones_row = jnp.where(orow == 0, 1.0, 0.0).astype(BF16)
    for u in range(nblk):
        k = k_ref[u * bs:(u + 1) * bs, :]
        kmean_ref[0, pl.ds(g * nblk + u, 1), :] = jnp.mean(k.astype(F32), axis=0, keepdims=True)
        vt = v_ref[u * bs:(u + 1) * bs, :].astype(F32).T
        vtb_ref[0, u, 0:HEAD_DIM, :] = vt.astype(BF16)
        vtb_ref[0, u, HEAD_DIM:VT_ROWS, :] = ones_row


def _moba_prep(pa, batch, seq):
    nb = seq // MOBA_BLOCK
    grp = MOBA_QTILE
    ng = nb // grp
    bh = batch * MOBA_HEADS
    kcol = MOBA_W // HEAD_DIM
    vcol = 2 * MOBA_W // HEAD_DIM

    def kmap(n, g):
        return ((n // MOBA_HEADS) * ng + g, kcol + n % MOBA_HEADS)

    def vmap(n, g):
        return ((n // MOBA_HEADS) * ng + g, vcol + n % MOBA_HEADS)

    return pl.pallas_call(
        _moba_prep_kernel,
        name="moba_prep",
        grid=(bh, ng),
        in_specs=[
            pl.BlockSpec((grp * MOBA_BLOCK, HEAD_DIM), kmap),
            pl.BlockSpec((grp * MOBA_BLOCK, HEAD_DIM), vmap),
        ],
        out_specs=[
            pl.BlockSpec((1, grp, VT_ROWS, MOBA_BLOCK), lambda n, g: (n, g, 0, 0)),
            pl.BlockSpec((1, nb, HEAD_DIM), lambda n, g: (n, 0, 0)),
        ],
        out_shape=[
            jax.ShapeDtypeStruct((bh, nb, VT_ROWS, MOBA_BLOCK), BF16),
            jax.ShapeDtypeStruct((bh, nb, HEAD_DIM), F32),
        ],
        compiler_params=_cparams(("parallel", "arbitrary")),
    )(pa, pa)


def _moba_gate_kernel(q_ref, kmean_ref, qrow_ref, sel_ref, rank_ref, cnt_ref):
    t = pl.program_id(1)
    bs = MOBA_BLOCK
    tq = q_ref.shape[0]
    nb = kmean_ref.shape[1]

    q = q_ref[...]
    qrow_ref[...] = q.astype(F32)
    qt = q.astype(F32).T.astype(BF16)

    lane = lax.broadcasted_iota(jnp.int32, (1, tq), 1)
    own = t * (tq // bs) + lane // bs
    km = kmean_ref[0]
    km_hi = km.astype(BF16)
    km_lo = (km - km_hi.astype(F32)).astype(BF16)
    gate = _dot(km_hi, qt) + _dot(km_lo, qt)
    blk = lax.broadcasted_iota(jnp.int32, (nb, tq), 0)
    gate = jnp.where(blk < own, gate, NEG_INF)

    @pl.when(t == 0)
    def _():
        cnt_ref[...] = jnp.zeros_like(cnt_ref)

    run = cnt_ref[0][:, 0:1]
    qi = lax.broadcasted_iota(jnp.int32, (bs, bs), 0)
    qj = lax.broadcasted_iota(jnp.int32, (bs, bs), 1)
    before = jnp.where(qi < qj, 1.0, 0.0).astype(BF16)
    sels, ranks = [], []
    for r in range(MOBA_TOPK):
        mx = jnp.max(gate, axis=0, keepdims=True)
        first = jnp.min(jnp.where(gate == mx, blk, nb), axis=0, keepdims=True)
        hit = blk == first
        gate = jnp.where(hit, -jnp.inf, gate)
        valid = own > r
        oh = jnp.where(hit & valid, 1.0, 0.0)
        parts = []
        for c in range(tq // bs):
            ohc = oh[:, c * bs:(c + 1) * bs]
            prior = _dot(ohc.astype(BF16), before)
            parts.append(jnp.sum(ohc * (run + prior), axis=0, keepdims=True))
            run = run + jnp.sum(ohc, axis=1, keepdims=True)
        ranks.append(jnp.concatenate(parts, axis=1))
        sels.append(jnp.where(valid, first, -1))
    cnt_ref[0] = jnp.broadcast_to(run, cnt_ref.shape[1:])
    pad = jnp.zeros((8 - MOBA_TOPK, tq), jnp.int32)
    sel_ref[0] = jnp.concatenate(sels + [pad], axis=0)
    rank_ref[0] = jnp.concatenate([x.astype(jnp.int32) for x in ranks] + [pad], axis=0)


def _moba_gate(pa, kmean, batch, seq):
    nb = seq // MOBA_BLOCK
    tq = min(MOBA_QTILE * MOBA_BLOCK, seq)
    nt = seq // tq
    bh = batch * MOBA_HEADS

    def qmap(n, t):
        return ((n // MOBA_HEADS) * nt + t, n % MOBA_HEADS)

    return pl.pallas_call(
        _moba_gate_kernel,
        name="moba_gate",
        grid=(bh, nt),
        in_specs=[
            pl.BlockSpec((tq, HEAD_DIM), qmap),
            pl.BlockSpec((1, nb, HEAD_DIM), lambda n, t: (n, 0, 0)),
        ],
        out_specs=[
            pl.BlockSpec((tq, HEAD_DIM), lambda n, t: (n * nt + t, 0)),
            pl.BlockSpec((1, 8, tq), lambda n, t: (n, 0, t)),
            pl.BlockSpec((1, 8, tq), lambda n, t: (n, 0, t)),
            pl.BlockSpec((1, nb, HEAD_DIM), lambda n, t: (n, 0, 0)),
        ],
        out_shape=[
            jax.ShapeDtypeStruct((bh * seq, HEAD_DIM), F32),
            jax.ShapeDtypeStruct((bh, 8, seq), jnp.int32),
            jax.ShapeDtypeStruct((bh, 8, seq), jnp.int32),
            jax.ShapeDtypeStruct((bh, nb, HEAD_DIM), F32),
        ],
        compiler_params=_cparams(("parallel", "arbitrary")),
    )(pa, kmean)


def _moba_slot_kernel(sel_ref, rank_ref, base_ref, slot_ref, *, trash):
    nb = base_ref.shape[1]
    tq = sel_ref.shape[2]
    base = base_ref[0][:, 0:1]
    blk = lax.broadcasted_iota(jnp.int32, (nb, tq), 0)
    spare = trash + lax.broadcasted_iota(jnp.int32, (1, tq), 1) % SLOT_TILE
    rows = []
    for r in range(MOBA_TOPK):
        sel = sel_ref[0, r:r + 1, :]
        start = jnp.sum(jnp.where(blk == sel, base, 0.0), axis=0, keepdims=True)
        slot = start.astype(jnp.int32) + rank_ref[0, r:r + 1, :]
        rows.append(jnp.where(sel >= 0, slot, spare))
    rows.append(jnp.broadcast_to(spare, (8 - MOBA_TOPK, tq)))
    slot_ref[0] = jnp.concatenate(rows, axis=0)


def _moba_slots(sel, rank, base, trash):
    bh, _, seq = sel.shape
    nb = base.shape[1]
    tq = min(SLOTS_TQ, seq)
    spec = pl.BlockSpec((1, 8, tq), lambda n, t: (n, 0, t))
    return pl.pallas_call(
        functools.partial(_moba_slot_kernel, trash=trash),
        name="moba_slots",
        grid=(bh, seq // tq),
        in_specs=[spec, spec, pl.BlockSpec((1, nb, HEAD_DIM), lambda n, t: (n, 0, 0))],
        out_specs=spec,
        out_shape=jax.ShapeDtypeStruct((bh, 8, seq), jnp.int32),
        compiler_params=_cparams(("parallel", "parallel")),
    )(sel, rank, base)


def _sc_mesh():
    return plsc.VectorSubcoreMesh(core_axis_name="core", subcore_axis_name="subcore")


def _sc_scatter_rows(rows, slots, n_out):
    bh8, seq = slots.shape
    bh = bh8 // 8
    nw = seq // SC_WINDOW

    @pl.kernel(out_type=jax.ShapeDtypeStruct((n_out, HEAD_DIM), rows.dtype), mesh=_sc_mesh(),
               scratch_types=[])
    def scatter(x_hbm, i_hbm, o_hbm):
        def body(x_vmem, i_vmem):
            pltpu.sync_copy(x_vmem, o_hbm.at[i_vmem.at[0]])

        pltpu.emit_pipeline(
            body,
            grid=(bh * MOBA_TOPK * nw,),
            in_specs=[
                pl.BlockSpec((SC_WINDOW, HEAD_DIM),
                             index_map=lambda i: ((i // (MOBA_TOPK * nw)) * nw + i % nw, 0)),
                pl.BlockSpec((1, SC_WINDOW),
                             index_map=lambda i: ((i // (MOBA_TOPK * nw)) * 8 + (i // nw) % MOBA_TOPK,
                                                  i % nw)),
            ],
            out_specs=[],
            core_axis_name=("core", "subcore"),
            dimension_semantics=(pltpu.PARALLEL,),
        )(x_hbm, i_hbm)

    return scatter(rows, slots)


def _sc_gather_rows(table, slots):
    bh8, seq = slots.shape
    bh = bh8 // 8
    nw = seq // SC_WINDOW
    n_out = bh * MOBA_TOPK * seq

    @pl.kernel(out_type=jax.ShapeDtypeStruct((n_out, HEAD_DIM), table.dtype), mesh=_sc_mesh())
    def gather(x_hbm, i_hbm, o_hbm):
        def body(i_vmem, o_vmem):
            pltpu.sync_copy(x_hbm.at[i_vmem.at[0]], o_vmem)

        pltpu.emit_pipeline(
            body,
            grid=(bh * MOBA_TOPK * nw,),
            in_specs=[
                pl.BlockSpec((1, SC_WINDOW),
                             index_map=lambda i: ((i // (MOBA_TOPK * nw)) * 8 + (i // nw) % MOBA_TOPK,
                                                  i % nw)),
            ],
            out_specs=[pl.BlockSpec((SC_WINDOW, HEAD_DIM), index_map=lambda i: (i, 0))],
            core_axis_name=("core", "subcore"),
            dimension_semantics=(pltpu.PARALLEL,),
        )(i_hbm, o_hbm)

    return gather(table, slots)


def _moba_routed_kernel(tb_ref, nt_ref, qs_ref, k_ref, vtb_ref, slope_ref, o_ref,
                        sa_ref, sb_ref, sc_ref):
    n = pl.program_id(0)
    g = pl.program_id(1)
    bs = MOBA_BLOCK
    tpg = qs_ref.shape[0] // SLOT_TILE
    tiles_cap = pl.num_programs(1) * tpg

    @pl.when(g * tpg < nt_ref[n])
    def _():
        krow = lax.broadcasted_iota(jnp.int32, (bs, SLOT_TILE), 0).astype(F32)
        bias = slope_ref[0, 0:1, 0:1] * krow
        frow = lax.broadcasted_iota(jnp.int32, (HEAD_DIM, SLOT_TILE), 0)
        blocks = [tb_ref[n * tiles_cap + g * tpg + u] for u in range(tpg)]

        def scores(u):
            qt = qs_ref[u * SLOT_TILE:(u + 1) * SLOT_TILE, :].T.astype(BF16)
            k = k_ref[pl.ds(pl.multiple_of(blocks[u] * bs, bs), bs), :]
            return _dot(k, qt) + bias

        bufs = (sa_ref, sb_ref, sc_ref)
        sa_ref[...] = scores(0)
        sb_ref[...] = scores(1)
        for u in range(tpg):
            if u + 2 < tpg:
                bufs[(u + 2) % 3][...] = scores(u + 2)
            s = bufs[u % 3][...]
            m = jnp.max(s, axis=0, keepdims=True)
            p = jnp.exp(s - m).astype(BF16)
            acc = _dot(vtb_ref[0, blocks[u]], p)
            l = acc[HEAD_DIM:HEAD_DIM + 1]
            o = (acc[0:HEAD_DIM] / l).astype(BF16).astype(F32)
            lse = m + jnp.log(l)
            ob = lax.bitcast_convert_type(o, jnp.uint32)
            lb = lax.bitcast_convert_type(lse, jnp.uint32)
            extra = jnp.where(frow == 0, lb >> 16, jnp.where(frow == 1, lb & 0xFFFF, 0))
            packed = lax.bitcast_convert_type(ob | extra, F32)
            o_ref[u * SLOT_TILE:(u + 1) * SLOT_TILE, :] = packed.T


def _moba_routed(qs, pa, vtb, slopes, tile_block, n_tiles, tiles_cap):
    bh, nb = vtb.shape[0], vtb.shape[1]
    seq = nb * MOBA_BLOCK
    kcol = MOBA_W // HEAD_DIM
    tpg = ROUTED_TILES_PER_STEP
    steps = tiles_cap // tpg
    rows = tpg * SLOT_TILE

    def qmap(n, g, tb, nt):
        used = jnp.maximum((nt[n] + tpg - 1) // tpg, 1)
        return (n * steps + jnp.minimum(g, used - 1), 0)

    grid_spec = pltpu.PrefetchScalarGridSpec(
        num_scalar_prefetch=2,
        grid=(bh, steps),
        in_specs=[
            pl.BlockSpec((rows, HEAD_DIM), qmap),
            pl.BlockSpec((seq, HEAD_DIM),
                         lambda n, g, tb, nt: (n // MOBA_HEADS, kcol + n % MOBA_HEADS)),
            pl.BlockSpec((1, nb, VT_ROWS, MOBA_BLOCK), lambda n, g, tb, nt: (n, 0, 0, 0)),
            pl.BlockSpec((1, 1, HEAD_DIM), lambda n, g, tb, nt: (n, 0, 0)),
        ],
        out_specs=pl.BlockSpec((rows, HEAD_DIM), qmap),
        scratch_shapes=[pltpu.VMEM((MOBA_BLOCK, SLOT_TILE), F32)] * 3,
    )
    return pl.pallas_call(
        _moba_routed_kernel,
        name="moba_routed",
        grid_spec=grid_spec,
        out_shape=jax.ShapeDtypeStruct((qs.shape[0], HEAD_DIM), F32),
        compiler_params=_cparams(("parallel", "arbitrary")),
    )(tile_block, n_tiles, qs, pa, vtb, slopes)


def _moba_merge_kernel(q_ref, k_ref, vtb_ref, og_ref, sel_ref, slope_ref, o_ref,
                       sa_ref, sb_ref, sc_ref):
    t = pl.program_id(1)
    bs = MOBA_BLOCK
    tq = q_ref.shape[0]
    slope = slope_ref[0, 0:1, 0:1]

    qt = q_ref[...].astype(F32).T.astype(BF16)
    lane = lax.broadcasted_iota(jnp.int32, (1, tq), 1)
    tpos = (t * tq + lane).astype(F32)
    lk = lax.broadcasted_iota(jnp.int32, (bs, bs), 0)
    lq = lax.broadcasted_iota(jnp.int32, (bs, bs), 1)
    dist = (lq - lk).astype(F32)
    nq = tq // bs

    def scores(u):
        return _dot(k_ref[u * bs:(u + 1) * bs, :], qt[:, u * bs:(u + 1) * bs])

    sbufs = (sa_ref, sb_ref, sc_ref)
    for u in range(min(2, nq)):
        sbufs[u][...] = scores(u)
    accs, ms = [], []
    for u in range(nq):
        if u + 2 < nq:
            sbufs[(u + 2) % 3][...] = scores(u + 2)
        s = sbufs[u % 3][...]
        s = jnp.where(dist >= 0.0, s - slope * dist, NEG_INF)
        m_u = jnp.max(s, axis=0, keepdims=True)
        p = jnp.exp(s - m_u).astype(BF16)
        accs.append(_dot(vtb_ref[0, u], p))
        ms.append(m_u)
    acc = jnp.concatenate(accs, axis=1)
    l = acc[HEAD_DIM:HEAD_DIM + 1]
    parts = [acc[0:HEAD_DIM] / l]
    lses = [jnp.concatenate(ms, axis=1) + jnp.log(l)]

    for r in range(MOBA_TOPK):
        u = lax.bitcast_convert_type(og_ref[0, r, 0].T, jnp.uint32)
        lb = ((u[0:1] & 0xFFFF) << 16) | (u[1:2] & 0xFFFF)
        lse_r = lax.bitcast_convert_type(lb, F32)
        sel_r = sel_ref[0, r:r + 1, :]
        valid = sel_r >= 0
        lses.append(jnp.where(valid, lse_r + slope * ((sel_r * bs).astype(F32) - tpos), NEG_INF))
        hi_half = lax.bitcast_convert_type((u >> 16) << 16, F32)
        parts.append(jnp.where(valid, hi_half, 0.0))
    mx = functools.reduce(jnp.maximum, lses)
    ws = [jnp.exp(x - mx) for x in lses]
    num = functools.reduce(lambda a, b: a + b, [w * o for w, o in zip(ws, parts)])
    out_t = num / functools.reduce(lambda a, b: a + b, ws)
    o_ref[...] = out_t.T.astype(o_ref.dtype)


def _moba_merge(pa, vtb, og, sel, slopes, batch, seq):
    tq = MOBA_QTILE * MOBA_BLOCK
    nt = seq // tq
    bh = batch * MOBA_HEADS
    kcol = MOBA_W // HEAD_DIM
    og = og.reshape(bh, MOBA_TOPK, nt, tq, HEAD_DIM)

    def qmap(n, t):
        return ((n // MOBA_HEADS) * nt + t, n % MOBA_HEADS)

    def kmap(n, t):
        return ((n // MOBA_HEADS) * nt + t, kcol + n % MOBA_HEADS)

    return pl.pallas_call(
        _moba_merge_kernel,
        name="moba_merge",
        grid=(bh, nt),
        in_specs=[
            pl.BlockSpec((tq, HEAD_DIM), qmap),
            pl.BlockSpec((tq, HEAD_DIM), kmap),
            pl.BlockSpec((1, MOBA_QTILE, VT_ROWS, MOBA_BLOCK), lambda n, t: (n, t, 0, 0)),
            pl.BlockSpec((1, MOBA_TOPK, 1, tq, HEAD_DIM), lambda n, t: (n, 0, t, 0, 0)),
            pl.BlockSpec((1, 8, tq), lambda n, t: (n, 0, t)),
            pl.BlockSpec((1, 1, HEAD_DIM), lambda n, t: (n, 0, 0)),
        ],
        out_specs=pl.BlockSpec((tq, HEAD_DIM), qmap),
        out_shape=jax.ShapeDtypeStruct((batch * seq, MOBA_W), BF16),
        scratch_shapes=[pltpu.VMEM((MOBA_BLOCK, MOBA_BLOCK), F32)] * 3,
        compiler_params=_cparams(("parallel", "parallel")),
    )(pa, pa, vtb, og, sel, slopes)


def _moba_routed_attention(pa, vtb, kmean, slopes, batch, seq, companion):
    nb = seq // MOBA_BLOCK
    bh = batch * MOBA_HEADS
    qrows, sel, rank, cnt = _moba_gate(pa, kmean, batch, seq)

    step_slots = SLOT_TILE * ROUTED_TILES_PER_STEP
    cap = -(-(MOBA_TOPK * seq + nb * SLOT_TILE) // step_slots) * step_slots
    tiles_cap = cap // SLOT_TILE
    counts = cnt[:, :, 0].astype(jnp.int32)
    padded = (counts + SLOT_TILE - 1) // SLOT_TILE * SLOT_TILE
    ends = jnp.cumsum(padded, axis=1)
    base = ends - padded + (jnp.arange(bh, dtype=jnp.int32) * cap)[:, None]
    n_tiles = ends[:, -1] // SLOT_TILE
    tile_idx = jnp.arange(tiles_cap, dtype=jnp.int32)
    tile_block = jnp.sum(ends[:, None, :] // SLOT_TILE <= tile_idx[None, :, None], axis=-1)
    tile_block = jnp.minimum(tile_block, nb - 1).astype(jnp.int32).reshape(bh * tiles_cap)
    base_b = jnp.broadcast_to(base.astype(F32)[:, :, None], (bh, nb, HEAD_DIM))

    n_rows = bh * cap + SLOT_TILE
    slots = _moba_slots(sel, rank, base_b, bh * cap).reshape(bh * 8, seq)
    qs = _sc_scatter_rows(qrows, slots, n_rows)
    qs, companion = lax.optimization_barrier((qs, companion))
    part = _moba_routed(qs, pa, vtb, slopes, tile_block, n_tiles, tiles_cap)
    og = _sc_gather_rows(part, slots)
    return _moba_merge(pa, vtb, og, sel, slopes, batch, seq), companion


def _split3(x):
    hi = x.astype(BF16)
    r1 = x - hi.astype(F32)
    mid = r1.astype(BF16)
    lo = (r1 - mid.astype(F32)).astype(BF16)
    return hi, mid, lo


def _hgrn_kernel(qb_ref, fb_ref, ib_ref, gb_ref, lbl_ref, gn_ref, o_ref, s_ref, oi_ref, st_ref,
                 *, layer):
    tt = qb_ref.shape[0]
    sub = HGRN_SUB
    hd = HEAD_DIM

    @pl.when(pl.program_id(1) == 0)
    def _():
        s_ref[...] = jnp.zeros_like(s_ref)

    logits = lbl_ref[...]
    e = jnp.exp(logits - jnp.max(logits, axis=0, keepdims=True))
    pl_ = e / jnp.sum(e, axis=0, keepdims=True)
    lb = jnp.sum(pl_[0:layer + 1], axis=0, keepdims=True) - pl_[0:1]

    fb = fb_ref[...].astype(F32)
    t = jnp.exp(-jnp.abs(fb))
    r = 1.0 / (1.0 + t)
    sig_pos = jnp.where(fb >= 0, r, t * r)
    sig_neg = jnp.where(fb >= 0, t * r, r)
    f_gate = lb + (1.0 - lb) * sig_pos
    logf = jnp.log(jnp.maximum(f_gate, F_MIN))
    k_all = (1.0 - lb) * sig_neg
    qb = qb_ref[...].astype(F32)
    q_all = qb * jax.nn.sigmoid(qb)
    v_all = ib_ref[...].astype(F32)

    ri = lax.broadcasted_iota(jnp.int32, (tt, tt), 0)
    ci = lax.broadcasted_iota(jnp.int32, (tt, tt), 1)
    same = (ri // sub) == (ci // sub)
    tri = jnp.where(same & (ci <= ri), 1.0, 0.0).astype(BF16)
    hi, mid, lo = _split3(logf)
    gl = _dot(tri, hi) + _dot(tri, mid) + _dot(tri, lo)

    half = sub // 2
    nsub = tt // sub
    nv = tt // half
    lane_sum = jnp.ones((hd, hd), BF16)
    row_in = lax.broadcasted_iota(jnp.int32, (nv, half, hd), 1)

    def pair_terms(qx, gx, kx, vx, causal):
        n = qx.shape[0]
        out = jnp.zeros_like(qx)
        for rho in range(half):
            kr = pltpu.roll(kx, rho, 1) if rho else kx
            vr = pltpu.roll(vx, rho, 1) if rho else vx
            gr = pltpu.roll(gx[1], rho, 1) if rho else gx[1]
            prod = qx * kr * jnp.exp(gx[0] - gr)
            if causal and rho:
                prod = jnp.where(row_in >= rho, prod, 0.0)
            a = _dot(prod.reshape(n * half, hd).astype(BF16), lane_sum)
            out = out + a.reshape(n, half, hd) * vr
        return out

    for h in range(HGRN_HEADS):
        cs = slice(h * hd, (h + 1) * hd)
        q = q_all[:, cs]
        k = k_all[:, cs]
        v = v_all[:, cs]
        g = gl[:, cs]

        q3, k3, v3, g3 = (a.reshape(nv, half, hd) for a in (q, k, v, g))
        od = pair_terms(q3, (g3, g3), k3, v3, True).reshape(nsub, 2, half, hd)
        q4, k4, v4, g4 = (a.reshape(nsub, 2, half, hd) for a in (q, k, v, g))
        oh = pair_terms(q4[:, 1], (g4[:, 1], g4[:, 0]), k4[:, 0], v4[:, 0], False)
        o_diag = jnp.stack([od[:, 0], od[:, 1] + oh], axis=1).reshape(tt, hd)

        gs = g.reshape(nsub, sub, hd)
        g_end = gs[:, sub - 1:sub, :]
        qd = (q * jnp.exp(g)).astype(BF16).reshape(nsub, sub, hd)
        kd = (k.reshape(nsub, sub, hd) * jnp.exp(g_end - gs)).astype(BF16)
        vb = v.astype(BF16).reshape(nsub, sub, hd)
        dec = jnp.exp(g_end)
        upd = [_dot_t0(vb[c], kd[c]) for c in range(nsub)]
        st = s_ref[h]
        for c in range(nsub):
            st_ref[c] = st.astype(BF16)
            st = st * dec[c] + upd[c]
        s_ref[h] = st
        for c in range(nsub):
            oi_ref[c * sub:(c + 1) * sub, cs] = lax.dot_general(
                qd[c], st_ref[c], (((1,), (1,)), ((), ())), preferred_element_type=F32)

        o = oi_ref[:, cs] + o_diag
        rr = lax.rsqrt(jnp.mean(o * o, axis=-1, keepdims=True) + RMS_EPS)
        gate = jax.nn.sigmoid(gb_ref[:, cs].astype(F32))
        o_ref[:, cs] = (o * rr * gn_ref[...] * gate).astype(o_ref.dtype)


def _hgrn(pa, lb_logits, out_norm, layer, batch, seq):
    tt = min(HGRN_TT, seq)
    nt = seq // tt
    c0 = 3 * MOBA_W // HGRN_W
    nl = lb_logits.shape[0]

    def cmap(off):
        return lambda b, t: (b * nt + t, c0 + off)

    return pl.pallas_call(
        functools.partial(_hgrn_kernel, layer=layer),
        name="hgrn",
        grid=(batch, nt),
        in_specs=[
            pl.BlockSpec((tt, HGRN_W), cmap(0)),
            pl.BlockSpec((tt, HGRN_W), cmap(1)),
            pl.BlockSpec((tt, HGRN_W), cmap(2)),
            pl.BlockSpec((tt, HGRN_W), cmap(3)),
            pl.BlockSpec((nl, HGRN_W), lambda b, t: (0, 0)),
            pl.BlockSpec((1, HEAD_DIM), lambda b, t: (0, 0)),
        ],
        out_specs=pl.BlockSpec((tt, HGRN_W), lambda b, t: (b * nt + t, 0)),
        out_shape=jax.ShapeDtypeStruct((batch * seq, HGRN_W), BF16),
        scratch_shapes=[pltpu.VMEM((HGRN_HEADS, HEAD_DIM, HEAD_DIM), F32),
                        pltpu.VMEM((tt, HGRN_W), F32),
                        pltpu.VMEM((tt // HGRN_SUB, HEAD_DIM, HEAD_DIM), BF16)],
        compiler_params=_cparams(("parallel", "arbitrary")),
    )(pa, pa, pa, pa, lb_logits, out_norm.reshape(1, HEAD_DIM))


def _memkv_kernel(mem_ref, g_ref, w_ref, kt_ref, v_ref):
    h = _rms(mem_ref[0], g_ref[...]).astype(BF16)
    kv = _dot(h, w_ref[...])
    kt_ref[0] = kv[:, 0:MEM_W].T.astype(BF16)
    v_ref[0] = kv[:, MEM_W:].astype(BF16)


def _memkv(mem, g, w):
    b, m, d = mem.shape
    return pl.pallas_call(
        _memkv_kernel,
        name="memkv",
        grid=(b,),
        in_specs=[
            pl.BlockSpec((1, m, d), lambda i: (i, 0, 0)),
            pl.BlockSpec((1, d), lambda i: (0, 0)),
            pl.BlockSpec((d, 2 * MEM_W), lambda i: (0, 0)),
        ],
        out_specs=[
            pl.BlockSpec((1, MEM_W, m), lambda i: (i, 0, 0)),
            pl.BlockSpec((1, m, MEM_W), lambda i: (i, 0, 0)),
        ],
        out_shape=[
            jax.ShapeDtypeStruct((b, MEM_W, m), BF16),
            jax.ShapeDtypeStruct((b, m, MEM_W), BF16),
        ],
        compiler_params=_cparams(("parallel",)),
    )(mem, g.reshape(1, d), w)


def _merge_kernel(x_ref, oa_ref, ob_ref, qm_ref, g0_ref, g1_ref, g2_ref, mkt_ref, mv_ref,
                  wa_ref, wb_ref, wm_ref, wo_ref, o_ref, om_ref):
    scale = HEAD_DIM ** -0.5
    for h in range(MEM_HEADS):
        cs = slice(h * HEAD_DIM, (h + 1) * HEAD_DIM)
        s = _dot(qm_ref[:, cs], mkt_ref[0, cs, :]) * scale
        p = jnp.exp(s - jnp.max(s, axis=-1, keepdims=True))
        l = jnp.sum(p, axis=-1, keepdims=True)
        om_ref[:, cs] = (_dot(p.astype(BF16), mv_ref[0, :, cs]) / l).astype(BF16)

    y = jax.nn.sigmoid(g0_ref[...].astype(F32)) * _dot(oa_ref[...], wa_ref[...])
    y += jax.nn.sigmoid(g1_ref[...].astype(F32)) * _dot(ob_ref[...], wb_ref[...])
    y += jax.nn.sigmoid(g2_ref[...].astype(F32)) * _dot(om_ref[...], wm_ref[...])
    o_ref[...] = x_ref[...] + _dot(y.astype(BF16), wo_ref[...])


def _merge(x, oa, ob, pa, gates, mkt, mv, wa, wb, wm, wo, batch, seq):
    n, d = x.shape
    m = mv.shape[1]
    tm = _pick(seq, MERGE_TM)
    per_b = seq // tm
    qcol = (3 * MOBA_W + 4 * HGRN_W) // MEM_W
    row = lambda i: (i, 0)
    return pl.pallas_call(
        _merge_kernel,
        name="merge",
        grid=(n // tm,),
        in_specs=[
            pl.BlockSpec((tm, d), row),
            pl.BlockSpec((tm, MOBA_W), row),
            pl.BlockSpec((tm, HGRN_W), row),
            pl.BlockSpec((tm, MEM_W), lambda i: (i, qcol)),
            pl.BlockSpec((tm, d), lambda i: (i, 0)),
            pl.BlockSpec((tm, d), lambda i: (i, 1)),
            pl.BlockSpec((tm, d), lambda i: (i, 2)),
            pl.BlockSpec((1, MEM_W, m), lambda i: (i // per_b, 0, 0)),
            pl.BlockSpec((1, m, MEM_W), lambda i: (i // per_b, 0, 0)),
            _const_spec((MOBA_W, d)),
            _const_spec((HGRN_W, d)),
            _const_spec((MEM_W, d)),
            _const_spec((d, d)),
        ],
        out_specs=pl.BlockSpec((tm, d), row),
        out_shape=jax.ShapeDtypeStruct((n, d), F32),
        scratch_shapes=[pltpu.VMEM((tm, MEM_W), BF16)],
        compiler_params=_cparams(("parallel",)),
    )(x, oa, ob, pa, gates, gates, gates, mkt, mv, wa, wb, wm, wo)


def kernel(x, mem, ffn1_norm, ffn1_w1, ffn1_w3, ffn1_w2, mix_norm, w_in, hgrn_lb_logits,
           hgrn_out_norm, mem_norm, w_mem_kv, w_proj_moba, w_proj_hgrn, w_proj_mem, w_out,
           ffn2_norm, ffn2_w1, ffn2_w3, ffn2_w2, final_norm):
    batch, seq, d = x.shape
    depth = ffn1_w1.shape[0]
    assert seq % (MOBA_QTILE * MOBA_BLOCK) == 0
    assert w_in.shape[-1] == MIX_W + 3 * d
    hs = jnp.arange(1, MOBA_HEADS + 1, dtype=F32)
    slopes = jnp.tile(jnp.exp2(-8.0 * hs / MOBA_HEADS), batch)
    slopes = jnp.broadcast_to(slopes[:, None, None], (batch * MOBA_HEADS, 1, HEAD_DIM))
    mix_scale = jnp.concatenate([jnp.full((MOBA_W,), HEAD_DIM ** -0.5, F32),
                                 jnp.ones((MIX_W - MOBA_W,), F32)])
    gate_scale = jnp.ones((3 * d,), F32)

    xs = x.reshape(batch * seq, d)
    for l in range(depth):
        last = l == depth - 1
        bf = functools.partial(_cast_layer, layer=l)
        xs = _ffn(xs, ffn1_norm[l], bf(ffn1_w1), bf(ffn1_w3), bf(ffn1_w2), final_norm, final=False)

        w_mix, w_gates = bf(w_in, splits=(MIX_W, 3 * d))
        pa = _normproj(xs, mix_norm[l], w_mix, mix_scale)
        gates = _normproj(xs, mix_norm[l], w_gates, gate_scale)

        vtb, kmean = _moba_prep(pa, batch, seq)
        ob = _hgrn(pa, hgrn_lb_logits, hgrn_out_norm[l], l, batch, seq)
        oa, ob = _moba_routed_attention(pa, vtb, kmean, slopes, batch, seq, ob)
        mkt, mv = _memkv(mem, mem_norm[l], bf(w_mem_kv))
        xs = _merge(xs, oa, ob, pa, gates, mkt, mv, bf(w_proj_moba), bf(w_proj_hgrn),
                    bf(w_proj_mem), bf(w_out), batch, seq)

        xs = _ffn(xs, ffn2_norm[l], bf(ffn2_w1), bf(ffn2_w3), bf(ffn2_w2), final_norm, final=last)
    return xs.reshape(batch, seq, d)
```

```python
import functools

import jax
import jax.numpy as jnp
from jax import lax
from jax.experimental import pallas as pl
from jax.experimental.pallas import tpu as pltpu
from jax.experimental.pallas import tpu_sc as plsc

F32 = jnp.float32
BF16 = jnp.bfloat16

HEAD_DIM = 128
MOBA_HEADS = 8
MOBA_BLOCK = 256
MOBA_TOPK = 3
HGRN_HEADS = 4
MEM_HEADS = 4
RMS_EPS = 1e-6
NEG_INF = -1e30
F_MIN = 1e-20
LOG2E = 1.4426950408889634

MOBA_W = MOBA_HEADS * HEAD_DIM
HGRN_W = HGRN_HEADS * HEAD_DIM
MEM_W = MEM_HEADS * HEAD_DIM
MIX_W = 3 * MOBA_W + 4 * HGRN_W + MEM_W

VMEM_LIMIT_BYTES = 60 * 1024 * 1024

LANE = 128
BF16_ROWS = 16
VT_ROWS = HEAD_DIM + BF16_ROWS
MOBA_QTILE = 4

HGRN_SUB = 16
HGRN_TT = 256
FFN_TM = 1024
FFN_TF = 512
FFN_OUT_CHUNK = 512
PROJ_TM = 512
PROJ_TN = 512
MERGE_TM = 256
SLOTS_TQ = 2048
CAST_BLOCK_BYTES = 8 * 1024 * 1024
SLOT_TILE = 512
ROUTED_TILES_PER_STEP = 8
SC_WINDOW = 128


def _cparams(sem):
    return pltpu.CompilerParams(dimension_semantics=sem, vmem_limit_bytes=VMEM_LIMIT_BYTES)


def _dot(a, b):
    return jnp.dot(a, b, preferred_element_type=F32)


def _dot_t0(a, b):
    return lax.dot_general(a, b, (((0,), (0,)), ((), ())), preferred_element_type=F32)


def _rms(xf, g):
    r = lax.rsqrt(jnp.mean(xf * xf, axis=-1, keepdims=True) + RMS_EPS)
    return xf * r * g


def _pick(n, want):
    if n <= want:
        return n
    t = (want // LANE) * LANE
    while t >= LANE:
        if n % t == 0:
            return t
        t -= LANE
    return n


def _const_spec(shape):
    nd = len(shape)
    return pl.BlockSpec(shape, lambda *_: (0,) * nd)


def _cast_kernel(w_ref, *o_refs, splits):
    lo = 0
    for o_ref, width in zip(o_refs, splits):
        o_ref[...] = w_ref[0, :, lo:lo + width].astype(BF16)
        lo += width


def _cast_layer(w_stack, layer, splits=None):
    _, rows, cols = w_stack.shape
    splits = (cols,) if splits is None else tuple(splits)
    assert sum(splits) == cols
    tr = rows
    while tr * cols * 4 > CAST_BLOCK_BYTES and tr % (2 * BF16_ROWS) == 0:
        tr //= 2
    outs = pl.pallas_call(
        functools.partial(_cast_kernel, splits=splits),
        name="cast",
        grid=(rows // tr,),
        in_specs=[pl.BlockSpec((1, tr, cols), lambda i: (layer, i, 0))],
        out_specs=[pl.BlockSpec((tr, w), lambda i: (i, 0)) for w in splits],
        out_shape=[jax.ShapeDtypeStruct((rows, w), BF16) for w in splits],
        compiler_params=_cparams(("parallel",)),
    )(w_stack)
    return outs[0] if len(splits) == 1 else outs


def _ffn_kernel(x_ref, g_ref, w1_ref, w3_ref, w2_ref, fg_ref, o_ref, h_ref, *, final):
    j = pl.program_id(1)

    @pl.when(j == 0)
    def _():
        xf = x_ref[...]
        h_ref[...] = _rms(xf, g_ref[...]).astype(BF16)
        o_ref[...] = xf

    h = h_ref[...]
    u = _dot(h, w1_ref[...])
    v = _dot(h, w3_ref[...])
    a = (0.5 * u * jax.nn.sigmoid(u) * v).astype(BF16)
    tn = FFN_OUT_CHUNK if o_ref.shape[1] % FFN_OUT_CHUNK == 0 else o_ref.shape[1]
    for c in range(o_ref.shape[1] // tn):
        cs = slice(c * tn, (c + 1) * tn)
        o_ref[:, cs] += _dot(a, w2_ref[:, cs])

    if final:
        @pl.when(j == pl.num_programs(1) - 1)
        def _():
            o_ref[...] = _rms(o_ref[...], fg_ref[...])


def _ffn(x, g, w1, w3, w2, final_g, *, final):
    n, d = x.shape
    dff = w1.shape[1]
    tm = _pick(n, FFN_TM)
    tf = _pick(dff, FFN_TF)
    return pl.pallas_call(
        functools.partial(_ffn_kernel, final=final),
        name="ffn_final" if final else "ffn",
        grid=(n // tm, dff // tf),
        in_specs=[
            pl.BlockSpec((tm, d), lambda i, j: (i, 0)),
            pl.BlockSpec((1, d), lambda i, j: (0, 0)),
            pl.BlockSpec((d, tf), lambda i, j: (0, j)),
            pl.BlockSpec((d, tf), lambda i, j: (0, j)),
            pl.BlockSpec((tf, d), lambda i, j: (j, 0)),
            pl.BlockSpec((1, d), lambda i, j: (0, 0)),
        ],
        out_specs=pl.BlockSpec((tm, d), lambda i, j: (i, 0)),
        out_shape=jax.ShapeDtypeStruct((n, d), F32),
        scratch_shapes=[pltpu.VMEM((tm, d), BF16)],
        compiler_params=_cparams(("parallel", "arbitrary")),
    )(x, g.reshape(1, d), w1, w3, w2, final_g.reshape(1, d))


def _normproj_kernel(x_ref, g_ref, w_ref, s_ref, o_ref, h_ref, *, tn):
    h_ref[...] = _rms(x_ref[...], g_ref[...]).astype(BF16)
    for c in range(w_ref.shape[1] // tn):
        cs = slice(c * tn, (c + 1) * tn)
        o_ref[:, cs] = (_dot(h_ref[...], w_ref[:, cs]) * s_ref[:, cs]).astype(o_ref.dtype)


def _normproj(x, g, w, col_scale):
    n, d = x.shape
    nout = w.shape[1]
    tm = _pick(n, PROJ_TM)
    tn = _pick(nout, PROJ_TN)
    return pl.pallas_call(
        functools.partial(_normproj_kernel, tn=tn),
        name="normproj",
        grid=(n // tm,),
        in_specs=[
            pl.BlockSpec((tm, d), lambda i: (i, 0)),
            pl.BlockSpec((1, d), lambda i: (0, 0)),
            pl.BlockSpec((d, nout), lambda i: (0, 0), pipeline_mode=pl.Buffered(1)),
            pl.BlockSpec((1, nout), lambda i: (0, 0)),
        ],
        out_specs=pl.BlockSpec((tm, nout), lambda i: (i, 0)),
        out_shape=jax.ShapeDtypeStruct((n, nout), BF16),
        scratch_shapes=[pltpu.VMEM((tm, d), BF16)],
        compiler_params=_cparams(("parallel",)),
    )(x, g.reshape(1, d), w, col_scale.reshape(1, nout))


def _moba_prep_kernel(k_ref, v_ref, vtb_ref, kmean_ref):
    g = pl.program_id(1)
    bs = MOBA_BLOCK
    nblk = k_ref.shape[0] // bs
    orow = lax.broadcasted_iota(jnp.int32, (VT_ROWS - HEAD_DIM, bs), 0)
    ones_row = jnp.where(orow == 0, 1.0, 0.0).astype(BF16)
    for u in range(nblk):
        k = k_ref[u * bs:(u + 1) * bs, :]
        kmean_ref[0, pl.ds(g * nblk + u, 1), :] = jnp.mean(k.astype(F32), axis=0, keepdims=True)
        vt = v_ref[u * bs:(u + 1) * bs, :].astype(F32).T
        vtb_ref[0, u, 0:HEAD_DIM, :] = vt.astype(BF16)
        vtb_ref[0, u, HEAD_DIM:VT_ROWS, :] = ones_row


def _moba_prep(pa, batch, seq):
    nb = seq // MOBA_BLOCK
    grp = MOBA_QTILE
    ng = nb // grp
    bh = batch * MOBA_HEADS
    kcol = MOBA_W // HEAD_DIM
    vcol = 2 * MOBA_W // HEAD_DIM

    def kmap(n, g):
        return ((n // MOBA_HEADS) * ng + g, kcol + n % MOBA_HEADS)

    def vmap(n, g):
        return ((n // MOBA_HEADS) * ng + g, vcol + n % MOBA_HEADS)

    return pl.pallas_call(
        _moba_prep_kernel,
        name="moba_prep",
        grid=(bh, ng),
        in_specs=[
            pl.BlockSpec((grp * MOBA_BLOCK, HEAD_DIM), kmap),
            pl.BlockSpec((grp * MOBA_BLOCK, HEAD_DIM), vmap),
        ],
        out_specs=[
            pl.BlockSpec((1, grp, VT_ROWS, MOBA_BLOCK), lambda n, g: (n, g, 0, 0)),
            pl.BlockSpec((1, nb, HEAD_DIM), lambda n, g: (n, 0, 0)),
        ],
        out_shape=[
            jax.ShapeDtypeStruct((bh, nb, VT_ROWS, MOBA_BLOCK), BF16),
            jax.ShapeDtypeStruct((bh, nb, HEAD_DIM), F32),
        ],
        compiler_params=_cparams(("parallel", "arbitrary")),
    )(pa, pa)


def _moba_gate_kernel(q_ref, kmean_ref, qrow_ref, sel_ref, rank_ref, cnt_ref):
    t = pl.program_id(1)
    bs = MOBA_BLOCK
    tq = q_ref.shape[0]
    nb = kmean_ref.shape[1]

    q = q_ref[...]
    qrow_ref[...] = q.astype(F32)
    qt = q.astype(F32).T.astype(BF16)

    lane = lax.broadcasted_iota(jnp.int32, (1, tq), 1)
    own = t * (tq // bs) + lane // bs
    km = kmean_ref[0]
    km_hi = km.astype(BF16)
    km_lo = (km - km_hi.astype(F32)).astype(BF16)
    gate = _dot(km_hi, qt) + _dot(km_lo, qt)
    blk = lax.broadcasted_iota(jnp.int32, (nb, tq), 0)
    gate = jnp.where(blk < own, gate, NEG_INF)

    @pl.when(t == 0)
    def _():
        cnt_ref[...] = jnp.zeros_like(cnt_ref)

    run = cnt_ref[0][:, 0:1]
    qi = lax.broadcasted_iota(jnp.int32, (bs, bs), 0)
    qj = lax.broadcasted_iota(jnp.int32, (bs, bs), 1)
    before = jnp.where(qi < qj, 1.0, 0.0).astype(BF16)
    sels, ranks = [], []
    for r in range(MOBA_TOPK):
        mx = jnp.max(gate, axis=0, keepdims=True)
        first = jnp.min(jnp.where(gate == mx, blk, nb), axis=0, keepdims=True)
        hit = blk == first
        gate = jnp.where(hit, -jnp.inf, gate)
        valid = own > r
        oh = jnp.where(hit & valid, 1.0, 0.0)
        parts = []
        for c in range(tq // bs):
            ohc = oh[:, c * bs:(c + 1) * bs]
            prior = _dot(ohc.astype(BF16), before)
            parts.append(jnp.sum(ohc * (run + prior), axis=0, keepdims=True))
            run = run + jnp.sum(ohc, axis=1, keepdims=True)
        ranks.append(jnp.concatenate(parts, axis=1))
        sels.append(jnp.where(valid, first, -1))
    cnt_ref[0] = jnp.broadcast_to(run, cnt_ref.shape[1:])
    pad = jnp.zeros((8 - MOBA_TOPK, tq), jnp.int32)
    sel_ref[0] = jnp.concatenate(sels + [pad], axis=0)
    rank_ref[0] = jnp.concatenate([x.astype(jnp.int32) for x in ranks] + [pad], axis=0)


def _moba_gate(pa, kmean, batch, seq):
    nb = seq // MOBA_BLOCK
    tq = min(MOBA_QTILE * MOBA_BLOCK, seq)
    nt = seq // tq
    bh = batch * MOBA_HEADS

    def qmap(n, t):
        return ((n // MOBA_HEADS) * nt + t, n % MOBA_HEADS)

    return pl.pallas_call(
        _moba_gate_kernel,
        name="moba_gate",
        grid=(bh, nt),
        in_specs=[
            pl.BlockSpec((tq, HEAD_DIM), qmap),
            pl.BlockSpec((1, nb, HEAD_DIM), lambda n, t: (n, 0, 0)),
        ],
        out_specs=[
            pl.BlockSpec((tq, HEAD_DIM), lambda n, t: (n * nt + t, 0)),
            pl.BlockSpec((1, 8, tq), lambda n, t: (n, 0, t)),
            pl.BlockSpec((1, 8, tq), lambda n, t: (n, 0, t)),
            pl.BlockSpec((1, nb, HEAD_DIM), lambda n, t: (n, 0, 0)),
        ],
        out_shape=[
            jax.ShapeDtypeStruct((bh * seq, HEAD_DIM), F32),
            jax.ShapeDtypeStruct((bh, 8, seq), jnp.int32),
            jax.ShapeDtypeStruct((bh, 8, seq), jnp.int32),
            jax.ShapeDtypeStruct((bh, nb, HEAD_DIM), F32),
        ],
        compiler_params=_cparams(("parallel", "arbitrary")),
    )(pa, kmean)


def _moba_slot_kernel(sel_ref, rank_ref, base_ref, slot_ref, *, trash):
    nb = base_ref.shape[1]
    tq = sel_ref.shape[2]
    base = base_ref[0][:, 0:1]
    blk = lax.broadcasted_iota(jnp.int32, (nb, tq), 0)
    spare = trash + lax.broadcasted_iota(jnp.int32, (1, tq), 1) % SLOT_TILE
    rows = []
    for r in range(MOBA_TOPK):
        sel = sel_ref[0, r:r + 1, :]
        start = jnp.sum(jnp.where(blk == sel, base, 0.0), axis=0, keepdims=True)
        slot = start.astype(jnp.int32) + rank_ref[0, r:r + 1, :]
        rows.append(jnp.where(sel >= 0, slot, spare))
    rows.append(jnp.broadcast_to(spare, (8 - MOBA_TOPK, tq)))
    slot_ref[0] = jnp.concatenate(rows, axis=0)


def _moba_slots(sel, rank, base, trash):
    bh, _, seq = sel.shape
    nb = base.shape[1]
    tq = min(SLOTS_TQ, seq)
    spec = pl.BlockSpec((1, 8, tq), lambda n, t: (n, 0, t))
    return pl.pallas_call(
        functools.partial(_moba_slot_kernel, trash=trash),
        name="moba_slots",
        grid=(bh, seq // tq),
        in_specs=[spec, spec, pl.BlockSpec((1, nb, HEAD_DIM), lambda n, t: (n, 0, 0))],
        out_specs=spec,
        out_shape=jax.ShapeDtypeStruct((bh, 8, seq), jnp.int32),
        compiler_params=_cparams(("parallel", "parallel")),
    )(sel, rank, base)


def _sc_mesh():
    return plsc.VectorSubcoreMesh(core_axis_name="core", subcore_axis_name="subcore")


def _sc_scatter_rows(rows, slots, n_out):
    bh8, seq = slots.shape
    bh = bh8 // 8
    nw = seq // SC_WINDOW

    @pl.kernel(out_type=jax.ShapeDtypeStruct((n_out, HEAD_DIM), rows.dtype), mesh=_sc_mesh(),
               scratch_types=[])
    def scatter(x_hbm, i_hbm, o_hbm):
        def body(x_vmem, i_vmem):
            pltpu.sync_copy(x_vmem, o_hbm.at[i_vmem.at[0]])

        pltpu.emit_pipeline(
            body,
            grid=(bh * MOBA_TOPK * nw,),
            in_specs=[
                pl.BlockSpec((SC_WINDOW, HEAD_DIM),
                             index_map=lambda i: ((i // (MOBA_TOPK * nw)) * nw + i % nw, 0)),
                pl.BlockSpec((1, SC_WINDOW),
                             index_map=lambda i: ((i // (MOBA_TOPK * nw)) * 8 + (i // nw) % MOBA_TOPK,
                                                  i % nw)),
            ],
            out_specs=[],
            core_axis_name=("core", "subcore"),
            dimension_semantics=(pltpu.PARALLEL,),
        )(x_hbm, i_hbm)

    return scatter(rows, slots)


def _sc_gather_rows(table, slots):
    bh8, seq = slots.shape
    bh = bh8 // 8
    nw = seq // SC_WINDOW
    n_out = bh * MOBA_TOPK * seq

    @pl.kernel(out_type=jax.ShapeDtypeStruct((n_out, HEAD_DIM), table.dtype), mesh=_sc_mesh())
    def gather(x_hbm, i_hbm, o_hbm):
        def body(i_vmem, o_vmem):
            pltpu.sync_copy(x_hbm.at[i_vmem.at[0]], o_vmem)

        pltpu.emit_pipeline(
            body,
            grid=(bh * MOBA_TOPK * nw,),
            in_specs=[
                pl.BlockSpec((1, SC_WINDOW),
                             index_map=lambda i: ((i // (MOBA_TOPK * nw)) * 8 + (i // nw) % MOBA_TOPK,
                                                  i % nw)),
            ],
            out_specs=[pl.BlockSpec((SC_WINDOW, HEAD_DIM), index_map=lambda i: (i, 0))],
            core_axis_name=("core", "subcore"),
            dimension_semantics=(pltpu.PARALLEL,),
        )(i_hbm, o_hbm)

    return gather(table, slots)


def _moba_routed_kernel(tb_ref, nt_ref, qs_ref, k_ref, vtb_ref, slope_ref, o_ref,
                        sa_ref, sb_ref, sc_ref):
    n = pl.program_id(0)
    g = pl.program_id(1)
    bs = MOBA_BLOCK
    tpg = qs_ref.shape[0] // SLOT_TILE
    tiles_cap = pl.num_programs(1) * tpg

    @pl.when(g * tpg < nt_ref[n])
    def _():
        krow = lax.broadcasted_iota(jnp.int32, (bs, SLOT_TILE), 0).astype(F32)
        bias = slope_ref[0, 0:1, 0:1] * krow
        frow = lax.broadcasted_iota(jnp.int32, (HEAD_DIM, SLOT_TILE), 0)
        blocks = [tb_ref[n * tiles_cap + g * tpg + u] for u in range(tpg)]

        def scores(u):
            qt = qs_ref[u * SLOT_TILE:(u + 1) * SLOT_TILE, :].T.astype(BF16)
            k = k_ref[pl.ds(pl.multiple_of(blocks[u] * bs, bs), bs), :]
            return _dot(k, qt) + bias

        bufs = (sa_ref, sb_ref, sc_ref)
        sa_ref[...] = scores(0)
        sb_ref[...] = scores(1)
        for u in range(tpg):
            if u + 2 < tpg:
                bufs[(u + 2) % 3][...] = scores(u + 2)
            s = bufs[u % 3][...]
            m = jnp.max(s, axis=0, keepdims=True)
            p = jnp.exp(s - m).astype(BF16)
            acc = _dot(vtb_ref[0, blocks[u]], p)
            l = acc[HEAD_DIM:HEAD_DIM + 1]
            o = (acc[0:HEAD_DIM] / l).astype(BF16).astype(F32)
            lse = m + jnp.log(l)
            ob = lax.bitcast_convert_type(o, jnp.uint32)
            lb = lax.bitcast_convert_type(lse, jnp.uint32)
            extra = jnp.where(frow == 0, lb >> 16, jnp.where(frow == 1, lb & 0xFFFF, 0))
            packed = lax.bitcast_convert_type(ob | extra, F32)
            o_ref[u * SLOT_TILE:(u + 1) * SLOT_TILE, :] = packed.T


def _moba_routed(qs, pa, vtb, slopes, tile_block, n_tiles, tiles_cap):
    bh, nb = vtb.shape[0], vtb.shape[1]
    seq = nb * MOBA_BLOCK
    kcol = MOBA_W // HEAD_DIM
    tpg = ROUTED_TILES_PER_STEP
    steps = tiles_cap // tpg
    rows = tpg * SLOT_TILE

    def qmap(n, g, tb, nt):
        used = jnp.maximum((nt[n] + tpg - 1) // tpg, 1)
        return (n * steps + jnp.minimum(g, used - 1), 0)

    grid_spec = pltpu.PrefetchScalarGridSpec(
        num_scalar_prefetch=2,
        grid=(bh, steps),
        in_specs=[
            pl.BlockSpec((rows, HEAD_DIM), qmap),
            pl.BlockSpec((seq, HEAD_DIM),
                         lambda n, g, tb, nt: (n // MOBA_HEADS, kcol + n % MOBA_HEADS)),
            pl.BlockSpec((1, nb, VT_ROWS, MOBA_BLOCK), lambda n, g, tb, nt: (n, 0, 0, 0)),
            pl.BlockSpec((1, 1, HEAD_DIM), lambda n, g, tb, nt: (n, 0, 0)),
        ],
        out_specs=pl.BlockSpec((rows, HEAD_DIM), qmap),
        scratch_shapes=[pltpu.VMEM((MOBA_BLOCK, SLOT_TILE), F32)] * 3,
    )
    return pl.pallas_call(
        _moba_routed_kernel,
        name="moba_routed",
        grid_spec=grid_spec,
        out_shape=jax.ShapeDtypeStruct((qs.shape[0], HEAD_DIM), F32),
        compiler_params=_cparams(("parallel", "arbitrary")),
    )(tile_block, n_tiles, qs, pa, vtb, slopes)


def _moba_merge_kernel(q_ref, k_ref, vtb_ref, og_ref, sel_ref, slope_ref, o_ref,
                       sa_ref, sb_ref, sc_ref):
    t = pl.program_id(1)
    bs = MOBA_BLOCK
    tq = q_ref.shape[0]
    slope = slope_ref[0, 0:1, 0:1]

    qt = q_ref[...].astype(F32).T.astype(BF16)
    lane = lax.broadcasted_iota(jnp.int32, (1, tq), 1)
    tpos = (t * tq + lane).astype(F32)
    lk = lax.broadcasted_iota(jnp.int32, (bs, bs), 0)
    lq = lax.broadcasted_iota(jnp.int32, (bs, bs), 1)
    dist = (lq - lk).astype(F32)
    nq = tq // bs

    def scores(u):
        return _dot(k_ref[u * bs:(u + 1) * bs, :], qt[:, u * bs:(u + 1) * bs])

    sbufs = (sa_ref, sb_ref, sc_ref)
    for u in range(min(2, nq)):
        sbufs[u][...] = scores(u)
    accs, ms = [], []
    for u in range(nq):
        if u + 2 < nq:
            sbufs[(u + 2) % 3][...] = scores(u + 2)
        s = sbufs[u % 3][...]
        s = jnp.where(dist >= 0.0, s - slope * dist, NEG_INF)
        m_u = jnp.max(s, axis=0, keepdims=True)
        p = jnp.exp(s - m_u).astype(BF16)
        accs.append(_dot(vtb_ref[0, u], p))
        ms.append(m_u)
    acc = jnp.concatenate(accs, axis=1)
    l = acc[HEAD_DIM:HEAD_DIM + 1]
    parts = [acc[0:HEAD_DIM] / l]
    lses = [jnp.concatenate(ms, axis=1) + jnp.log(l)]

    for r in range(MOBA_TOPK):
        u = lax.bitcast_convert_type(og_ref[0, r, 0].T, jnp.uint32)
        lb = ((u[0:1] & 0xFFFF) << 16) | (u[1:2] & 0xFFFF)
        lse_r = lax.bitcast_convert_type(lb, F32)
        sel_r = sel_ref[0, r:r + 1, :]
        valid = sel_r >= 0
        lses.append(jnp.where(valid, lse_r + slope * ((sel_r * bs).astype(F32) - tpos), NEG_INF))
        hi_half = lax.bitcast_convert_type((u >> 16) << 16, F32)
        parts.append(jnp.where(valid, hi_half, 0.0))
    mx = functools.reduce(jnp.maximum, lses)
    ws = [jnp.exp(x - mx) for x in lses]
    num = functools.reduce(lambda a, b: a + b, [w * o for w, o in zip(ws, parts)])
    out_t = num / functools.reduce(lambda a, b: a + b, ws)
    o_ref[...] = out_t.T.astype(o_ref.dtype)


def _moba_merge(pa, vtb, og, sel, slopes, batch, seq):
    tq = MOBA_QTILE * MOBA_BLOCK
    nt = seq // tq
    bh = batch * MOBA_HEADS
    kcol = MOBA_W // HEAD_DIM
    og = og.reshape(bh, MOBA_TOPK, nt, tq, HEAD_DIM)

    def qmap(n, t):
        return ((n // MOBA_HEADS) * nt + t, n % MOBA_HEADS)

    def kmap(n, t):
        return ((n // MOBA_HEADS) * nt + t, kcol + n % MOBA_HEADS)

    return pl.pallas_call(
        _moba_merge_kernel,
        name="moba_merge",
        grid=(bh, nt),
        in_specs=[
            pl.BlockSpec((tq, HEAD_DIM), qmap),
            pl.BlockSpec((tq, HEAD_DIM), kmap),
            pl.BlockSpec((1, MOBA_QTILE, VT_ROWS, MOBA_BLOCK), lambda n, t: (n, t, 0, 0)),
            pl.BlockSpec((1, MOBA_TOPK, 1, tq, HEAD_DIM), lambda n, t: (n, 0, t, 0, 0)),
            pl.BlockSpec((1, 8, tq), lambda n, t: (n, 0, t)),
            pl.BlockSpec((1, 1, HEAD_DIM), lambda n, t: (n, 0, 0)),
        ],
        out_specs=pl.BlockSpec((tq, HEAD_DIM), qmap),
        out_shape=jax.ShapeDtypeStruct((batch * seq, MOBA_W), BF16),
        scratch_shapes=[pltpu.VMEM((MOBA_BLOCK, MOBA_BLOCK), F32)] * 3,
        compiler_params=_cparams(("parallel", "parallel")),
    )(pa, pa, vtb, og, sel, slopes)


def _moba_routed_attention(pa, vtb, kmean, slopes, batch, seq, companion):
    nb = seq // MOBA_BLOCK
    bh = batch * MOBA_HEADS
    qrows, sel, rank, cnt = _moba_gate(pa, kmean, batch, seq)

    step_slots = SLOT_TILE * ROUTED_TILES_PER_STEP
    cap = -(-(MOBA_TOPK * seq + nb * SLOT_TILE) // step_slots) * step_slots
    tiles_cap = cap // SLOT_TILE
    counts = cnt[:, :, 0].astype(jnp.int32)
    padded = (counts + SLOT_TILE - 1) // SLOT_TILE * SLOT_TILE
    ends = jnp.cumsum(padded, axis=1)
    base = ends - padded + (jnp.arange(bh, dtype=jnp.int32) * cap)[:, None]
    n_tiles = ends[:, -1] // SLOT_TILE
    tile_idx = jnp.arange(tiles_cap, dtype=jnp.int32)
    tile_block = jnp.sum(ends[:, None, :] // SLOT_TILE <= tile_idx[None, :, None], axis=-1)
    tile_block = jnp.minimum(tile_block, nb - 1).astype(jnp.int32).reshape(bh * tiles_cap)
    base_b = jnp.broadcast_to(base.astype(F32)[:, :, None], (bh, nb, HEAD_DIM))

    n_rows = bh * cap + SLOT_TILE
    slots = _moba_slots(sel, rank, base_b, bh * cap).reshape(bh * 8, seq)
    qs = _sc_scatter_rows(qrows, slots, n_rows)
    qs, companion = lax.optimization_barrier((qs, companion))
    part = _moba_routed(qs, pa, vtb, slopes, tile_block, n_tiles, tiles_cap)
    og = _sc_gather_rows(part, slots)
    return _moba_merge(pa, vtb, og, sel, slopes, batch, seq), companion


def _split3(x):
    hi = x.astype(BF16)
    r1 = x - hi.astype(F32)
    mid = r1.astype(BF16)
    lo = (r1 - mid.astype(F32)).astype(BF16)
    return hi, mid, lo


def _hgrn_kernel(qb_ref, fb_ref, ib_ref, gb_ref, lbl_ref, gn_ref, o_ref, s_ref, oi_ref, st_ref,
                 *, layer):
    tt = qb_ref.shape[0]
    sub = HGRN_SUB
    hd = HEAD_DIM

    @pl.when(pl.program_id(1) == 0)
    def _():
        s_ref[...] = jnp.zeros_like(s_ref)

    logits = lbl_ref[...]
    e = jnp.exp(logits - jnp.max(logits, axis=0, keepdims=True))
    pl_ = e / jnp.sum(e, axis=0, keepdims=True)
    lb = jnp.sum(pl_[0:layer + 1], axis=0, keepdims=True) - pl_[0:1]

    fb = fb_ref[...].astype(F32)
    t = jnp.exp(-jnp.abs(fb))
    r = 1.0 / (1.0 + t)
    sig_pos = jnp.where(fb >= 0, r, t * r)
    sig_neg = jnp.where(fb >= 0, t * r, r)
    f_gate = lb + (1.0 - lb) * sig_pos
    logf = jnp.log(jnp.maximum(f_gate, F_MIN))
    k_all = (1.0 - lb) * sig_neg
    qb = qb_ref[...].astype(F32)
    q_all = qb * jax.nn.sigmoid(qb)
    v_all = ib_ref[...].astype(F32)

    ri = lax.broadcasted_iota(jnp.int32, (tt, tt), 0)
    ci = lax.broadcasted_iota(jnp.int32, (tt, tt), 1)
    same = (ri // sub) == (ci // sub)
    tri = jnp.where(same & (ci <= ri), 1.0, 0.0).astype(BF16)
    hi, mid, lo = _split3(logf)
    gl = (_dot(tri, hi) + _dot(tri, mid) + _dot(tri, lo)) * LOG2E

    half = sub // 2
    nsub = tt // sub
    nv = tt // half
    lane_sum = jnp.ones((hd, hd), BF16)
    row_in = lax.broadcasted_iota(jnp.int32, (nv, half, hd), 1)

    def pair_terms(qx, gx, kx, vx, causal):
        n = qx.shape[0]
        out = jnp.zeros_like(qx)
        for rho in range(half):
            kr = pltpu.roll(kx, rho, 1) if rho else kx
            vr = pltpu.roll(vx, rho, 1) if rho else vx
            gr = pltpu.roll(gx[1], rho, 1) if rho else gx[1]
            prod = qx * kr * jnp.exp2(gx[0] - gr)
            if causal and rho:
                prod = jnp.where(row_in >= rho, prod, 0.0)
            a = _dot(prod.reshape(n * half, hd).astype(BF16), lane_sum)
            out = out + a.reshape(n, half, hd) * vr
        return out

    for h in range(HGRN_HEADS):
        cs = slice(h * hd, (h + 1) * hd)
        q = q_all[:, cs]
        k = k_all[:, cs]
        v = v_all[:, cs]
        g = gl[:, cs]

        q3, k3, v3, g3 = (a.reshape(nv, half, hd) for a in (q, k, v, g))
        od = pair_terms(q3, (g3, g3), k3, v3, True).reshape(nsub, 2, half, hd)
        q4, k4, v4, g4 = (a.reshape(nsub, 2, half, hd) for a in (q, k, v, g))
        oh = pair_terms(q4[:, 1], (g4[:, 1], g4[:, 0]), k4[:, 0], v4[:, 0], False)
        o_diag = jnp.stack([od[:, 0], od[:, 1] + oh], axis=1).reshape(tt, hd)

        gs = g.reshape(nsub, sub, hd)
        g_end = gs[:, sub - 1:sub, :]
        qd = (q * jnp.exp2(g)).astype(BF16).reshape(nsub, sub, hd)
        kd = (k.reshape(nsub, sub, hd) * jnp.exp2(g_end - gs)).astype(BF16)
        vb = v.astype(BF16).reshape(nsub, sub, hd)
        dec = jnp.exp2(g_end)
        upd = [_dot_t0(vb[c], kd[c]) for c in range(nsub)]
        st = s_ref[h]
        for c in range(nsub):
            st_ref[c] = st.astype(BF16)
            st = st * dec[c] + upd[c]
        s_ref[h] = st
        for c in range(nsub):
            oi_ref[c * sub:(c + 1) * sub, cs] = lax.dot_general(
                qd[c], st_ref[c], (((1,), (1,)), ((), ())), preferred_element_type=F32)

        o = oi_ref[:, cs] + o_diag
        rr = lax.rsqrt(jnp.mean(o * o, axis=-1, keepdims=True) + RMS_EPS)
        gate = jax.nn.sigmoid(gb_ref[:, cs].astype(F32))
        o_ref[:, cs] = (o * rr * gn_ref[...] * gate).astype(o_ref.dtype)


def _hgrn(pa, lb_logits, out_norm, layer, batch, seq):
    tt = min(HGRN_TT, seq)
    nt = seq // tt
    c0 = 3 * MOBA_W // HGRN_W
    nl = lb_logits.shape[0]

    def cmap(off):
        return lambda b, t: (b * nt + t, c0 + off)

    return pl.pallas_call(
        functools.partial(_hgrn_kernel, layer=layer),
        name="hgrn",
        grid=(batch, nt),
        in_specs=[
            pl.BlockSpec((tt, HGRN_W), cmap(0)),
            pl.BlockSpec((tt, HGRN_W), cmap(1)),
            pl.BlockSpec((tt, HGRN_W), cmap(2)),
            pl.BlockSpec((tt, HGRN_W), cmap(3)),
            pl.BlockSpec((nl, HGRN_W), lambda b, t: (0, 0)),
            pl.BlockSpec((1, HEAD_DIM), lambda b, t: (0, 0)),
        ],
        out_specs=pl.BlockSpec((tt, HGRN_W), lambda b, t: (b * nt + t, 0)),
        out_shape=jax.ShapeDtypeStruct((batch * seq, HGRN_W), BF16),
        scratch_shapes=[pltpu.VMEM((HGRN_HEADS, HEAD_DIM, HEAD_DIM), F32),
                        pltpu.VMEM((tt, HGRN_W), F32),
                        pltpu.VMEM((tt // HGRN_SUB, HEAD_DIM, HEAD_DIM), BF16)],
        compiler_params=_cparams(("parallel", "arbitrary")),
    )(pa, pa, pa, pa, lb_logits, out_norm.reshape(1, HEAD_DIM))


def _memkv_kernel(mem_ref, g_ref, w_ref, kt_ref, v_ref):
    h = _rms(mem_ref[0], g_ref[...]).astype(BF16)
    kv = _dot(h, w_ref[...])
    kt_ref[0] = kv[:, 0:MEM_W].T.astype(BF16)
    v_ref[0] = kv[:, MEM_W:].astype(BF16)


def _memkv(mem, g, w):
    b, m, d = mem.shape
    return pl.pallas_call(
        _memkv_kernel,
        name="memkv",
        grid=(b,),
        in_specs=[
            pl.BlockSpec((1, m, d), lambda i: (i, 0, 0)),
            pl.BlockSpec((1, d), lambda i: (0, 0)),
            pl.BlockSpec((d, 2 * MEM_W), lambda i: (0, 0)),
        ],
        out_specs=[
            pl.BlockSpec((1, MEM_W, m), lambda i: (i, 0, 0)),
            pl.BlockSpec((1, m, MEM_W), lambda i: (i, 0, 0)),
        ],
        out_shape=[
            jax.ShapeDtypeStruct((b, MEM_W, m), BF16),
            jax.ShapeDtypeStruct((b, m, MEM_W), BF16),
        ],
        compiler_params=_cparams(("parallel",)),
    )(mem, g.reshape(1, d), w)


def _merge_kernel(x_ref, oa_ref, ob_ref, qm_ref, g0_ref, g1_ref, g2_ref, mkt_ref, mv_ref,
                  wa_ref, wb_ref, wm_ref, wo_ref, o_ref, om_ref):
    scale = HEAD_DIM ** -0.5
    for h in range(MEM_HEADS):
        cs = slice(h * HEAD_DIM, (h + 1) * HEAD_DIM)
        s = _dot(qm_ref[:, cs], mkt_ref[0, cs, :]) * scale
        p = jnp.exp(s - jnp.max(s, axis=-1, keepdims=True))
        l = jnp.sum(p, axis=-1, keepdims=True)
        om_ref[:, cs] = (_dot(p.astype(BF16), mv_ref[0, :, cs]) / l).astype(BF16)

    y = jax.nn.sigmoid(g0_ref[...].astype(F32)) * _dot(oa_ref[...], wa_ref[...])
    y += jax.nn.sigmoid(g1_ref[...].astype(F32)) * _dot(ob_ref[...], wb_ref[...])
    y += jax.nn.sigmoid(g2_ref[...].astype(F32)) * _dot(om_ref[...], wm_ref[...])
    o_ref[...] = x_ref[...] + _dot(y.astype(BF16), wo_ref[...])


def _merge(x, oa, ob, pa, gates, mkt, mv, wa, wb, wm, wo, batch, seq):
    n, d = x.shape
    m = mv.shape[1]
    tm = _pick(seq, MERGE_TM)
    per_b = seq // tm
    qcol = (3 * MOBA_W + 4 * HGRN_W) // MEM_W
    row = lambda i: (i, 0)
    return pl.pallas_call(
        _merge_kernel,
        name="merge",
        grid=(n // tm,),
        in_specs=[
            pl.BlockSpec((tm, d), row),
            pl.BlockSpec((tm, MOBA_W), row),
            pl.BlockSpec((tm, HGRN_W), row),
            pl.BlockSpec((tm, MEM_W), lambda i: (i, qcol)),
            pl.BlockSpec((tm, d), lambda i: (i, 0)),
            pl.BlockSpec((tm, d), lambda i: (i, 1)),
            pl.BlockSpec((tm, d), lambda i: (i, 2)),
            pl.BlockSpec((1, MEM_W, m), lambda i: (i // per_b, 0, 0)),
            pl.BlockSpec((1, m, MEM_W), lambda i: (i // per_b, 0, 0)),
            _const_spec((MOBA_W, d)),
            _const_spec((HGRN_W, d)),
            _const_spec((MEM_W, d)),
            _const_spec((d, d)),
        ],
        out_specs=pl.BlockSpec((tm, d), row),
        out_shape=jax.ShapeDtypeStruct((n, d), F32),
        scratch_shapes=[pltpu.VMEM((tm, MEM_W), BF16)],
        compiler_params=_cparams(("parallel",)),
    )(x, oa, ob, pa, gates, gates, gates, mkt, mv, wa, wb, wm, wo)


def kernel(x, mem, ffn1_norm, ffn1_w1, ffn1_w3, ffn1_w2, mix_norm, w_in, hgrn_lb_logits,
           hgrn_out_norm, mem_norm, w_mem_kv, w_proj_moba, w_proj_hgrn, w_proj_mem, w_out,
           ffn2_norm, ffn2_w1, ffn2_w3, ffn2_w2, final_norm):
    batch, seq, d = x.shape
    depth = ffn1_w1.shape[0]
    assert seq % (MOBA_QTILE * MOBA_BLOCK) == 0
    assert w_in.shape[-1] == MIX_W + 3 * d
    hs = jnp.arange(1, MOBA_HEADS + 1, dtype=F32)
    slopes = jnp.tile(jnp.exp2(-8.0 * hs / MOBA_HEADS), batch)
    slopes = jnp.broadcast_to(slopes[:, None, None], (batch * MOBA_HEADS, 1, HEAD_DIM))
    mix_scale = jnp.concatenate([jnp.full((MOBA_W,), HEAD_DIM ** -0.5, F32),
                                 jnp.ones((MIX_W - MOBA_W,), F32)])
    gate_scale = jnp.ones((3 * d,), F32)

    xs = x.reshape(batch * seq, d)
    for l in range(depth):
        last = l == depth - 1
        bf = functools.partial(_cast_layer, layer=l)
        xs = _ffn(xs, ffn1_norm[l], bf(ffn1_w1), bf(ffn1_w3), bf(ffn1_w2), final_norm, final=False)

        w_mix, w_gates = bf(w_in, splits=(MIX_W, 3 * d))
        pa = _normproj(xs, mix_norm[l], w_mix, mix_scale)
        gates = _normproj(xs, mix_norm[l], w_gates, gate_scale)

        vtb, kmean = _moba_prep(pa, batch, seq)
        ob = _hgrn(pa, hgrn_lb_logits, hgrn_out_norm[l], l, batch, seq)
        oa, ob = _moba_routed_attention(pa, vtb, kmean, slopes, batch, seq, ob)
        mkt, mv = _memkv(mem, mem_norm[l], bf(w_mem_kv))
        xs = _merge(xs, oa, ob, pa, gates, mkt, mv, bf(w_proj_moba), bf(w_proj_hgrn),
                    bf(w_proj_mem), bf(w_out), batch, seq)

        xs = _ffn(xs, ffn2_norm[l], bf(ffn2_w1), bf(ffn2_w3), bf(ffn2_w2), final_norm, final=last)
    return xs.reshape(batch, seq, d)
```

```python
import functools

import jax
import jax.numpy as jnp
from jax import lax
from jax.experimental import pallas as pl
from jax.experimental.pallas import tpu as pltpu
from jax.experimental.pallas import tpu_sc as plsc

F32 = jnp.float32
BF16 = jnp.bfloat16

HEAD_DIM = 128
MOBA_HEADS = 8
MOBA_BLOCK = 256
MOBA_TOPK = 3
HGRN_HEADS = 4
MEM_HEADS = 4
RMS_EPS = 1e-6
NEG_INF = -1e30
F_MIN = 1e-20
LOG2E = 1.4426950408889634

MOBA_W = MOBA_HEADS * HEAD_DIM
HGRN_W = HGRN_HEADS * HEAD_DIM
MEM_W = MEM_HEADS * HEAD_DIM
MIX_W = 3 * MOBA_W + 4 * HGRN_W + MEM_W

VMEM_LIMIT_BYTES = 60 * 1024 * 1024

LANE = 128
BF16_ROWS = 16
VT_ROWS = HEAD_DIM + BF16_ROWS
MOBA_QTILE = 4

HGRN_SUB = 16
HGRN_TT = 256
FFN_TM = 1024
FFN_TF = 512
FFN_OUT_CHUNK = 512
PROJ_TM = 512
PROJ_TN = 512
MERGE_TM = 256
SLOTS_TQ = 2048
CAST_BLOCK_BYTES = 8 * 1024 * 1024
SLOT_TILE = 256
ROUTED_TILES_PER_STEP = 16
SC_WINDOW = 128


def _cparams(sem):
    return pltpu.CompilerParams(dimension_semantics=sem, vmem_limit_bytes=VMEM_LIMIT_BYTES)


def _dot(a, b):
    return jnp.dot(a, b, preferred_element_type=F32)


def _dot_t0(a, b):
    return lax.dot_general(a, b, (((0,), (0,)), ((), ())), preferred_element_type=F32)


def _rms(xf, g):
    r = lax.rsqrt(jnp.mean(xf * xf, axis=-1, keepdims=True) + RMS_EPS)
    return xf * r * g


def _pick(n, want):
    if n <= want:
        return n
    t = (want // LANE) * LANE
    while t >= LANE:
        if n % t == 0:
            return t
        t -= LANE
    return n


def _const_spec(shape):
    nd = len(shape)
    return pl.BlockSpec(shape, lambda *_: (0,) * nd)


def _cast_kernel(w_ref, *o_refs, splits):
    lo = 0
    for o_ref, width in zip(o_refs, splits):
        o_ref[...] = w_ref[0, :, lo:lo + width].astype(BF16)
        lo += width


def _cast_layer(w_stack, layer, splits=None):
    _, rows, cols = w_stack.shape
    splits = (cols,) if splits is None else tuple(splits)
    assert sum(splits) == cols
    tr = rows
    while tr * cols * 4 > CAST_BLOCK_BYTES and tr % (2 * BF16_ROWS) == 0:
        tr //= 2
    outs = pl.pallas_call(
        functools.partial(_cast_kernel, splits=splits),
        name="cast",
        grid=(rows // tr,),
        in_specs=[pl.BlockSpec((1, tr, cols), lambda i: (layer, i, 0))],
        out_specs=[pl.BlockSpec((tr, w), lambda i: (i, 0)) for w in splits],
        out_shape=[jax.ShapeDtypeStruct((rows, w), BF16) for w in splits],
        compiler_params=_cparams(("parallel",)),
    )(w_stack)
    return outs[0] if len(splits) == 1 else outs


def _ffn_kernel(x_ref, g_ref, w1_ref, w3_ref, w2_ref, fg_ref, o_ref, h_ref, *, final):
    j = pl.program_id(1)

    @pl.when(j == 0)
    def _():
        xf = x_ref[...]
        h_ref[...] = _rms(xf, g_ref[...]).astype(BF16)
        o_ref[...] = xf

    h = h_ref[...]
    u = _dot(h, w1_ref[...])
    v = _dot(h, w3_ref[...])
    a = (0.5 * u * jax.nn.sigmoid(u) * v).astype(BF16)
    tn = FFN_OUT_CHUNK if o_ref.shape[1] % FFN_OUT_CHUNK == 0 else o_ref.shape[1]
    for c in range(o_ref.shape[1] // tn):
        cs = slice(c * tn, (c + 1) * tn)
        o_ref[:, cs] += _dot(a, w2_ref[:, cs])

    if final:
        @pl.when(j == pl.num_programs(1) - 1)
        def _():
            o_ref[...] = _rms(o_ref[...], fg_ref[...])


def _ffn(x, g, w1, w3, w2, final_g, *, final):
    n, d = x.shape
    dff = w1.shape[1]
    tm = _pick(n, FFN_TM)
    tf = _pick(dff, FFN_TF)
    return pl.pallas_call(
        functools.partial(_ffn_kernel, final=final),
        name="ffn_final" if final else "ffn",
        grid=(n // tm, dff // tf),
        in_specs=[
            pl.BlockSpec((tm, d), lambda i, j: (i, 0)),
            pl.BlockSpec((1, d), lambda i, j: (0, 0)),
            pl.BlockSpec((d, tf), lambda i, j: (0, j)),
            pl.BlockSpec((d, tf), lambda i, j: (0, j)),
            pl.BlockSpec((tf, d), lambda i, j: (j, 0)),
            pl.BlockSpec((1, d), lambda i, j: (0, 0)),
        ],
        out_specs=pl.BlockSpec((tm, d), lambda i, j: (i, 0)),
        out_shape=jax.ShapeDtypeStruct((n, d), F32),
        scratch_shapes=[pltpu.VMEM((tm, d), BF16)],
        compiler_params=_cparams(("parallel", "arbitrary")),
    )(x, g.reshape(1, d), w1, w3, w2, final_g.reshape(1, d))


def _normproj_kernel(x_ref, g_ref, w_ref, s_ref, o_ref, h_ref, *, tn):
    h_ref[...] = _rms(x_ref[...], g_ref[...]).astype(BF16)
    for c in range(w_ref.shape[1] // tn):
        cs = slice(c * tn, (c + 1) * tn)
        o_ref[:, cs] = (_dot(h_ref[...], w_ref[:, cs]) * s_ref[:, cs]).astype(o_ref.dtype)


def _normproj(x, g, w, col_scale):
    n, d = x.shape
    nout = w.shape[1]
    tm = _pick(n, PROJ_TM)
    tn = _pick(nout, PROJ_TN)
    return pl.pallas_call(
        functools.partial(_normproj_kernel, tn=tn),
        name="normproj",
        grid=(n // tm,),
        in_specs=[
            pl.BlockSpec((tm, d), lambda i: (i, 0)),
            pl.BlockSpec((1, d), lambda i: (0, 0)),
            pl.BlockSpec((d, nout), lambda i: (0, 0), pipeline_mode=pl.Buffered(1)),
            pl.BlockSpec((1, nout), lambda i: (0, 0)),
        ],
        out_specs=pl.BlockSpec((tm, nout), lambda i: (i, 0)),
        out_shape=jax.ShapeDtypeStruct((n, nout), BF16),
        scratch_shapes=[pltpu.VMEM((tm, d), BF16)],
        compiler_params=_cparams(("parallel",)),
    )(x, g.reshape(1, d), w, col_scale.reshape(1, nout))


def _moba_prep_kernel(k_ref, v_ref, vtb_ref, kmean_ref):
    g = pl.program_id(1)
    bs = MOBA_BLOCK
    nblk = k_ref.shape[0] // bs
    orow = lax.broadcasted_iota(jnp.int32, (VT_ROWS - HEAD_DIM, bs), 0)
    ones_row = jnp.where(orow == 0, 1.0, 0.0).astype(BF16)
    for u in range(nblk):
        k = k_ref[u * bs:(u + 1) * bs, :]
        kmean_ref[0, pl.ds(g * nblk + u, 1), :] = jnp.mean(k.astype(F32), axis=0, keepdims=True)
        vt = v_ref[u * bs:(u + 1) * bs, :].astype(F32).T
        vtb_ref[0, u, 0:HEAD_DIM, :] = vt.astype(BF16)
        vtb_ref[0, u, HEAD_DIM:VT_ROWS, :] = ones_row


def _moba_prep(pa, batch, seq):
    nb = seq // MOBA_BLOCK
    grp = MOBA_QTILE
    ng = nb // grp
    bh = batch * MOBA_HEADS
    kcol = MOBA_W // HEAD_DIM
    vcol = 2 * MOBA_W // HEAD_DIM

    def kmap(n, g):
        return ((n // MOBA_HEADS) * ng + g, kcol + n % MOBA_HEADS)

    def vmap(n, g):
        return ((n // MOBA_HEADS) * ng + g, vcol + n % MOBA_HEADS)

    return pl.pallas_call(
        _moba_prep_kernel,
        name="moba_prep",
        grid=(bh, ng),
        in_specs=[
            pl.BlockSpec((grp * MOBA_BLOCK, HEAD_DIM), kmap),
            pl.BlockSpec((grp * MOBA_BLOCK, HEAD_DIM), vmap),
        ],
        out_specs=[
            pl.BlockSpec((1, grp, VT_ROWS, MOBA_BLOCK), lambda n, g: (n, g, 0, 0)),
            pl.BlockSpec((1, nb, HEAD_DIM), lambda n, g: (n, 0, 0)),
        ],
        out_shape=[
            jax.ShapeDtypeStruct((bh, nb, VT_ROWS, MOBA_BLOCK), BF16),
            jax.ShapeDtypeStruct((bh, nb, HEAD_DIM), F32),
        ],
        compiler_params=_cparams(("parallel", "arbitrary")),
    )(pa, pa)


def _moba_gate_kernel(q_ref, kmean_ref, qrow_ref, sel_ref, rank_ref, cnt_ref):
    t = pl.program_id(1)
    bs = MOBA_BLOCK
    tq = q_ref.shape[0]
    nb = kmean_ref.shape[1]

    q = q_ref[...]
    qrow_ref[...] = q.astype(F32)
    qt = q.astype(F32).T.astype(BF16)

    lane = lax.broadcasted_iota(jnp.int32, (1, tq), 1)
    own = t * (tq // bs) + lane // bs
    km = kmean_ref[0]
    km_hi = km.astype(BF16)
    km_lo = (km - km_hi.astype(F32)).astype(BF16)
    gate = _dot(km_hi, qt) + _dot(km_lo, qt)
    blk = lax.broadcasted_iota(jnp.int32, (nb, tq), 0)
    gate = jnp.where(blk < own, gate, NEG_INF)

    @pl.when(t == 0)
    def _():
        cnt_ref[...] = jnp.zeros_like(cnt_ref)

    run = cnt_ref[0][:, 0:1]
    qi = lax.broadcasted_iota(jnp.int32, (bs, bs), 0)
    qj = lax.broadcasted_iota(jnp.int32, (bs, bs), 1)
    before = jnp.where(qi < qj, 1.0, 0.0).astype(BF16)
    sels, ranks = [], []
    for r in range(MOBA_TOPK):
        mx = jnp.max(gate, axis=0, keepdims=True)
        first = jnp.min(jnp.where(gate == mx, blk, nb), axis=0, keepdims=True)
        hit = blk == first
        gate = jnp.where(hit, -jnp.inf, gate)
        valid = own > r
        oh = jnp.where(hit & valid, 1.0, 0.0)
        parts = []
        for c in range(tq // bs):
            ohc = oh[:, c * bs:(c + 1) * bs]
            prior = _dot(ohc.astype(BF16), before)
            parts.append(jnp.sum(ohc * (run + prior), axis=0, keepdims=True))
            run = run + jnp.sum(ohc, axis=1, keepdims=True)
        ranks.append(jnp.concatenate(parts, axis=1))
        sels.append(jnp.where(valid, first, -1))
    cnt_ref[0] = jnp.broadcast_to(run, cnt_ref.shape[1:])
    pad = jnp.zeros((8 - MOBA_TOPK, tq), jnp.int32)
    sel_ref[0] = jnp.concatenate(sels + [pad], axis=0)
    rank_ref[0] = jnp.concatenate([x.astype(jnp.int32) for x in ranks] + [pad], axis=0)


def _moba_gate(pa, kmean, batch, seq):
    nb = seq // MOBA_BLOCK
    tq = min(MOBA_QTILE * MOBA_BLOCK, seq)
    nt = seq // tq
    bh = batch * MOBA_HEADS

    def qmap(n, t):
        return ((n // MOBA_HEADS) * nt + t, n % MOBA_HEADS)

    return pl.pallas_call(
        _moba_gate_kernel,
        name="moba_gate",
        grid=(bh, nt),
        in_specs=[
            pl.BlockSpec((tq, HEAD_DIM), qmap),
            pl.BlockSpec((1, nb, HEAD_DIM), lambda n, t: (n, 0, 0)),
        ],
        out_specs=[
            pl.BlockSpec((tq, HEAD_DIM), lambda n, t: (n * nt + t, 0)),
            pl.BlockSpec((1, 8, tq), lambda n, t: (n, 0, t)),
            pl.BlockSpec((1, 8, tq), lambda n, t: (n, 0, t)),
            pl.BlockSpec((1, nb, HEAD_DIM), lambda n, t: (n, 0, 0)),
        ],
        out_shape=[
            jax.ShapeDtypeStruct((bh * seq, HEAD_DIM), F32),
            jax.ShapeDtypeStruct((bh, 8, seq), jnp.int32),
            jax.ShapeDtypeStruct((bh, 8, seq), jnp.int32),
            jax.ShapeDtypeStruct((bh, nb, HEAD_DIM), F32),
        ],
        compiler_params=_cparams(("parallel", "arbitrary")),
    )(pa, kmean)


def _moba_slot_kernel(sel_ref, rank_ref, base_ref, slot_ref, *, trash):
    nb = base_ref.shape[1]
    tq = sel_ref.shape[2]
    base = base_ref[0][:, 0:1]
    blk = lax.broadcasted_iota(jnp.int32, (nb, tq), 0)
    spare = trash + lax.broadcasted_iota(jnp.int32, (1, tq), 1) % SLOT_TILE
    rows = []
    for r in range(MOBA_TOPK):
        sel = sel_ref[0, r:r + 1, :]
        start = jnp.sum(jnp.where(blk == sel, base, 0.0), axis=0, keepdims=True)
        slot = start.astype(jnp.int32) + rank_ref[0, r:r + 1, :]
        rows.append(jnp.where(sel >= 0, slot, spare))
    rows.append(jnp.broadcast_to(spare, (8 - MOBA_TOPK, tq)))
    slot_ref[0] = jnp.concatenate(rows, axis=0)


def _moba_slots(sel, rank, base, trash):
    bh, _, seq = sel.shape
    nb = base.shape[1]
    tq = min(SLOTS_TQ, seq)
    spec = pl.BlockSpec((1, 8, tq), lambda n, t: (n, 0, t))
    return pl.pallas_call(
        functools.partial(_moba_slot_kernel, trash=trash),
        name="moba_slots",
        grid=(bh, seq // tq),
        in_specs=[spec, spec, pl.BlockSpec((1, nb, HEAD_DIM), lambda n, t: (n, 0, 0))],
        out_specs=spec,
        out_shape=jax.ShapeDtypeStruct((bh, 8, seq), jnp.int32),
        compiler_params=_cparams(("parallel", "parallel")),
    )(sel, rank, base)


def _sc_mesh():
    return plsc.VectorSubcoreMesh(core_axis_name="core", subcore_axis_name="subcore")


def _sc_scatter_rows(rows, slots, n_out):
    bh8, seq = slots.shape
    bh = bh8 // 8
    nw = seq // SC_WINDOW

    @pl.kernel(out_type=jax.ShapeDtypeStruct((n_out, HEAD_DIM), rows.dtype), mesh=_sc_mesh(),
               scratch_types=[])
    def scatter(x_hbm, i_hbm, o_hbm):
        def body(x_vmem, i_vmem):
            pltpu.sync_copy(x_vmem, o_hbm.at[i_vmem.at[0]])

        pltpu.emit_pipeline(
            body,
            grid=(bh * MOBA_TOPK * nw,),
            in_specs=[
                pl.BlockSpec((SC_WINDOW, HEAD_DIM),
                             index_map=lambda i: ((i // (MOBA_TOPK * nw)) * nw + i % nw, 0)),
                pl.BlockSpec((1, SC_WINDOW),
                             index_map=lambda i: ((i // (MOBA_TOPK * nw)) * 8 + (i // nw) % MOBA_TOPK,
                                                  i % nw)),
            ],
            out_specs=[],
            core_axis_name=("core", "subcore"),
            dimension_semantics=(pltpu.PARALLEL,),
        )(x_hbm, i_hbm)

    return scatter(rows, slots)


def _sc_gather_rows(table, slots):
    bh8, seq = slots.shape
    bh = bh8 // 8
    nw = seq // SC_WINDOW
    n_out = bh * MOBA_TOPK * seq

    @pl.kernel(out_type=jax.ShapeDtypeStruct((n_out, HEAD_DIM), table.dtype), mesh=_sc_mesh())
    def gather(x_hbm, i_hbm, o_hbm):
        def body(i_vmem, o_vmem):
            pltpu.sync_copy(x_hbm.at[i_vmem.at[0]], o_vmem)

        pltpu.emit_pipeline(
            body,
            grid=(bh * MOBA_TOPK * nw,),
            in_specs=[
                pl.BlockSpec((1, SC_WINDOW),
                             index_map=lambda i: ((i // (MOBA_TOPK * nw)) * 8 + (i // nw) % MOBA_TOPK,
                                                  i % nw)),
            ],
            out_specs=[pl.BlockSpec((SC_WINDOW, HEAD_DIM), index_map=lambda i: (i, 0))],
            core_axis_name=("core", "subcore"),
            dimension_semantics=(pltpu.PARALLEL,),
        )(i_hbm, o_hbm)

    return gather(table, slots)


def _moba_routed_kernel(tb_ref, nt_ref, qs_ref, k_ref, vtb_ref, slope_ref, o_ref,
                        sa_ref, sb_ref, sc_ref):
    n = pl.program_id(0)
    g = pl.program_id(1)
    bs = MOBA_BLOCK
    tpg = qs_ref.shape[0] // SLOT_TILE
    tiles_cap = pl.num_programs(1) * tpg

    @pl.when(g * tpg < nt_ref[n])
    def _():
        krow = lax.broadcasted_iota(jnp.int32, (bs, SLOT_TILE), 0).astype(F32)
        bias = slope_ref[0, 0:1, 0:1] * krow
        frow = lax.broadcasted_iota(jnp.int32, (HEAD_DIM, SLOT_TILE), 0)
        blocks = [tb_ref[n * tiles_cap + g * tpg + u] for u in range(tpg)]

        def scores(u):
            qt = qs_ref[u * SLOT_TILE:(u + 1) * SLOT_TILE, :].T.astype(BF16)
            k = k_ref[pl.ds(pl.multiple_of(blocks[u] * bs, bs), bs), :]
            return _dot(k, qt) + bias

        bufs = (sa_ref, sb_ref, sc_ref)
        sa_ref[...] = scores(0)
        sb_ref[...] = scores(1)
        for u in range(tpg):
            if u + 2 < tpg:
                bufs[(u + 2) % 3][...] = scores(u + 2)
            s = bufs[u % 3][...]
            m = jnp.max(s, axis=0, keepdims=True)
            p = jnp.exp(s - m).astype(BF16)
            acc = _dot(vtb_ref[0, blocks[u]], p)
            l = acc[HEAD_DIM:HEAD_DIM + 1]
            o = (acc[0:HEAD_DIM] / l).astype(BF16).astype(F32)
            lse = m + jnp.log(l)
            ob = lax.bitcast_convert_type(o, jnp.uint32)
            lb = lax.bitcast_convert_type(lse, jnp.uint32)
            extra = jnp.where(frow == 0, lb >> 16, jnp.where(frow == 1, lb & 0xFFFF, 0))
            packed = lax.bitcast_convert_type(ob | extra, F32)
            o_ref[u * SLOT_TILE:(u + 1) * SLOT_TILE, :] = packed.T


def _moba_routed(qs, pa, vtb, slopes, tile_block, n_tiles, tiles_cap):
    bh, nb = vtb.shape[0], vtb.shape[1]
    seq = nb * MOBA_BLOCK
    kcol = MOBA_W // HEAD_DIM
    tpg = ROUTED_TILES_PER_STEP
    steps = tiles_cap // tpg
    rows = tpg * SLOT_TILE

    def qmap(n, g, tb, nt):
        used = jnp.maximum((nt[n] + tpg - 1) // tpg, 1)
        return (n * steps + jnp.minimum(g, used - 1), 0)

    grid_spec = pltpu.PrefetchScalarGridSpec(
        num_scalar_prefetch=2,
        grid=(bh, steps),
        in_specs=[
            pl.BlockSpec((rows, HEAD_DIM), qmap),
            pl.BlockSpec((seq, HEAD_DIM),
                         lambda n, g, tb, nt: (n // MOBA_HEADS, kcol + n % MOBA_HEADS)),
            pl.BlockSpec((1, nb, VT_ROWS, MOBA_BLOCK), lambda n, g, tb, nt: (n, 0, 0, 0)),
            pl.BlockSpec((1, 1, HEAD_DIM), lambda n, g, tb, nt: (n, 0, 0)),
        ],
        out_specs=pl.BlockSpec((rows, HEAD_DIM), qmap),
        scratch_shapes=[pltpu.VMEM((MOBA_BLOCK, SLOT_TILE), F32)] * 3,
    )
    return pl.pallas_call(
        _moba_routed_kernel,
        name="moba_routed",
        grid_spec=grid_spec,
        out_shape=jax.ShapeDtypeStruct((qs.shape[0], HEAD_DIM), F32),
        compiler_params=_cparams(("parallel", "arbitrary")),
    )(tile_block, n_tiles, qs, pa, vtb, slopes)


def _moba_merge_kernel(q_ref, k_ref, vtb_ref, og_ref, sel_ref, slope_ref, o_ref,
                       sa_ref, sb_ref, sc_ref):
    t = pl.program_id(1)
    bs = MOBA_BLOCK
    tq = q_ref.shape[0]
    slope = slope_ref[0, 0:1, 0:1]

    qt = q_ref[...].astype(F32).T.astype(BF16)
    lane = lax.broadcasted_iota(jnp.int32, (1, tq), 1)
    tpos = (t * tq + lane).astype(F32)
    lk = lax.broadcasted_iota(jnp.int32, (bs, bs), 0)
    lq = lax.broadcasted_iota(jnp.int32, (bs, bs), 1)
    dist = (lq - lk).astype(F32)
    nq = tq // bs

    def scores(u):
        return _dot(k_ref[u * bs:(u + 1) * bs, :], qt[:, u * bs:(u + 1) * bs])

    sbufs = (sa_ref, sb_ref, sc_ref)
    for u in range(min(2, nq)):
        sbufs[u][...] = scores(u)
    accs, ms = [], []
    for u in range(nq):
        if u + 2 < nq:
            sbufs[(u + 2) % 3][...] = scores(u + 2)
        s = sbufs[u % 3][...]
        s = jnp.where(dist >= 0.0, s - slope * dist, NEG_INF)
        m_u = jnp.max(s, axis=0, keepdims=True)
        p = jnp.exp(s - m_u).astype(BF16)
        accs.append(_dot(vtb_ref[0, u], p))
        ms.append(m_u)
    acc = jnp.concatenate(accs, axis=1)
    l = acc[HEAD_DIM:HEAD_DIM + 1]
    parts = [acc[0:HEAD_DIM] / l]
    lses = [jnp.concatenate(ms, axis=1) + jnp.log(l)]

    for r in range(MOBA_TOPK):
        u = lax.bitcast_convert_type(og_ref[0, r, 0].T, jnp.uint32)
        lb = ((u[0:1] & 0xFFFF) << 16) | (u[1:2] & 0xFFFF)
        lse_r = lax.bitcast_convert_type(lb, F32)
        sel_r = sel_ref[0, r:r + 1, :]
        valid = sel_r >= 0
        lses.append(jnp.where(valid, lse_r + slope * ((sel_r * bs).astype(F32) - tpos), NEG_INF))
        hi_half = lax.bitcast_convert_type((u >> 16) << 16, F32)
        parts.append(jnp.where(valid, hi_half, 0.0))
    mx = functools.reduce(jnp.maximum, lses)
    ws = [jnp.exp(x - mx) for x in lses]
    num = functools.reduce(lambda a, b: a + b, [w * o for w, o in zip(ws, parts)])
    out_t = num / functools.reduce(lambda a, b: a + b, ws)
    o_ref[...] = out_t.T.astype(o_ref.dtype)


def _moba_merge(pa, vtb, og, sel, slopes, batch, seq):
    tq = MOBA_QTILE * MOBA_BLOCK
    nt = seq // tq
    bh = batch * MOBA_HEADS
    kcol = MOBA_W // HEAD_DIM
    og = og.reshape(bh, MOBA_TOPK, nt, tq, HEAD_DIM)

    def qmap(n, t):
        return ((n // MOBA_HEADS) * nt + t, n % MOBA_HEADS)

    def kmap(n, t):
        return ((n // MOBA_HEADS) * nt + t, kcol + n % MOBA_HEADS)

    return pl.pallas_call(
        _moba_merge_kernel,
        name="moba_merge",
        grid=(bh, nt),
        in_specs=[
            pl.BlockSpec((tq, HEAD_DIM), qmap),
            pl.BlockSpec((tq, HEAD_DIM), kmap),
            pl.BlockSpec((1, MOBA_QTILE, VT_ROWS, MOBA_BLOCK), lambda n, t: (n, t, 0, 0)),
            pl.BlockSpec((1, MOBA_TOPK, 1, tq, HEAD_DIM), lambda n, t: (n, 0, t, 0, 0)),
            pl.BlockSpec((1, 8, tq), lambda n, t: (n, 0, t)),
            pl.BlockSpec((1, 1, HEAD_DIM), lambda n, t: (n, 0, 0)),
        ],
        out_specs=pl.BlockSpec((tq, HEAD_DIM), qmap),
        out_shape=jax.ShapeDtypeStruct((batch * seq, MOBA_W), BF16),
        scratch_shapes=[pltpu.VMEM((MOBA_BLOCK, MOBA_BLOCK), F32)] * 3,
        compiler_params=_cparams(("parallel", "parallel")),
    )(pa, pa, vtb, og, sel, slopes)


def _moba_routed_attention(pa, vtb, kmean, slopes, batch, seq, companion):
    nb = seq // MOBA_BLOCK
    bh = batch * MOBA_HEADS
    qrows, sel, rank, cnt = _moba_gate(pa, kmean, batch, seq)

    step_slots = SLOT_TILE * ROUTED_TILES_PER_STEP
    cap = -(-(MOBA_TOPK * seq + nb * SLOT_TILE) // step_slots) * step_slots
    tiles_cap = cap // SLOT_TILE
    counts = cnt[:, :, 0].astype(jnp.int32)
    padded = (counts + SLOT_TILE - 1) // SLOT_TILE * SLOT_TILE
    ends = jnp.cumsum(padded, axis=1)
    base = ends - padded + (jnp.arange(bh, dtype=jnp.int32) * cap)[:, None]
    n_tiles = ends[:, -1] // SLOT_TILE
    tile_idx = jnp.arange(tiles_cap, dtype=jnp.int32)
    tile_block = jnp.sum(ends[:, None, :] // SLOT_TILE <= tile_idx[None, :, None], axis=-1)
    tile_block = jnp.minimum(tile_block, nb - 1).astype(jnp.int32).reshape(bh * tiles_cap)
    base_b = jnp.broadcast_to(base.astype(F32)[:, :, None], (bh, nb, HEAD_DIM))

    n_rows = bh * cap + SLOT_TILE
    slots = _moba_slots(sel, rank, base_b, bh * cap).reshape(bh * 8, seq)
    qs = _sc_scatter_rows(qrows, slots, n_rows)
    qs, companion = lax.optimization_barrier((qs, companion))
    part = _moba_routed(qs, pa, vtb, slopes, tile_block, n_tiles, tiles_cap)
    og = _sc_gather_rows(part, slots)
    return _moba_merge(pa, vtb, og, sel, slopes, batch, seq), companion


def _split3(x):
    hi = x.astype(BF16)
    r1 = x - hi.astype(F32)
    mid = r1.astype(BF16)
    lo = (r1 - mid.astype(F32)).astype(BF16)
    return hi, mid, lo


def _hgrn_kernel(qb_ref, fb_ref, ib_ref, gb_ref, lbl_ref, gn_ref, o_ref, s_ref, oi_ref, st_ref,
                 *, layer):
    tt = qb_ref.shape[0]
    sub = HGRN_SUB
    hd = HEAD_DIM

    @pl.when(pl.program_id(1) == 0)
    def _():
        s_ref[...] = jnp.zeros_like(s_ref)

    logits = lbl_ref[...]
    e = jnp.exp(logits - jnp.max(logits, axis=0, keepdims=True))
    pl_ = e / jnp.sum(e, axis=0, keepdims=True)
    lb = jnp.sum(pl_[0:layer + 1], axis=0, keepdims=True) - pl_[0:1]

    fb = fb_ref[...].astype(F32)
    t = jnp.exp(-jnp.abs(fb))
    r = 1.0 / (1.0 + t)
    sig_pos = jnp.where(fb >= 0, r, t * r)
    sig_neg = jnp.where(fb >= 0, t * r, r)
    f_gate = lb + (1.0 - lb) * sig_pos
    logf = jnp.log(jnp.maximum(f_gate, F_MIN))
    k_all = (1.0 - lb) * sig_neg
    qb = qb_ref[...].astype(F32)
    q_all = qb * jax.nn.sigmoid(qb)
    v_all = ib_ref[...].astype(F32)

    ri = lax.broadcasted_iota(jnp.int32, (tt, tt), 0)
    ci = lax.broadcasted_iota(jnp.int32, (tt, tt), 1)
    same = (ri // sub) == (ci // sub)
    tri = jnp.where(same & (ci <= ri), 1.0, 0.0).astype(BF16)
    hi, mid, lo = _split3(logf)
    gl = (_dot(tri, hi) + _dot(tri, mid) + _dot(tri, lo)) * LOG2E

    half = sub // 2
    nsub = tt // sub
    nv = tt // half
    lane_sum = jnp.ones((hd, hd), BF16)
    row_in = lax.broadcasted_iota(jnp.int32, (nv, half, hd), 1)

    def pair_terms(qx, gx, kx, vx, causal):
        n = qx.shape[0]
        out = jnp.zeros_like(qx)
        for rho in range(half):
            kr = pltpu.roll(kx, rho, 1) if rho else kx
            vr = pltpu.roll(vx, rho, 1) if rho else vx
            gr = pltpu.roll(gx[1], rho, 1) if rho else gx[1]
            prod = qx * kr * jnp.exp2(gx[0] - gr)
            if causal and rho:
                prod = jnp.where(row_in >= rho, prod, 0.0)
            a = _dot(prod.reshape(n * half, hd).astype(BF16), lane_sum)
            out = out + a.reshape(n, half, hd) * vr
        return out

    for h in range(HGRN_HEADS):
        cs = slice(h * hd, (h + 1) * hd)
        q = q_all[:, cs]
        k = k_all[:, cs]
        v = v_all[:, cs]
        g = gl[:, cs]

        q3, k3, v3, g3 = (a.reshape(nv, half, hd) for a in (q, k, v, g))
        od = pair_terms(q3, (g3, g3), k3, v3, True).reshape(nsub, 2, half, hd)
        q4, k4, v4, g4 = (a.reshape(nsub, 2, half, hd) for a in (q, k, v, g))
        oh = pair_terms(q4[:, 1], (g4[:, 1], g4[:, 0]), k4[:, 0], v4[:, 0], False)
        o_diag = jnp.stack([od[:, 0], od[:, 1] + oh], axis=1).reshape(tt, hd)

        gs = g.reshape(nsub, sub, hd)
        g_end = gs[:, sub - 1:sub, :]
        qd = (q * jnp.exp2(g)).astype(BF16).reshape(nsub, sub, hd)
        kd = (k.reshape(nsub, sub, hd) * jnp.exp2(g_end - gs)).astype(BF16)
        vb = v.astype(BF16).reshape(nsub, sub, hd)
        dec = jnp.exp2(g_end)
        upd = [_dot_t0(vb[c], kd[c]) for c in range(nsub)]
        st = s_ref[h]
        for c in range(nsub):
            st_ref[c] = st.astype(BF16)
            st = st * dec[c] + upd[c]
        s_ref[h] = st
        for c in range(nsub):
            oi_ref[c * sub:(c + 1) * sub, cs] = lax.dot_general(
                qd[c], st_ref[c], (((1,), (1,)), ((), ())), preferred_element_type=F32)

        o = oi_ref[:, cs] + o_diag
        rr = lax.rsqrt(jnp.mean(o * o, axis=-1, keepdims=True) + RMS_EPS)
        gate = jax.nn.sigmoid(gb_ref[:, cs].astype(F32))
        o_ref[:, cs] = (o * rr * gn_ref[...] * gate).astype(o_ref.dtype)


def _hgrn(pa, lb_logits, out_norm, layer, batch, seq):
    tt = min(HGRN_TT, seq)
    nt = seq // tt
    c0 = 3 * MOBA_W // HGRN_W
    nl = lb_logits.shape[0]

    def cmap(off):
        return lambda b, t: (b * nt + t, c0 + off)

    return pl.pallas_call(
        functools.partial(_hgrn_kernel, layer=layer),
        name="hgrn",
        grid=(batch, nt),
        in_specs=[
            pl.BlockSpec((tt, HGRN_W), cmap(0)),
            pl.BlockSpec((tt, HGRN_W), cmap(1)),
            pl.BlockSpec((tt, HGRN_W), cmap(2)),
            pl.BlockSpec((tt, HGRN_W), cmap(3)),
            pl.BlockSpec((nl, HGRN_W), lambda b, t: (0, 0)),
            pl.BlockSpec((1, HEAD_DIM), lambda b, t: (0, 0)),
        ],
        out_specs=pl.BlockSpec((tt, HGRN_W), lambda b, t: (b * nt + t, 0)),
        out_shape=jax.ShapeDtypeStruct((batch * seq, HGRN_W), BF16),
        scratch_shapes=[pltpu.VMEM((HGRN_HEADS, HEAD_DIM, HEAD_DIM), F32),
                        pltpu.VMEM((tt, HGRN_W), F32),
                        pltpu.VMEM((tt // HGRN_SUB, HEAD_DIM, HEAD_DIM), BF16)],
        compiler_params=_cparams(("parallel", "arbitrary")),
    )(pa, pa, pa, pa, lb_logits, out_norm.reshape(1, HEAD_DIM))


def _memkv_kernel(mem_ref, g_ref, w_ref, kt_ref, v_ref):
    h = _rms(mem_ref[0], g_ref[...]).astype(BF16)
    kv = _dot(h, w_ref[...])
    kt_ref[0] = kv[:, 0:MEM_W].T.astype(BF16)
    v_ref[0] = kv[:, MEM_W:].astype(BF16)


def _memkv(mem, g, w):
    b, m, d = mem.shape
    return pl.pallas_call(
        _memkv_kernel,
        name="memkv",
        grid=(b,),
        in_specs=[
            pl.BlockSpec((1, m, d), lambda i: (i, 0, 0)),
            pl.BlockSpec((1, d), lambda i: (0, 0)),
            pl.BlockSpec((d, 2 * MEM_W), lambda i: (0, 0)),
        ],
        out_specs=[
            pl.BlockSpec((1, MEM_W, m), lambda i: (i, 0, 0)),
            pl.BlockSpec((1, m, MEM_W), lambda i: (i, 0, 0)),
        ],
        out_shape=[
            jax.ShapeDtypeStruct((b, MEM_W, m), BF16),
            jax.ShapeDtypeStruct((b, m, MEM_W), BF16),
        ],
        compiler_params=_cparams(("parallel",)),
    )(mem, g.reshape(1, d), w)


def _merge_kernel(x_ref, oa_ref, ob_ref, qm_ref, g0_ref, g1_ref, g2_ref, mkt_ref, mv_ref,
                  wa_ref, wb_ref, wm_ref, wo_ref, o_ref, om_ref):
    scale = HEAD_DIM ** -0.5
    for h in range(MEM_HEADS):
        cs = slice(h * HEAD_DIM, (h + 1) * HEAD_DIM)
        s = _dot(qm_ref[:, cs], mkt_ref[0, cs, :]) * scale
        p = jnp.exp(s - jnp.max(s, axis=-1, keepdims=True))
        l = jnp.sum(p, axis=-1, keepdims=True)
        om_ref[:, cs] = (_dot(p.astype(BF16), mv_ref[0, :, cs]) / l).astype(BF16)

    y = jax.nn.sigmoid(g0_ref[...].astype(F32)) * _dot(oa_ref[...], wa_ref[...])
    y += jax.nn.sigmoid(g1_ref[...].astype(F32)) * _dot(ob_ref[...], wb_ref[...])
    y += jax.nn.sigmoid(g2_ref[...].astype(F32)) * _dot(om_ref[...], wm_ref[...])
    o_ref[...] = x_ref[...] + _dot(y.astype(BF16), wo_ref[...])


def _merge(x, oa, ob, pa, gates, mkt, mv, wa, wb, wm, wo, batch, seq):
    n, d = x.shape
    m = mv.shape[1]
    tm = _pick(seq, MERGE_TM)
    per_b = seq // tm
    qcol = (3 * MOBA_W + 4 * HGRN_W) // MEM_W
    row = lambda i: (i, 0)
    return pl.pallas_call(
        _merge_kernel,
        name="merge",
        grid=(n // tm,),
        in_specs=[
            pl.BlockSpec((tm, d), row),
            pl.BlockSpec((tm, MOBA_W), row),
            pl.BlockSpec((tm, HGRN_W), row),
            pl.BlockSpec((tm, MEM_W), lambda i: (i, qcol)),
            pl.BlockSpec((tm, d), lambda i: (i, 0)),
            pl.BlockSpec((tm, d), lambda i: (i, 1)),
            pl.BlockSpec((tm, d), lambda i: (i, 2)),
            pl.BlockSpec((1, MEM_W, m), lambda i: (i // per_b, 0, 0)),
            pl.BlockSpec((1, m, MEM_W), lambda i: (i // per_b, 0, 0)),
            _const_spec((MOBA_W, d)),
            _const_spec((HGRN_W, d)),
            _const_spec((MEM_W, d)),
            _const_spec((d, d)),
        ],
        out_specs=pl.BlockSpec((tm, d), row),
        out_shape=jax.ShapeDtypeStruct((n, d), F32),
        scratch_shapes=[pltpu.VMEM((tm, MEM_W), BF16)],
        compiler_params=_cparams(("parallel",)),
    )(x, oa, ob, pa, gates, gates, gates, mkt, mv, wa, wb, wm, wo)


def kernel(x, mem, ffn1_norm, ffn1_w1, ffn1_w3, ffn1_w2, mix_norm, w_in, hgrn_lb_logits,
           hgrn_out_norm, mem_norm, w_mem_kv, w_proj_moba, w_proj_hgrn, w_proj_mem, w_out,
           ffn2_norm, ffn2_w1, ffn2_w3, ffn2_w2, final_norm):
    batch, seq, d = x.shape
    depth = ffn1_w1.shape[0]
    assert seq % (MOBA_QTILE * MOBA_BLOCK) == 0
    assert w_in.shape[-1] == MIX_W + 3 * d
    hs = jnp.arange(1, MOBA_HEADS + 1, dtype=F32)
    slopes = jnp.tile(jnp.exp2(-8.0 * hs / MOBA_HEADS), batch)
    slopes = jnp.broadcast_to(slopes[:, None, None], (batch * MOBA_HEADS, 1, HEAD_DIM))
    mix_scale = jnp.concatenate([jnp.full((MOBA_W,), HEAD_DIM ** -0.5, F32),
                                 jnp.ones((MIX_W - MOBA_W,), F32)])
    gate_scale = jnp.ones((3 * d,), F32)

    xs = x.reshape(batch * seq, d)
    for l in range(depth):
        last = l == depth - 1
        bf = functools.partial(_cast_layer, layer=l)
        xs = _ffn(xs, ffn1_norm[l], bf(ffn1_w1), bf(ffn1_w3), bf(ffn1_w2), final_norm, final=False)

        w_mix, w_gates = bf(w_in, splits=(MIX_W, 3 * d))
        pa = _normproj(xs, mix_norm[l], w_mix, mix_scale)
        gates = _normproj(xs, mix_norm[l], w_gates, gate_scale)

        vtb, kmean = _moba_prep(pa, batch, seq)
        ob = _hgrn(pa, hgrn_lb_logits, hgrn_out_norm[l], l, batch, seq)
        oa, ob = _moba_routed_attention(pa, vtb, kmean, slopes, batch, seq, ob)
        mkt, mv = _memkv(mem, mem_norm[l], bf(w_mem_kv))
        xs = _merge(xs, oa, ob, pa, gates, mkt, mv, bf(w_proj_moba), bf(w_proj_hgrn),
                    bf(w_proj_mem), bf(w_out), batch, seq)

        xs = _ffn(xs, ffn2_norm[l], bf(ffn2_w1), bf(ffn2_w3), bf(ffn2_w2), final_norm, final=last)
    return xs.reshape(batch, seq, d)
```

```python
import functools

import jax
import jax.numpy as jnp
from jax import lax
from jax.experimental import pallas as pl
from jax.experimental.pallas import tpu as pltpu
from jax.experimental.pallas import tpu_sc as plsc

F32 = jnp.float32
BF16 = jnp.bfloat16

HEAD_DIM = 128
MOBA_HEADS = 8
MOBA_BLOCK = 256
MOBA_TOPK = 3
HGRN_HEADS = 4
MEM_HEADS = 4
RMS_EPS = 1e-6
NEG_INF = -1e30
F_MIN = 1e-20
LOG2E = 1.4426950408889634

MOBA_W = MOBA_HEADS * HEAD_DIM
HGRN_W = HGRN_HEADS * HEAD_DIM
MEM_W = MEM_HEADS * HEAD_DIM
MIX_W = 3 * MOBA_W + 4 * HGRN_W + MEM_W

VMEM_LIMIT_BYTES = 60 * 1024 * 1024

LANE = 128
BF16_ROWS = 16
VT_ROWS = HEAD_DIM + BF16_ROWS
MOBA_QTILE = 4

HGRN_SUB = 16
HGRN_TT = 256
FFN_TM = 1024
FFN_TF = 512
FFN_OUT_CHUNK = 512
PROJ_TM = 512
PROJ_TN = 512
MERGE_TM = 512
SLOTS_TQ = 2048
CAST_BLOCK_BYTES = 8 * 1024 * 1024
SLOT_TILE = 512
ROUTED_TILES_PER_STEP = 8
SC_WINDOW = 128


def _cparams(sem):
    return pltpu.CompilerParams(dimension_semantics=sem, vmem_limit_bytes=VMEM_LIMIT_BYTES)


def _dot(a, b):
    return jnp.dot(a, b, preferred_element_type=F32)


def _dot_t0(a, b):
    return lax.dot_general(a, b, (((0,), (0,)), ((), ())), preferred_element_type=F32)


def _rms(xf, g):
    r = lax.rsqrt(jnp.mean(xf * xf, axis=-1, keepdims=True) + RMS_EPS)
    return xf * r * g


def _pick(n, want):
    if n <= want:
        return n
    t = (want // LANE) * LANE
    while t >= LANE:
        if n % t == 0:
            return t
        t -= LANE
    return n


def _const_spec(shape):
    nd = len(shape)
    return pl.BlockSpec(shape, lambda *_: (0,) * nd, pipeline_mode=pl.Buffered(1))


def _cast_kernel(w_ref, *o_refs, splits):
    lo = 0
    for o_ref, width in zip(o_refs, splits):
        o_ref[...] = w_ref[0, :, lo:lo + width].astype(BF16)
        lo += width


def _cast_layer(w_stack, layer, splits=None):
    _, rows, cols = w_stack.shape
    splits = (cols,) if splits is None else tuple(splits)
    assert sum(splits) == cols
    tr = rows
    while tr * cols * 4 > CAST_BLOCK_BYTES and tr % (2 * BF16_ROWS) == 0:
        tr //= 2
    outs = pl.pallas_call(
        functools.partial(_cast_kernel, splits=splits),
        name="cast",
        grid=(rows // tr,),
        in_specs=[pl.BlockSpec((1, tr, cols), lambda i: (layer, i, 0))],
        out_specs=[pl.BlockSpec((tr, w), lambda i: (i, 0)) for w in splits],
        out_shape=[jax.ShapeDtypeStruct((rows, w), BF16) for w in splits],
        compiler_params=_cparams(("parallel",)),
    )(w_stack)
    return outs[0] if len(splits) == 1 else outs


def _ffn_kernel(x_ref, g_ref, w1_ref, w3_ref, w2_ref, fg_ref, o_ref, h_ref, *, final):
    j = pl.program_id(1)

    @pl.when(j == 0)
    def _():
        xf = x_ref[...]
        h_ref[...] = _rms(xf, g_ref[...]).astype(BF16)
        o_ref[...] = xf

    h = h_ref[...]
    u = _dot(h, w1_ref[...])
    v = _dot(h, w3_ref[...])
    a = (0.5 * u * jax.nn.sigmoid(u) * v).astype(BF16)
    tn = FFN_OUT_CHUNK if o_ref.shape[1] % FFN_OUT_CHUNK == 0 else o_ref.shape[1]
    for c in range(o_ref.shape[1] // tn):
        cs = slice(c * tn, (c + 1) * tn)
        o_ref[:, cs] += _dot(a, w2_ref[:, cs])

    if final:
        @pl.when(j == pl.num_programs(1) - 1)
        def _():
            o_ref[...] = _rms(o_ref[...], fg_ref[...])


def _ffn(x, g, w1, w3, w2, final_g, *, final):
    n, d = x.shape
    dff = w1.shape[1]
    tm = _pick(n, FFN_TM)
    tf = _pick(dff, FFN_TF)
    return pl.pallas_call(
        functools.partial(_ffn_kernel, final=final),
        name="ffn_final" if final else "ffn",
        grid=(n // tm, dff // tf),
        in_specs=[
            pl.BlockSpec((tm, d), lambda i, j: (i, 0)),
            pl.BlockSpec((1, d), lambda i, j: (0, 0)),
            pl.BlockSpec((d, tf), lambda i, j: (0, j)),
            pl.BlockSpec((d, tf), lambda i, j: (0, j)),
            pl.BlockSpec((tf, d), lambda i, j: (j, 0)),
            pl.BlockSpec((1, d), lambda i, j: (0, 0)),
        ],
        out_specs=pl.BlockSpec((tm, d), lambda i, j: (i, 0)),
        out_shape=jax.ShapeDtypeStruct((n, d), F32),
        scratch_shapes=[pltpu.VMEM((tm, d), BF16)],
        compiler_params=_cparams(("parallel", "arbitrary")),
    )(x, g.reshape(1, d), w1, w3, w2, final_g.reshape(1, d))


def _normproj_kernel(x_ref, g_ref, w_ref, s_ref, o_ref, h_ref, *, tn):
    h_ref[...] = _rms(x_ref[...], g_ref[...]).astype(BF16)
    for c in range(w_ref.shape[1] // tn):
        cs = slice(c * tn, (c + 1) * tn)
        o_ref[:, cs] = (_dot(h_ref[...], w_ref[:, cs]) * s_ref[:, cs]).astype(o_ref.dtype)


def _normproj(x, g, w, col_scale):
    n, d = x.shape
    nout = w.shape[1]
    tm = _pick(n, PROJ_TM)
    tn = _pick(nout, PROJ_TN)
    return pl.pallas_call(
        functools.partial(_normproj_kernel, tn=tn),
        name="normproj",
        grid=(n // tm,),
        in_specs=[
            pl.BlockSpec((tm, d), lambda i: (i, 0)),
            pl.BlockSpec((1, d), lambda i: (0, 0)),
            pl.BlockSpec((d, nout), lambda i: (0, 0), pipeline_mode=pl.Buffered(1)),
            pl.BlockSpec((1, nout), lambda i: (0, 0)),
        ],
        out_specs=pl.BlockSpec((tm, nout), lambda i: (i, 0)),
        out_shape=jax.ShapeDtypeStruct((n, nout), BF16),
        scratch_shapes=[pltpu.VMEM((tm, d), BF16)],
        compiler_params=_cparams(("parallel",)),
    )(x, g.reshape(1, d), w, col_scale.reshape(1, nout))


def _moba_prep_kernel(k_ref, v_ref, vtb_ref, kmean_ref):
    g = pl.program_id(1)
    bs = MOBA_BLOCK
    nblk = k_ref.shape[0] // bs
    orow = lax.broadcasted_iota(jnp.int32, (VT_ROWS - HEAD_DIM, bs), 0)
    ones_row = jnp.where(orow == 0, 1.0, 0.0).astype(BF16)
    for u in range(nblk):
        k = k_ref[u * bs:(u + 1) * bs, :]
        kmean_ref[0, pl.ds(g * nblk + u, 1), :] = jnp.mean(k.astype(F32), axis=0, keepdims=True)
        vt = v_ref[u * bs:(u + 1) * bs, :].astype(F32).T
        vtb_ref[0, u, 0:HEAD_DIM, :] = vt.astype(BF16)
        vtb_ref[0, u, HEAD_DIM:VT_ROWS, :] = ones_row


def _moba_prep(pa, batch, seq):
    nb = seq // MOBA_BLOCK
    grp = MOBA_QTILE
    ng = nb // grp
    bh = batch * MOBA_HEADS
    kcol = MOBA_W // HEAD_DIM
    vcol = 2 * MOBA_W // HEAD_DIM

    def kmap(n, g):
        return ((n // MOBA_HEADS) * ng + g, kcol + n % MOBA_HEADS)

    def vmap(n, g):
        return ((n // MOBA_HEADS) * ng + g, vcol + n % MOBA_HEADS)

    return pl.pallas_call(
        _moba_prep_kernel,
        name="moba_prep",
        grid=(bh, ng),
        in_specs=[
            pl.BlockSpec((grp * MOBA_BLOCK, HEAD_DIM), kmap),
            pl.BlockSpec((grp * MOBA_BLOCK, HEAD_DIM), vmap),
        ],
        out_specs=[
            pl.BlockSpec((1, grp, VT_ROWS, MOBA_BLOCK), lambda n, g: (n, g, 0, 0)),
            pl.BlockSpec((1, nb, HEAD_DIM), lambda n, g: (n, 0, 0)),
        ],
        out_shape=[
            jax.ShapeDtypeStruct((bh, nb, VT_ROWS, MOBA_BLOCK), BF16),
            jax.ShapeDtypeStruct((bh, nb, HEAD_DIM), F32),
        ],
        compiler_params=_cparams(("parallel", "arbitrary")),
    )(pa, pa)


def _moba_gate_kernel(q_ref, kmean_ref, qrow_ref, sel_ref, rank_ref, cnt_ref):
    t = pl.program_id(1)
    bs = MOBA_BLOCK
    tq = q_ref.shape[0]
    nb = kmean_ref.shape[1]

    q = q_ref[...]
    qrow_ref[...] = q.astype(F32)
    qt = q.astype(F32).T.astype(BF16)

    lane = lax.broadcasted_iota(jnp.int32, (1, tq), 1)
    own = t * (tq // bs) + lane // bs
    km = kmean_ref[0]
    km_hi = km.astype(BF16)
    km_lo = (km - km_hi.astype(F32)).astype(BF16)
    gate = _dot(km_hi, qt) + _dot(km_lo, qt)
    blk = lax.broadcasted_iota(jnp.int32, (nb, tq), 0)
    gate = jnp.where(blk < own, gate, NEG_INF)

    @pl.when(t == 0)
    def _():
        cnt_ref[...] = jnp.zeros_like(cnt_ref)

    run = cnt_ref[0][:, 0:1]
    qi = lax.broadcasted_iota(jnp.int32, (bs, bs), 0)
    qj = lax.broadcasted_iota(jnp.int32, (bs, bs), 1)
    before = jnp.where(qi < qj, 1.0, 0.0).astype(BF16)
    sels, ranks = [], []
    for r in range(MOBA_TOPK):
        mx = jnp.max(gate, axis=0, keepdims=True)
        first = jnp.min(jnp.where(gate == mx, blk, nb), axis=0, keepdims=True)
        hit = blk == first
        gate = jnp.where(hit, -jnp.inf, gate)
        valid = own > r
        oh = jnp.where(hit & valid, 1.0, 0.0)
        parts = []
        for c in range(tq // bs):
            ohc = oh[:, c * bs:(c + 1) * bs]
            prior = _dot(ohc.astype(BF16), before)
            parts.append(jnp.sum(ohc * (run + prior), axis=0, keepdims=True))
            run = run + jnp.sum(ohc, axis=1, keepdims=True)
        ranks.append(jnp.concatenate(parts, axis=1))
        sels.append(jnp.where(valid, first, -1))
    cnt_ref[0] = jnp.broadcast_to(run, cnt_ref.shape[1:])
    pad = jnp.zeros((8 - MOBA_TOPK, tq), jnp.int32)
    sel_ref[0] = jnp.concatenate(sels + [pad], axis=0)
    rank_ref[0] = jnp.concatenate([x.astype(jnp.int32) for x in ranks] + [pad], axis=0)


def _moba_gate(pa, kmean, batch, seq):
    nb = seq // MOBA_BLOCK
    tq = min(MOBA_QTILE * MOBA_BLOCK, seq)
    nt = seq // tq
    bh = batch * MOBA_HEADS

    def qmap(n, t):
        return ((n // MOBA_HEADS) * nt + t, n % MOBA_HEADS)

    return pl.pallas_call(
        _moba_gate_kernel,
        name="moba_gate",
        grid=(bh, nt),
        in_specs=[
            pl.BlockSpec((tq, HEAD_DIM), qmap),
            pl.BlockSpec((1, nb, HEAD_DIM), lambda n, t: (n, 0, 0)),
        ],
        out_specs=[
            pl.BlockSpec((tq, HEAD_DIM), lambda n, t: (n * nt + t, 0)),
            pl.BlockSpec((1, 8, tq), lambda n, t: (n, 0, t)),
            pl.BlockSpec((1, 8, tq), lambda n, t: (n, 0, t)),
            pl.BlockSpec((1, nb, HEAD_DIM), lambda n, t: (n, 0, 0)),
        ],
        out_shape=[
            jax.ShapeDtypeStruct((bh * seq, HEAD_DIM), F32),
            jax.ShapeDtypeStruct((bh, 8, seq), jnp.int32),
            jax.ShapeDtypeStruct((bh, 8, seq), jnp.int32),
            jax.ShapeDtypeStruct((bh, nb, HEAD_DIM), F32),
        ],
        compiler_params=_cparams(("parallel", "arbitrary")),
    )(pa, kmean)


def _moba_slot_kernel(sel_ref, rank_ref, base_ref, slot_ref, *, trash):
    nb = base_ref.shape[1]
    tq = sel_ref.shape[2]
    base = base_ref[0][:, 0:1]
    blk = lax.broadcasted_iota(jnp.int32, (nb, tq), 0)
    spare = trash + lax.broadcasted_iota(jnp.int32, (1, tq), 1) % SLOT_TILE
    rows = []
    for r in range(MOBA_TOPK):
        sel = sel_ref[0, r:r + 1, :]
        start = jnp.sum(jnp.where(blk == sel, base, 0.0), axis=0, keepdims=True)
        slot = start.astype(jnp.int32) + rank_ref[0, r:r + 1, :]
        rows.append(jnp.where(sel >= 0, slot, spare))
    rows.append(jnp.broadcast_to(spare, (8 - MOBA_TOPK, tq)))
    slot_ref[0] = jnp.concatenate(rows, axis=0)


def _moba_slots(sel, rank, base, trash):
    bh, _, seq = sel.shape
    nb = base.shape[1]
    tq = min(SLOTS_TQ, seq)
    spec = pl.BlockSpec((1, 8, tq), lambda n, t: (n, 0, t))
    return pl.pallas_call(
        functools.partial(_moba_slot_kernel, trash=trash),
        name="moba_slots",
        grid=(bh, seq // tq),
        in_specs=[spec, spec, pl.BlockSpec((1, nb, HEAD_DIM), lambda n, t: (n, 0, 0))],
        out_specs=spec,
        out_shape=jax.ShapeDtypeStruct((bh, 8, seq), jnp.int32),
        compiler_params=_cparams(("parallel", "parallel")),
    )(sel, rank, base)


def _sc_mesh():
    return plsc.VectorSubcoreMesh(core_axis_name="core", subcore_axis_name="subcore")


def _sc_scatter_rows(rows, slots, n_out):
    bh8, seq = slots.shape
    bh = bh8 // 8
    nw = seq // SC_WINDOW

    @pl.kernel(out_type=jax.ShapeDtypeStruct((n_out, HEAD_DIM), rows.dtype), mesh=_sc_mesh(),
               scratch_types=[])
    def scatter(x_hbm, i_hbm, o_hbm):
        def body(x_vmem, i_vmem):
            pltpu.sync_copy(x_vmem, o_hbm.at[i_vmem.at[0]])

        pltpu.emit_pipeline(
            body,
            grid=(bh * MOBA_TOPK * nw,),
            in_specs=[
                pl.BlockSpec((SC_WINDOW, HEAD_DIM),
                             index_map=lambda i: ((i // (MOBA_TOPK * nw)) * nw + i % nw, 0)),
                pl.BlockSpec((1, SC_WINDOW),
                             index_map=lambda i: ((i // (MOBA_TOPK * nw)) * 8 + (i // nw) % MOBA_TOPK,
                                                  i % nw)),
            ],
            out_specs=[],
            core_axis_name=("core", "subcore"),
            dimension_semantics=(pltpu.PARALLEL,),
        )(x_hbm, i_hbm)

    return scatter(rows, slots)


def _sc_gather_rows(table, slots):
    bh8, seq = slots.shape
    bh = bh8 // 8
    nw = seq // SC_WINDOW
    n_out = bh * MOBA_TOPK * seq

    @pl.kernel(out_type=jax.ShapeDtypeStruct((n_out, HEAD_DIM), table.dtype), mesh=_sc_mesh())
    def gather(x_hbm, i_hbm, o_hbm):
        def body(i_vmem, o_vmem):
            pltpu.sync_copy(x_hbm.at[i_vmem.at[0]], o_vmem)

        pltpu.emit_pipeline(
            body,
            grid=(bh * MOBA_TOPK * nw,),
            in_specs=[
                pl.BlockSpec((1, SC_WINDOW),
                             index_map=lambda i: ((i // (MOBA_TOPK * nw)) * 8 + (i // nw) % MOBA_TOPK,
                                                  i % nw)),
            ],
            out_specs=[pl.BlockSpec((SC_WINDOW, HEAD_DIM), index_map=lambda i: (i, 0))],
            core_axis_name=("core", "subcore"),
            dimension_semantics=(pltpu.PARALLEL,),
        )(i_hbm, o_hbm)

    return gather(table, slots)


def _moba_routed_kernel(tb_ref, nt_ref, qs_ref, k_ref, vtb_ref, slope_ref, o_ref,
                        sa_ref, sb_ref, sc_ref):
    n = pl.program_id(0)
    g = pl.program_id(1)
    bs = MOBA_BLOCK
    tpg = qs_ref.shape[0] // SLOT_TILE
    tiles_cap = pl.num_programs(1) * tpg

    @pl.when(g * tpg < nt_ref[n])
    def _():
        krow = lax.broadcasted_iota(jnp.int32, (bs, SLOT_TILE), 0).astype(F32)
        bias = slope_ref[0, 0:1, 0:1] * krow
        frow = lax.broadcasted_iota(jnp.int32, (HEAD_DIM, SLOT_TILE), 0)
        blocks = [tb_ref[n * tiles_cap + g * tpg + u] for u in range(tpg)]

        def scores(u):
            qt = qs_ref[u * SLOT_TILE:(u + 1) * SLOT_TILE, :].T.astype(BF16)
            k = k_ref[pl.ds(pl.multiple_of(blocks[u] * bs, bs), bs), :]
            return _dot(k, qt) + bias

        bufs = (sa_ref, sb_ref, sc_ref)
        sa_ref[...] = scores(0)
        sb_ref[...] = scores(1)
        for u in range(tpg):
            if u + 2 < tpg:
                bufs[(u + 2) % 3][...] = scores(u + 2)
            s = bufs[u % 3][...]
            m = jnp.max(s, axis=0, keepdims=True)
            p = jnp.exp(s - m).astype(BF16)
            acc = _dot(vtb_ref[0, blocks[u]], p)
            l = acc[HEAD_DIM:HEAD_DIM + 1]
            o = (acc[0:HEAD_DIM] / l).astype(BF16).astype(F32)
            lse = m + jnp.log(l)
            ob = lax.bitcast_convert_type(o, jnp.uint32)
            lb = lax.bitcast_convert_type(lse, jnp.uint32)
            extra = jnp.where(frow == 0, lb >> 16, jnp.where(frow == 1, lb & 0xFFFF, 0))
            packed = lax.bitcast_convert_type(ob | extra, F32)
            o_ref[u * SLOT_TILE:(u + 1) * SLOT_TILE, :] = packed.T


def _moba_routed(qs, pa, vtb, slopes, tile_block, n_tiles, tiles_cap):
    bh, nb = vtb.shape[0], vtb.shape[1]
    seq = nb * MOBA_BLOCK
    kcol = MOBA_W // HEAD_DIM
    tpg = ROUTED_TILES_PER_STEP
    steps = tiles_cap // tpg
    rows = tpg * SLOT_TILE

    def qmap(n, g, tb, nt):
        used = jnp.maximum((nt[n] + tpg - 1) // tpg, 1)
        return (n * steps + jnp.minimum(g, used - 1), 0)

    grid_spec = pltpu.PrefetchScalarGridSpec(
        num_scalar_prefetch=2,
        grid=(bh, steps),
        in_specs=[
            pl.BlockSpec((rows, HEAD_DIM), qmap),
            pl.BlockSpec((seq, HEAD_DIM),
                         lambda n, g, tb, nt: (n // MOBA_HEADS, kcol + n % MOBA_HEADS)),
            pl.BlockSpec((1, nb, VT_ROWS, MOBA_BLOCK), lambda n, g, tb, nt: (n, 0, 0, 0)),
            pl.BlockSpec((1, 1, HEAD_DIM), lambda n, g, tb, nt: (n, 0, 0)),
        ],
        out_specs=pl.BlockSpec((rows, HEAD_DIM), qmap),
        scratch_shapes=[pltpu.VMEM((MOBA_BLOCK, SLOT_TILE), F32)] * 3,
    )
    return pl.pallas_call(
        _moba_routed_kernel,
        name="moba_routed",
        grid_spec=grid_spec,
        out_shape=jax.ShapeDtypeStruct((qs.shape[0], HEAD_DIM), F32),
        compiler_params=_cparams(("parallel", "arbitrary")),
    )(tile_block, n_tiles, qs, pa, vtb, slopes)


def _moba_merge_kernel(q_ref, k_ref, vtb_ref, og_ref, sel_ref, slope_ref, o_ref,
                       sa_ref, sb_ref, sc_ref):
    t = pl.program_id(1)
    bs = MOBA_BLOCK
    tq = q_ref.shape[0]
    slope = slope_ref[0, 0:1, 0:1]

    qt = q_ref[...].astype(F32).T.astype(BF16)
    lane = lax.broadcasted_iota(jnp.int32, (1, tq), 1)
    tpos = (t * tq + lane).astype(F32)
    lk = lax.broadcasted_iota(jnp.int32, (bs, bs), 0)
    lq = lax.broadcasted_iota(jnp.int32, (bs, bs), 1)
    dist = (lq - lk).astype(F32)
    nq = tq // bs

    def scores(u):
        return _dot(k_ref[u * bs:(u + 1) * bs, :], qt[:, u * bs:(u + 1) * bs])

    sbufs = (sa_ref, sb_ref, sc_ref)
    for u in range(min(2, nq)):
        sbufs[u][...] = scores(u)
    accs, ms = [], []
    for u in range(nq):
        if u + 2 < nq:
            sbufs[(u + 2) % 3][...] = scores(u + 2)
        s = sbufs[u % 3][...]
        s = jnp.where(dist >= 0.0, s - slope * dist, NEG_INF)
        m_u = jnp.max(s, axis=0, keepdims=True)
        p = jnp.exp(s - m_u).astype(BF16)
        accs.append(_dot(vtb_ref[0, u], p))
        ms.append(m_u)
    acc = jnp.concatenate(accs, axis=1)
    l = acc[HEAD_DIM:HEAD_DIM + 1]
    parts = [acc[0:HEAD_DIM] / l]
    lses = [jnp.concatenate(ms, axis=1) + jnp.log(l)]

    for r in range(MOBA_TOPK):
        u = lax.bitcast_convert_type(og_ref[0, r, 0].T, jnp.uint32)
        lb = ((u[0:1] & 0xFFFF) << 16) | (u[1:2] & 0xFFFF)
        lse_r = lax.bitcast_convert_type(lb, F32)
        sel_r = sel_ref[0, r:r + 1, :]
        valid = sel_r >= 0
        lses.append(jnp.where(valid, lse_r + slope * ((sel_r * bs).astype(F32) - tpos), NEG_INF))
        hi_half = lax.bitcast_convert_type((u >> 16) << 16, F32)
        parts.append(jnp.where(valid, hi_half, 0.0))
    mx = functools.reduce(jnp.maximum, lses)
    ws = [jnp.exp(x - mx) for x in lses]
    num = functools.reduce(lambda a, b: a + b, [w * o for w, o in zip(ws, parts)])
    out_t = num / functools.reduce(lambda a, b: a + b, ws)
    o_ref[...] = out_t.T.astype(o_ref.dtype)


def _moba_merge(pa, vtb, og, sel, slopes, batch, seq):
    tq = MOBA_QTILE * MOBA_BLOCK
    nt = seq // tq
    bh = batch * MOBA_HEADS
    kcol = MOBA_W // HEAD_DIM
    og = og.reshape(bh, MOBA_TOPK, nt, tq, HEAD_DIM)

    def qmap(n, t):
        return ((n // MOBA_HEADS) * nt + t, n % MOBA_HEADS)

    def kmap(n, t):
        return ((n // MOBA_HEADS) * nt + t, kcol + n % MOBA_HEADS)

    return pl.pallas_call(
        _moba_merge_kernel,
        name="moba_merge",
        grid=(bh, nt),
        in_specs=[
            pl.BlockSpec((tq, HEAD_DIM), qmap),
            pl.BlockSpec((tq, HEAD_DIM), kmap),
            pl.BlockSpec((1, MOBA_QTILE, VT_ROWS, MOBA_BLOCK), lambda n, t: (n, t, 0, 0)),
            pl.BlockSpec((1, MOBA_TOPK, 1, tq, HEAD_DIM), lambda n, t: (n, 0, t, 0, 0)),
            pl.BlockSpec((1, 8, tq), lambda n, t: (n, 0, t)),
            pl.BlockSpec((1, 1, HEAD_DIM), lambda n, t: (n, 0, 0)),
        ],
        out_specs=pl.BlockSpec((tq, HEAD_DIM), qmap),
        out_shape=jax.ShapeDtypeStruct((batch * seq, MOBA_W), BF16),
        scratch_shapes=[pltpu.VMEM((MOBA_BLOCK, MOBA_BLOCK), F32)] * 3,
        compiler_params=_cparams(("parallel", "parallel")),
    )(pa, pa, vtb, og, sel, slopes)


def _moba_routed_attention(pa, vtb, kmean, slopes, batch, seq, companion):
    nb = seq // MOBA_BLOCK
    bh = batch * MOBA_HEADS
    qrows, sel, rank, cnt = _moba_gate(pa, kmean, batch, seq)

    step_slots = SLOT_TILE * ROUTED_TILES_PER_STEP
    cap = -(-(MOBA_TOPK * seq + nb * SLOT_TILE) // step_slots) * step_slots
    tiles_cap = cap // SLOT_TILE
    counts = cnt[:, :, 0].astype(jnp.int32)
    padded = (counts + SLOT_TILE - 1) // SLOT_TILE * SLOT_TILE
    ends = jnp.cumsum(padded, axis=1)
    base = ends - padded + (jnp.arange(bh, dtype=jnp.int32) * cap)[:, None]
    n_tiles = ends[:, -1] // SLOT_TILE
    tile_idx = jnp.arange(tiles_cap, dtype=jnp.int32)
    tile_block = jnp.sum(ends[:, None, :] // SLOT_TILE <= tile_idx[None, :, None], axis=-1)
    tile_block = jnp.minimum(tile_block, nb - 1).astype(jnp.int32).reshape(bh * tiles_cap)
    base_b = jnp.broadcast_to(base.astype(F32)[:, :, None], (bh, nb, HEAD_DIM))

    n_rows = bh * cap + SLOT_TILE
    slots = _moba_slots(sel, rank, base_b, bh * cap).reshape(bh * 8, seq)
    qs = _sc_scatter_rows(qrows, slots, n_rows)
    qs, companion = lax.optimization_barrier((qs, companion))
    part = _moba_routed(qs, pa, vtb, slopes, tile_block, n_tiles, tiles_cap)
    og = _sc_gather_rows(part, slots)
    return _moba_merge(pa, vtb, og, sel, slopes, batch, seq), companion


def _split3(x):
    hi = x.astype(BF16)
    r1 = x - hi.astype(F32)
    mid = r1.astype(BF16)
    lo = (r1 - mid.astype(F32)).astype(BF16)
    return hi, mid, lo


def _hgrn_kernel(qb_ref, fb_ref, ib_ref, gb_ref, lbl_ref, gn_ref, o_ref, s_ref, oi_ref, st_ref,
                 *, layer):
    tt = qb_ref.shape[0]
    sub = HGRN_SUB
    hd = HEAD_DIM

    @pl.when(pl.program_id(1) == 0)
    def _():
        s_ref[...] = jnp.zeros_like(s_ref)

    logits = lbl_ref[...]
    e = jnp.exp(logits - jnp.max(logits, axis=0, keepdims=True))
    pl_ = e / jnp.sum(e, axis=0, keepdims=True)
    lb = jnp.sum(pl_[0:layer + 1], axis=0, keepdims=True) - pl_[0:1]

    fb = fb_ref[...].astype(F32)
    t = jnp.exp(-jnp.abs(fb))
    r = 1.0 / (1.0 + t)
    sig_pos = jnp.where(fb >= 0, r, t * r)
    sig_neg = jnp.where(fb >= 0, t * r, r)
    f_gate = lb + (1.0 - lb) * sig_pos
    logf = jnp.log(jnp.maximum(f_gate, F_MIN))
    k_all = (1.0 - lb) * sig_neg
    qb = qb_ref[...].astype(F32)
    q_all = qb * jax.nn.sigmoid(qb)
    v_all = ib_ref[...].astype(F32)

    ri = lax.broadcasted_iota(jnp.int32, (tt, tt), 0)
    ci = lax.broadcasted_iota(jnp.int32, (tt, tt), 1)
    same = (ri // sub) == (ci // sub)
    tri = jnp.where(same & (ci <= ri), 1.0, 0.0).astype(BF16)
    hi, mid, lo = _split3(logf)
    gl = (_dot(tri, hi) + _dot(tri, mid) + _dot(tri, lo)) * LOG2E

    half = sub // 2
    nsub = tt // sub
    nv = tt // half
    lane_sum = jnp.ones((hd, hd), BF16)
    row_in = lax.broadcasted_iota(jnp.int32, (nv, half, hd), 1)

    def pair_terms(qx, gx, kx, vx, causal):
        n = qx.shape[0]
        out = jnp.zeros_like(qx)
        for rho in range(half):
            kr = pltpu.roll(kx, rho, 1) if rho else kx
            vr = pltpu.roll(vx, rho, 1) if rho else vx
            gr = pltpu.roll(gx[1], rho, 1) if rho else gx[1]
            prod = qx * kr * jnp.exp2(gx[0] - gr)
            if causal and rho:
                prod = jnp.where(row_in >= rho, prod, 0.0)
            a = _dot(prod.reshape(n * half, hd).astype(BF16), lane_sum)
            out = out + a.reshape(n, half, hd) * vr
        return out

    for h in range(HGRN_HEADS):
        cs = slice(h * hd, (h + 1) * hd)
        q = q_all[:, cs]
        k = k_all[:, cs]
        v = v_all[:, cs]
        g = gl[:, cs]

        q3, k3, v3, g3 = (a.reshape(nv, half, hd) for a in (q, k, v, g))
        od = pair_terms(q3, (g3, g3), k3, v3, True).reshape(nsub, 2, half, hd)
        q4, k4, v4, g4 = (a.reshape(nsub, 2, half, hd) for a in (q, k, v, g))
        oh = pair_terms(q4[:, 1], (g4[:, 1], g4[:, 0]), k4[:, 0], v4[:, 0], False)
        o_diag = jnp.stack([od[:, 0], od[:, 1] + oh], axis=1).reshape(tt, hd)

        gs = g.reshape(nsub, sub, hd)
        g_end = gs[:, sub - 1:sub, :]
        qd = (q * jnp.exp2(g)).astype(BF16).reshape(nsub, sub, hd)
        kd = (k.reshape(nsub, sub, hd) * jnp.exp2(g_end - gs)).astype(BF16)
        vb = v.astype(BF16).reshape(nsub, sub, hd)
        dec = jnp.exp2(g_end)
        upd = [_dot_t0(vb[c], kd[c]) for c in range(nsub)]
        st = s_ref[h]
        for c in range(nsub):
            st_ref[c] = st.astype(BF16)
            st = st * dec[c] + upd[c]
        s_ref[h] = st
        for c in range(nsub):
            oi_ref[c * sub:(c + 1) * sub, cs] = lax.dot_general(
                qd[c], st_ref[c], (((1,), (1,)), ((), ())), preferred_element_type=F32)

        o = oi_ref[:, cs] + o_diag
        rr = lax.rsqrt(jnp.mean(o * o, axis=-1, keepdims=True) + RMS_EPS)
        gate = jax.nn.sigmoid(gb_ref[:, cs].astype(F32))
        o_ref[:, cs] = (o * rr * gn_ref[...] * gate).astype(o_ref.dtype)


def _hgrn(pa, lb_logits, out_norm, layer, batch, seq):
    tt = min(HGRN_TT, seq)
    nt = seq // tt
    c0 = 3 * MOBA_W // HGRN_W
    nl = lb_logits.shape[0]

    def cmap(off):
        return lambda b, t: (b * nt + t, c0 + off)

    return pl.pallas_call(
        functools.partial(_hgrn_kernel, layer=layer),
        name="hgrn",
        grid=(batch, nt),
        in_specs=[
            pl.BlockSpec((tt, HGRN_W), cmap(0)),
            pl.BlockSpec((tt, HGRN_W), cmap(1)),
            pl.BlockSpec((tt, HGRN_W), cmap(2)),
            pl.BlockSpec((tt, HGRN_W), cmap(3)),
            pl.BlockSpec((nl, HGRN_W), lambda b, t: (0, 0)),
            pl.BlockSpec((1, HEAD_DIM), lambda b, t: (0, 0)),
        ],
        out_specs=pl.BlockSpec((tt, HGRN_W), lambda b, t: (b * nt + t, 0)),
        out_shape=jax.ShapeDtypeStruct((batch * seq, HGRN_W), BF16),
        scratch_shapes=[pltpu.VMEM((HGRN_HEADS, HEAD_DIM, HEAD_DIM), F32),
                        pltpu.VMEM((tt, HGRN_W), F32),
                        pltpu.VMEM((tt // HGRN_SUB, HEAD_DIM, HEAD_DIM), BF16)],
        compiler_params=_cparams(("parallel", "arbitrary")),
    )(pa, pa, pa, pa, lb_logits, out_norm.reshape(1, HEAD_DIM))


def _memkv_kernel(mem_ref, g_ref, w_ref, kt_ref, v_ref):
    h = _rms(mem_ref[0], g_ref[...]).astype(BF16)
    kv = _dot(h, w_ref[...])
    kt_ref[0] = kv[:, 0:MEM_W].T.astype(BF16)
    v_ref[0] = kv[:, MEM_W:].astype(BF16)


def _memkv(mem, g, w):
    b, m, d = mem.shape
    return pl.pallas_call(
        _memkv_kernel,
        name="memkv",
        grid=(b,),
        in_specs=[
            pl.BlockSpec((1, m, d), lambda i: (i, 0, 0)),
            pl.BlockSpec((1, d), lambda i: (0, 0)),
            pl.BlockSpec((d, 2 * MEM_W), lambda i: (0, 0)),
        ],
        out_specs=[
            pl.BlockSpec((1, MEM_W, m), lambda i: (i, 0, 0)),
            pl.BlockSpec((1, m, MEM_W), lambda i: (i, 0, 0)),
        ],
        out_shape=[
            jax.ShapeDtypeStruct((b, MEM_W, m), BF16),
            jax.ShapeDtypeStruct((b, m, MEM_W), BF16),
        ],
        compiler_params=_cparams(("parallel",)),
    )(mem, g.reshape(1, d), w)


def _merge_kernel(x_ref, oa_ref, ob_ref, qm_ref, g0_ref, g1_ref, g2_ref, mkt_ref, mv_ref,
                  wa_ref, wb_ref, wm_ref, wo_ref, o_ref, om_ref):
    scale = HEAD_DIM ** -0.5
    for h in range(MEM_HEADS):
        cs = slice(h * HEAD_DIM, (h + 1) * HEAD_DIM)
        s = _dot(qm_ref[:, cs], mkt_ref[0, cs, :]) * scale
        p = jnp.exp(s - jnp.max(s, axis=-1, keepdims=True))
        l = jnp.sum(p, axis=-1, keepdims=True)
        om_ref[:, cs] = (_dot(p.astype(BF16), mv_ref[0, :, cs]) / l).astype(BF16)

    y = jax.nn.sigmoid(g0_ref[...].astype(F32)) * _dot(oa_ref[...], wa_ref[...])
    y += jax.nn.sigmoid(g1_ref[...].astype(F32)) * _dot(ob_ref[...], wb_ref[...])
    y += jax.nn.sigmoid(g2_ref[...].astype(F32)) * _dot(om_ref[...], wm_ref[...])
    o_ref[...] = x_ref[...] + _dot(y.astype(BF16), wo_ref[...])


def _merge(x, oa, ob, pa, gates, mkt, mv, wa, wb, wm, wo, batch, seq):
    n, d = x.shape
    m = mv.shape[1]
    tm = _pick(seq, MERGE_TM)
    per_b = seq // tm
    qcol = (3 * MOBA_W + 4 * HGRN_W) // MEM_W
    row = lambda i: (i, 0)
    return pl.pallas_call(
        _merge_kernel,
        name="merge",
        grid=(n // tm,),
        in_specs=[
            pl.BlockSpec((tm, d), row),
            pl.BlockSpec((tm, MOBA_W), row),
            pl.BlockSpec((tm, HGRN_W), row),
            pl.BlockSpec((tm, MEM_W), lambda i: (i, qcol)),
            pl.BlockSpec((tm, d), lambda i: (i, 0)),
            pl.BlockSpec((tm, d), lambda i: (i, 1)),
            pl.BlockSpec((tm, d), lambda i: (i, 2)),
            pl.BlockSpec((1, MEM_W, m), lambda i: (i // per_b, 0, 0)),
            pl.BlockSpec((1, m, MEM_W), lambda i: (i // per_b, 0, 0)),
            _const_spec((MOBA_W, d)),
            _const_spec((HGRN_W, d)),
            _const_spec((MEM_W, d)),
            _const_spec((d, d)),
        ],
        out_specs=pl.BlockSpec((tm, d), row),
        out_shape=jax.ShapeDtypeStruct((n, d), F32),
        scratch_shapes=[pltpu.VMEM((tm, MEM_W), BF16)],
        compiler_params=_cparams(("parallel",)),
    )(x, oa, ob, pa, gates, gates, gates, mkt, mv, wa, wb, wm, wo)


def kernel(x, mem, ffn1_norm, ffn1_w1, ffn1_w3, ffn1_w2, mix_norm, w_in, hgrn_lb_logits,
           hgrn_out_norm, mem_norm, w_mem_kv, w_proj_moba, w_proj_hgrn, w_proj_mem, w_out,
           ffn2_norm, ffn2_w1, ffn2_w3, ffn2_w2, final_norm):
    batch, seq, d = x.shape
    depth = ffn1_w1.shape[0]
    assert seq % (MOBA_QTILE * MOBA_BLOCK) == 0
    assert w_in.shape[-1] == MIX_W + 3 * d
    hs = jnp.arange(1, MOBA_HEADS + 1, dtype=F32)
    slopes = jnp.tile(jnp.exp2(-8.0 * hs / MOBA_HEADS), batch)
    slopes = jnp.broadcast_to(slopes[:, None, None], (batch * MOBA_HEADS, 1, HEAD_DIM))
    mix_scale = jnp.concatenate([jnp.full((MOBA_W,), HEAD_DIM ** -0.5, F32),
                                 jnp.ones((MIX_W - MOBA_W,), F32)])
    gate_scale = jnp.ones((3 * d,), F32)

    xs = x.reshape(batch * seq, d)
    for l in range(depth):
        last = l == depth - 1
        bf = functools.partial(_cast_layer, layer=l)
        xs = _ffn(xs, ffn1_norm[l], bf(ffn1_w1), bf(ffn1_w3), bf(ffn1_w2), final_norm, final=False)

        w_mix, w_gates = bf(w_in, splits=(MIX_W, 3 * d))
        pa = _normproj(xs, mix_norm[l], w_mix, mix_scale)
        gates = _normproj(xs, mix_norm[l], w_gates, gate_scale)

        vtb, kmean = _moba_prep(pa, batch, seq)
        ob = _hgrn(pa, hgrn_lb_logits, hgrn_out_norm[l], l, batch, seq)
        oa, ob = _moba_routed_attention(pa, vtb, kmean, slopes, batch, seq, ob)
        mkt, mv = _memkv(mem, mem_norm[l], bf(w_mem_kv))
        xs = _merge(xs, oa, ob, pa, gates, mkt, mv, bf(w_proj_moba), bf(w_proj_hgrn),
                    bf(w_proj_mem), bf(w_out), batch, seq)

        xs = _ffn(xs, ffn2_norm[l], bf(ffn2_w1), bf(ffn2_w3), bf(ffn2_w2), final_norm, final=last)
    return xs.reshape(batch, seq, d)
```

```python
import functools

import jax
import jax.numpy as jnp
from jax import lax
from jax.experimental import pallas as pl
from jax.experimental.pallas import tpu as pltpu
from jax.experimental.pallas import tpu_sc as plsc

F32 = jnp.float32
BF16 = jnp.bfloat16

HEAD_DIM = 128
MOBA_HEADS = 8
MOBA_BLOCK = 256
MOBA_TOPK = 3
HGRN_HEADS = 4
MEM_HEADS = 4
RMS_EPS = 1e-6
NEG_INF = -1e30
F_MIN = 1e-20
LOG2E = 1.4426950408889634

MOBA_W = MOBA_HEADS * HEAD_DIM
HGRN_W = HGRN_HEADS * HEAD_DIM
MEM_W = MEM_HEADS * HEAD_DIM
MIX_W = 3 * MOBA_W + 4 * HGRN_W + MEM_W

VMEM_LIMIT_BYTES = 60 * 1024 * 1024

LANE = 128
BF16_ROWS = 16
VT_ROWS = HEAD_DIM + BF16_ROWS
MOBA_QTILE = 4

HGRN_SUB = 16
HGRN_TT = 256
FFN_TM = 1024
FFN_TF = 512
FFN_OUT_CHUNK = 512
PROJ_TM = 512
PROJ_TN = 512
MERGE_TM = 512
SLOTS_TQ = 2048
CAST_BLOCK_BYTES = 8 * 1024 * 1024
SLOT_TILE = 512
ROUTED_TILES_PER_STEP = 16
SC_WINDOW = 128


def _cparams(sem):
    return pltpu.CompilerParams(dimension_semantics=sem, vmem_limit_bytes=VMEM_LIMIT_BYTES)


def _dot(a, b):
    return jnp.dot(a, b, preferred_element_type=F32)


def _dot_t0(a, b):
    return lax.dot_general(a, b, (((0,), (0,)), ((), ())), preferred_element_type=F32)


def _rms(xf, g):
    r = lax.rsqrt(jnp.mean(xf * xf, axis=-1, keepdims=True) + RMS_EPS)
    return xf * r * g


def _pick(n, want):
    if n <= want:
        return n
    t = (want // LANE) * LANE
    while t >= LANE:
        if n % t == 0:
            return t
        t -= LANE
    return n


def _const_spec(shape):
    nd = len(shape)
    return pl.BlockSpec(shape, lambda *_: (0,) * nd, pipeline_mode=pl.Buffered(1))


def _cast_kernel(w_ref, *o_refs, splits):
    lo = 0
    for o_ref, width in zip(o_refs, splits):
        o_ref[...] = w_ref[0, :, lo:lo + width].astype(BF16)
        lo += width


def _cast_layer(w_stack, layer, splits=None):
    _, rows, cols = w_stack.shape
    splits = (cols,) if splits is None else tuple(splits)
    assert sum(splits) == cols
    tr = rows
    while tr * cols * 4 > CAST_BLOCK_BYTES and tr % (2 * BF16_ROWS) == 0:
        tr //= 2
    outs = pl.pallas_call(
        functools.partial(_cast_kernel, splits=splits),
        name="cast",
        grid=(rows // tr,),
        in_specs=[pl.BlockSpec((1, tr, cols), lambda i: (layer, i, 0))],
        out_specs=[pl.BlockSpec((tr, w), lambda i: (i, 0)) for w in splits],
        out_shape=[jax.ShapeDtypeStruct((rows, w), BF16) for w in splits],
        compiler_params=_cparams(("parallel",)),
    )(w_stack)
    return outs[0] if len(splits) == 1 else outs


def _ffn_kernel(x_ref, g_ref, w1_ref, w3_ref, w2_ref, fg_ref, o_ref, h_ref, *, final):
    j = pl.program_id(1)

    @pl.when(j == 0)
    def _():
        xf = x_ref[...]
        h_ref[...] = _rms(xf, g_ref[...]).astype(BF16)
        o_ref[...] = xf

    h = h_ref[...]
    u = _dot(h, w1_ref[...])
    v = _dot(h, w3_ref[...])
    a = (0.5 * u * jax.nn.sigmoid(u) * v).astype(BF16)
    tn = FFN_OUT_CHUNK if o_ref.shape[1] % FFN_OUT_CHUNK == 0 else o_ref.shape[1]
    for c in range(o_ref.shape[1] // tn):
        cs = slice(c * tn, (c + 1) * tn)
        o_ref[:, cs] += _dot(a, w2_ref[:, cs])

    if final:
        @pl.when(j == pl.num_programs(1) - 1)
        def _():
            o_ref[...] = _rms(o_ref[...], fg_ref[...])


def _ffn(x, g, w1, w3, w2, final_g, *, final):
    n, d = x.shape
    dff = w1.shape[1]
    tm = _pick(n, FFN_TM)
    tf = _pick(dff, FFN_TF)
    return pl.pallas_call(
        functools.partial(_ffn_kernel, final=final),
        name="ffn_final" if final else "ffn",
        grid=(n // tm, dff // tf),
        in_specs=[
            pl.BlockSpec((tm, d), lambda i, j: (i, 0)),
            pl.BlockSpec((1, d), lambda i, j: (0, 0)),
            pl.BlockSpec((d, tf), lambda i, j: (0, j)),
            pl.BlockSpec((d, tf), lambda i, j: (0, j)),
            pl.BlockSpec((tf, d), lambda i, j: (j, 0)),
            pl.BlockSpec((1, d), lambda i, j: (0, 0)),
        ],
        out_specs=pl.BlockSpec((tm, d), lambda i, j: (i, 0)),
        out_shape=jax.ShapeDtypeStruct((n, d), F32),
        scratch_shapes=[pltpu.VMEM((tm, d), BF16)],
        compiler_params=_cparams(("parallel", "arbitrary")),
    )(x, g.reshape(1, d), w1, w3, w2, final_g.reshape(1, d))


def _normproj_kernel(x_ref, g_ref, w_ref, s_ref, o_ref, h_ref, *, tn):
    h_ref[...] = _rms(x_ref[...], g_ref[...]).astype(BF16)
    for c in range(w_ref.shape[1] // tn):
        cs = slice(c * tn, (c + 1) * tn)
        o_ref[:, cs] = (_dot(h_ref[...], w_ref[:, cs]) * s_ref[:, cs]).astype(o_ref.dtype)


def _normproj(x, g, w, col_scale):
    n, d = x.shape
    nout = w.shape[1]
    tm = _pick(n, PROJ_TM)
    tn = _pick(nout, PROJ_TN)
    return pl.pallas_call(
        functools.partial(_normproj_kernel, tn=tn),
        name="normproj",
        grid=(n // tm,),
        in_specs=[
            pl.BlockSpec((tm, d), lambda i: (i, 0)),
            pl.BlockSpec((1, d), lambda i: (0, 0)),
            pl.BlockSpec((d, nout), lambda i: (0, 0), pipeline_mode=pl.Buffered(1)),
            pl.BlockSpec((1, nout), lambda i: (0, 0)),
        ],
        out_specs=pl.BlockSpec((tm, nout), lambda i: (i, 0)),
        out_shape=jax.ShapeDtypeStruct((n, nout), BF16),
        scratch_shapes=[pltpu.VMEM((tm, d), BF16)],
        compiler_params=_cparams(("parallel",)),
    )(x, g.reshape(1, d), w, col_scale.reshape(1, nout))


def _moba_prep_kernel(k_ref, v_ref, vtb_ref, kmean_ref):
    g = pl.program_id(1)
    bs = MOBA_BLOCK
    nblk = k_ref.shape[0] // bs
    orow = lax.broadcasted_iota(jnp.int32, (VT_ROWS - HEAD_DIM, bs), 0)
    ones_row = jnp.where(orow == 0, 1.0, 0.0).astype(BF16)
    for u in range(nblk):
        k = k_ref[u * bs:(u + 1) * bs, :]
        kmean_ref[0, pl.ds(g * nblk + u, 1), :] = jnp.mean(k.astype(F32), axis=0, keepdims=True)
        vt = v_ref[u * bs:(u + 1) * bs, :].astype(F32).T
        vtb_ref[0, u, 0:HEAD_DIM, :] = vt.astype(BF16)
        vtb_ref[0, u, HEAD_DIM:VT_ROWS, :] = ones_row


def _moba_prep(pa, batch, seq):
    nb = seq // MOBA_BLOCK
    grp = MOBA_QTILE
    ng = nb // grp
    bh = batch * MOBA_HEADS
    kcol = MOBA_W // HEAD_DIM
    vcol = 2 * MOBA_W // HEAD_DIM

    def kmap(n, g):
        return ((n // MOBA_HEADS) * ng + g, kcol + n % MOBA_HEADS)

    def vmap(n, g):
        return ((n // MOBA_HEADS) * ng + g, vcol + n % MOBA_HEADS)

    return pl.pallas_call(
        _moba_prep_kernel,
        name="moba_prep",
        grid=(bh, ng),
        in_specs=[
            pl.BlockSpec((grp * MOBA_BLOCK, HEAD_DIM), kmap),
            pl.BlockSpec((grp * MOBA_BLOCK, HEAD_DIM), vmap),
        ],
        out_specs=[
            pl.BlockSpec((1, grp, VT_ROWS, MOBA_BLOCK), lambda n, g: (n, g, 0, 0)),
            pl.BlockSpec((1, nb, HEAD_DIM), lambda n, g: (n, 0, 0)),
        ],
        out_shape=[
            jax.ShapeDtypeStruct((bh, nb, VT_ROWS, MOBA_BLOCK), BF16),
            jax.ShapeDtypeStruct((bh, nb, HEAD_DIM), F32),
        ],
        compiler_params=_cparams(("parallel", "arbitrary")),
    )(pa, pa)


def _moba_gate_kernel(q_ref, kmean_ref, qrow_ref, sel_ref, rank_ref, cnt_ref):
    t = pl.program_id(1)
    bs = MOBA_BLOCK
    tq = q_ref.shape[0]
    nb = kmean_ref.shape[1]

    q = q_ref[...]
    qrow_ref[...] = q.astype(F32)
    qt = q.astype(F32).T.astype(BF16)

    lane = lax.broadcasted_iota(jnp.int32, (1, tq), 1)
    own = t * (tq // bs) + lane // bs
    km = kmean_ref[0]
    km_hi = km.astype(BF16)
    km_lo = (km - km_hi.astype(F32)).astype(BF16)
    gate = _dot(km_hi, qt) + _dot(km_lo, qt)
    blk = lax.broadcasted_iota(jnp.int32, (nb, tq), 0)
    gate = jnp.where(blk < own, gate, NEG_INF)

    @pl.when(t == 0)
    def _():
        cnt_ref[...] = jnp.zeros_like(cnt_ref)

    run = cnt_ref[0][:, 0:1]
    qi = lax.broadcasted_iota(jnp.int32, (bs, bs), 0)
    qj = lax.broadcasted_iota(jnp.int32, (bs, bs), 1)
    before = jnp.where(qi < qj, 1.0, 0.0).astype(BF16)
    sels, ranks = [], []
    for r in range(MOBA_TOPK):
        mx = jnp.max(gate, axis=0, keepdims=True)
        first = jnp.min(jnp.where(gate == mx, blk, nb), axis=0, keepdims=True)
        hit = blk == first
        gate = jnp.where(hit, -jnp.inf, gate)
        valid = own > r
        oh = jnp.where(hit & valid, 1.0, 0.0)
        parts = []
        for c in range(tq // bs):
            ohc = oh[:, c * bs:(c + 1) * bs]
            prior = _dot(ohc.astype(BF16), before)
            parts.append(jnp.sum(ohc * (run + prior), axis=0, keepdims=True))
            run = run + jnp.sum(ohc, axis=1, keepdims=True)
        ranks.append(jnp.concatenate(parts, axis=1))
        sels.append(jnp.where(valid, first, -1))
    cnt_ref[0] = jnp.broadcast_to(run, cnt_ref.shape[1:])
    pad = jnp.zeros((8 - MOBA_TOPK, tq), jnp.int32)
    sel_ref[0] = jnp.concatenate(sels + [pad], axis=0)
    rank_ref[0] = jnp.concatenate([x.astype(jnp.int32) for x in ranks] + [pad], axis=0)


def _moba_gate(pa, kmean, batch, seq):
    nb = seq // MOBA_BLOCK
    tq = min(MOBA_QTILE * MOBA_BLOCK, seq)
    nt = seq // tq
    bh = batch * MOBA_HEADS

    def qmap(n, t):
        return ((n // MOBA_HEADS) * nt + t, n % MOBA_HEADS)

    return pl.pallas_call(
        _moba_gate_kernel,
        name="moba_gate",
        grid=(bh, nt),
        in_specs=[
            pl.BlockSpec((tq, HEAD_DIM), qmap),
            pl.BlockSpec((1, nb, HEAD_DIM), lambda n, t: (n, 0, 0)),
        ],
        out_specs=[
            pl.BlockSpec((tq, HEAD_DIM), lambda n, t: (n * nt + t, 0)),
            pl.BlockSpec((1, 8, tq), lambda n, t: (n, 0, t)),
            pl.BlockSpec((1, 8, tq), lambda n, t: (n, 0, t)),
            pl.BlockSpec((1, nb, HEAD_DIM), lambda n, t: (n, 0, 0)),
        ],
        out_shape=[
            jax.ShapeDtypeStruct((bh * seq, HEAD_DIM), F32),
            jax.ShapeDtypeStruct((bh, 8, seq), jnp.int32),
            jax.ShapeDtypeStruct((bh, 8, seq), jnp.int32),
            jax.ShapeDtypeStruct((bh, nb, HEAD_DIM), F32),
        ],
        compiler_params=_cparams(("parallel", "arbitrary")),
    )(pa, kmean)


def _moba_slot_kernel(sel_ref, rank_ref, base_ref, slot_ref, *, trash):
    nb = base_ref.shape[1]
    tq = sel_ref.shape[2]
    base = base_ref[0][:, 0:1]
    blk = lax.broadcasted_iota(jnp.int32, (nb, tq), 0)
    spare = trash + lax.broadcasted_iota(jnp.int32, (1, tq), 1) % SLOT_TILE
    rows = []
    for r in range(MOBA_TOPK):
        sel = sel_ref[0, r:r + 1, :]
        start = jnp.sum(jnp.where(blk == sel, base, 0.0), axis=0, keepdims=True)
        slot = start.astype(jnp.int32) + rank_ref[0, r:r + 1, :]
        rows.append(jnp.where(sel >= 0, slot, spare))
    rows.append(jnp.broadcast_to(spare, (8 - MOBA_TOPK, tq)))
    slot_ref[0] = jnp.concatenate(rows, axis=0)


def _moba_slots(sel, rank, base, trash):
    bh, _, seq = sel.shape
    nb = base.shape[1]
    tq = min(SLOTS_TQ, seq)
    spec = pl.BlockSpec((1, 8, tq), lambda n, t: (n, 0, t))
    return pl.pallas_call(
        functools.partial(_moba_slot_kernel, trash=trash),
        name="moba_slots",
        grid=(bh, seq // tq),
        in_specs=[spec, spec, pl.BlockSpec((1, nb, HEAD_DIM), lambda n, t: (n, 0, 0))],
        out_specs=spec,
        out_shape=jax.ShapeDtypeStruct((bh, 8, seq), jnp.int32),
        compiler_params=_cparams(("parallel", "parallel")),
    )(sel, rank, base)


def _sc_mesh():
    return plsc.VectorSubcoreMesh(core_axis_name="core", subcore_axis_name="subcore")


def _sc_scatter_rows(rows, slots, n_out):
    bh8, seq = slots.shape
    bh = bh8 // 8
    nw = seq // SC_WINDOW

    @pl.kernel(out_type=jax.ShapeDtypeStruct((n_out, HEAD_DIM), rows.dtype), mesh=_sc_mesh(),
               scratch_types=[])
    def scatter(x_hbm, i_hbm, o_hbm):
        def body(x_vmem, i_vmem):
            pltpu.sync_copy(x_vmem, o_hbm.at[i_vmem.at[0]])

        pltpu.emit_pipeline(
            body,
            grid=(bh * MOBA_TOPK * nw,),
            in_specs=[
                pl.BlockSpec((SC_WINDOW, HEAD_DIM),
                             index_map=lambda i: ((i // (MOBA_TOPK * nw)) * nw + i % nw, 0)),
                pl.BlockSpec((1, SC_WINDOW),
                             index_map=lambda i: ((i // (MOBA_TOPK * nw)) * 8 + (i // nw) % MOBA_TOPK,
                                                  i % nw)),
            ],
            out_specs=[],
            core_axis_name=("core", "subcore"),
            dimension_semantics=(pltpu.PARALLEL,),
        )(x_hbm, i_hbm)

    return scatter(rows, slots)


def _sc_gather_rows(table, slots):
    bh8, seq = slots.shape
    bh = bh8 // 8
    nw = seq // SC_WINDOW
    n_out = bh * MOBA_TOPK * seq

    @pl.kernel(out_type=jax.ShapeDtypeStruct((n_out, HEAD_DIM), table.dtype), mesh=_sc_mesh())
    def gather(x_hbm, i_hbm, o_hbm):
        def body(i_vmem, o_vmem):
            pltpu.sync_copy(x_hbm.at[i_vmem.at[0]], o_vmem)

        pltpu.emit_pipeline(
            body,
            grid=(bh * MOBA_TOPK * nw,),
            in_specs=[
                pl.BlockSpec((1, SC_WINDOW),
                             index_map=lambda i: ((i // (MOBA_TOPK * nw)) * 8 + (i // nw) % MOBA_TOPK,
                                                  i % nw)),
            ],
            out_specs=[pl.BlockSpec((SC_WINDOW, HEAD_DIM), index_map=lambda i: (i, 0))],
            core_axis_name=("core", "subcore"),
            dimension_semantics=(pltpu.PARALLEL,),
        )(i_hbm, o_hbm)

    return gather(table, slots)


def _moba_routed_kernel(tb_ref, nt_ref, qs_ref, k_ref, vtb_ref, slope_ref, o_ref,
                        sa_ref, sb_ref, sc_ref):
    n = pl.program_id(0)
    g = pl.program_id(1)
    bs = MOBA_BLOCK
    tpg = qs_ref.shape[0] // SLOT_TILE
    tiles_cap = pl.num_programs(1) * tpg

    @pl.when(g * tpg < nt_ref[n])
    def _():
        krow = lax.broadcasted_iota(jnp.int32, (bs, SLOT_TILE), 0).astype(F32)
        bias = slope_ref[0, 0:1, 0:1] * krow
        frow = lax.broadcasted_iota(jnp.int32, (HEAD_DIM, SLOT_TILE), 0)
        blocks = [tb_ref[n * tiles_cap + g * tpg + u] for u in range(tpg)]

        def scores(u):
            qt = qs_ref[u * SLOT_TILE:(u + 1) * SLOT_TILE, :].T.astype(BF16)
            k = k_ref[pl.ds(pl.multiple_of(blocks[u] * bs, bs), bs), :]
            return _dot(k, qt) + bias

        bufs = (sa_ref, sb_ref, sc_ref)
        sa_ref[...] = scores(0)
        sb_ref[...] = scores(1)
        for u in range(tpg):
            if u + 2 < tpg:
                bufs[(u + 2) % 3][...] = scores(u + 2)
            s = bufs[u % 3][...]
            m = jnp.max(s, axis=0, keepdims=True)
            p = jnp.exp(s - m).astype(BF16)
            acc = _dot(vtb_ref[0, blocks[u]], p)
            l = acc[HEAD_DIM:HEAD_DIM + 1]
            o = (acc[0:HEAD_DIM] / l).astype(BF16).astype(F32)
            lse = m + jnp.log(l)
            ob = lax.bitcast_convert_type(o, jnp.uint32)
            lb = lax.bitcast_convert_type(lse, jnp.uint32)
            extra = jnp.where(frow == 0, lb >> 16, jnp.where(frow == 1, lb & 0xFFFF, 0))
            packed = lax.bitcast_convert_type(ob | extra, F32)
            o_ref[u * SLOT_TILE:(u + 1) * SLOT_TILE, :] = packed.T


def _moba_routed(qs, pa, vtb, slopes, tile_block, n_tiles, tiles_cap):
    bh, nb = vtb.shape[0], vtb.shape[1]
    seq = nb * MOBA_BLOCK
    kcol = MOBA_W // HEAD_DIM
    tpg = ROUTED_TILES_PER_STEP
    steps = tiles_cap // tpg
    rows = tpg * SLOT_TILE

    def qmap(n, g, tb, nt):
        used = jnp.maximum((nt[n] + tpg - 1) // tpg, 1)
        return (n * steps + jnp.minimum(g, used - 1), 0)

    grid_spec = pltpu.PrefetchScalarGridSpec(
        num_scalar_prefetch=2,
        grid=(bh, steps),
        in_specs=[
            pl.BlockSpec((rows, HEAD_DIM), qmap),
            pl.BlockSpec((seq, HEAD_DIM),
                         lambda n, g, tb, nt: (n // MOBA_HEADS, kcol + n % MOBA_HEADS)),
            pl.BlockSpec((1, nb, VT_ROWS, MOBA_BLOCK), lambda n, g, tb, nt: (n, 0, 0, 0)),
            pl.BlockSpec((1, 1, HEAD_DIM), lambda n, g, tb, nt: (n, 0, 0)),
        ],
        out_specs=pl.BlockSpec((rows, HEAD_DIM), qmap),
        scratch_shapes=[pltpu.VMEM((MOBA_BLOCK, SLOT_TILE), F32)] * 3,
    )
    return pl.pallas_call(
        _moba_routed_kernel,
        name="moba_routed",
        grid_spec=grid_spec,
        out_shape=jax.ShapeDtypeStruct((qs.shape[0], HEAD_DIM), F32),
        compiler_params=_cparams(("parallel", "arbitrary")),
    )(tile_block, n_tiles, qs, pa, vtb, slopes)


def _moba_merge_kernel(q_ref, k_ref, vtb_ref, og_ref, sel_ref, slope_ref, o_ref,
                       sa_ref, sb_ref, sc_ref):
    t = pl.program_id(1)
    bs = MOBA_BLOCK
    tq = q_ref.shape[0]
    slope = slope_ref[0, 0:1, 0:1]

    qt = q_ref[...].astype(F32).T.astype(BF16)
    lane = lax.broadcasted_iota(jnp.int32, (1, tq), 1)
    tpos = (t * tq + lane).astype(F32)
    lk = lax.broadcasted_iota(jnp.int32, (bs, bs), 0)
    lq = lax.broadcasted_iota(jnp.int32, (bs, bs), 1)
    dist = (lq - lk).astype(F32)
    nq = tq // bs

    def scores(u):
        return _dot(k_ref[u * bs:(u + 1) * bs, :], qt[:, u * bs:(u + 1) * bs])

    sbufs = (sa_ref, sb_ref, sc_ref)
    for u in range(min(2, nq)):
        sbufs[u][...] = scores(u)
    accs, ms = [], []
    for u in range(nq):
        if u + 2 < nq:
            sbufs[(u + 2) % 3][...] = scores(u + 2)
        s = sbufs[u % 3][...]
        s = jnp.where(dist >= 0.0, s - slope * dist, NEG_INF)
        m_u = jnp.max(s, axis=0, keepdims=True)
        p = jnp.exp(s - m_u).astype(BF16)
        accs.append(_dot(vtb_ref[0, u], p))
        ms.append(m_u)
    acc = jnp.concatenate(accs, axis=1)
    l = acc[HEAD_DIM:HEAD_DIM + 1]
    parts = [acc[0:HEAD_DIM] / l]
    lses = [jnp.concatenate(ms, axis=1) + jnp.log(l)]

    for r in range(MOBA_TOPK):
        u = lax.bitcast_convert_type(og_ref[0, r, 0].T, jnp.uint32)
        lb = ((u[0:1] & 0xFFFF) << 16) | (u[1:2] & 0xFFFF)
        lse_r = lax.bitcast_convert_type(lb, F32)
        sel_r = sel_ref[0, r:r + 1, :]
        valid = sel_r >= 0
        lses.append(jnp.where(valid, lse_r + slope * ((sel_r * bs).astype(F32) - tpos), NEG_INF))
        hi_half = lax.bitcast_convert_type((u >> 16) << 16, F32)
        parts.append(jnp.where(valid, hi_half, 0.0))
    mx = functools.reduce(jnp.maximum, lses)
    ws = [jnp.exp(x - mx) for x in lses]
    num = functools.reduce(lambda a, b: a + b, [w * o for w, o in zip(ws, parts)])
    out_t = num / functools.reduce(lambda a, b: a + b, ws)
    o_ref[...] = out_t.T.astype(o_ref.dtype)


def _moba_merge(pa, vtb, og, sel, slopes, batch, seq):
    tq = MOBA_QTILE * MOBA_BLOCK
    nt = seq // tq
    bh = batch * MOBA_HEADS
    kcol = MOBA_W // HEAD_DIM
    og = og.reshape(bh, MOBA_TOPK, nt, tq, HEAD_DIM)

    def qmap(n, t):
        return ((n // MOBA_HEADS) * nt + t, n % MOBA_HEADS)

    def kmap(n, t):
        return ((n // MOBA_HEADS) * nt + t, kcol + n % MOBA_HEADS)

    return pl.pallas_call(
        _moba_merge_kernel,
        name="moba_merge",
        grid=(bh, nt),
        in_specs=[
            pl.BlockSpec((tq, HEAD_DIM), qmap),
            pl.BlockSpec((tq, HEAD_DIM), kmap),
            pl.BlockSpec((1, MOBA_QTILE, VT_ROWS, MOBA_BLOCK), lambda n, t: (n, t, 0, 0)),
            pl.BlockSpec((1, MOBA_TOPK, 1, tq, HEAD_DIM), lambda n, t: (n, 0, t, 0, 0)),
            pl.BlockSpec((1, 8, tq), lambda n, t: (n, 0, t)),
            pl.BlockSpec((1, 1, HEAD_DIM), lambda n, t: (n, 0, 0)),
        ],
        out_specs=pl.BlockSpec((tq, HEAD_DIM), qmap),
        out_shape=jax.ShapeDtypeStruct((batch * seq, MOBA_W), BF16),
        scratch_shapes=[pltpu.VMEM((MOBA_BLOCK, MOBA_BLOCK), F32)] * 3,
        compiler_params=_cparams(("parallel", "parallel")),
    )(pa, pa, vtb, og, sel, slopes)


def _moba_routed_attention(pa, vtb, kmean, slopes, batch, seq, companion):
    nb = seq // MOBA_BLOCK
    bh = batch * MOBA_HEADS
    qrows, sel, rank, cnt = _moba_gate(pa, kmean, batch, seq)

    step_slots = SLOT_TILE * ROUTED_TILES_PER_STEP
    cap = -(-(MOBA_TOPK * seq + nb * SLOT_TILE) // step_slots) * step_slots
    tiles_cap = cap // SLOT_TILE
    counts = cnt[:, :, 0].astype(jnp.int32)
    padded = (counts + SLOT_TILE - 1) // SLOT_TILE * SLOT_TILE
    ends = jnp.cumsum(padded, axis=1)
    base = ends - padded + (jnp.arange(bh, dtype=jnp.int32) * cap)[:, None]
    n_tiles = ends[:, -1] // SLOT_TILE
    tile_idx = jnp.arange(tiles_cap, dtype=jnp.int32)
    tile_block = jnp.sum(ends[:, None, :] // SLOT_TILE <= tile_idx[None, :, None], axis=-1)
    tile_block = jnp.minimum(tile_block, nb - 1).astype(jnp.int32).reshape(bh * tiles_cap)
    base_b = jnp.broadcast_to(base.astype(F32)[:, :, None], (bh, nb, HEAD_DIM))

    n_rows = bh * cap + SLOT_TILE
    slots = _moba_slots(sel, rank, base_b, bh * cap).reshape(bh * 8, seq)
    qs = _sc_scatter_rows(qrows, slots, n_rows)
    qs, companion = lax.optimization_barrier((qs, companion))
    part = _moba_routed(qs, pa, vtb, slopes, tile_block, n_tiles, tiles_cap)
    og = _sc_gather_rows(part, slots)
    return _moba_merge(pa, vtb, og, sel, slopes, batch, seq), companion


def _split3(x):
    hi = x.astype(BF16)
    r1 = x - hi.astype(F32)
    mid = r1.astype(BF16)
    lo = (r1 - mid.astype(F32)).astype(BF16)
    return hi, mid, lo


def _hgrn_kernel(qb_ref, fb_ref, ib_ref, gb_ref, lbl_ref, gn_ref, o_ref, s_ref, oi_ref, st_ref,
                 *, layer):
    tt = qb_ref.shape[0]
    sub = HGRN_SUB
    hd = HEAD_DIM

    @pl.when(pl.program_id(1) == 0)
    def _():
        s_ref[...] = jnp.zeros_like(s_ref)

    logits = lbl_ref[...]
    e = jnp.exp(logits - jnp.max(logits, axis=0, keepdims=True))
    pl_ = e / jnp.sum(e, axis=0, keepdims=True)
    lb = jnp.sum(pl_[0:layer + 1], axis=0, keepdims=True) - pl_[0:1]

    fb = fb_ref[...].astype(F32)
    t = jnp.exp(-jnp.abs(fb))
    r = 1.0 / (1.0 + t)
    sig_pos = jnp.where(fb >= 0, r, t * r)
    sig_neg = jnp.where(fb >= 0, t * r, r)
    f_gate = lb + (1.0 - lb) * sig_pos
    logf = jnp.log(jnp.maximum(f_gate, F_MIN))
    k_all = (1.0 - lb) * sig_neg
    qb = qb_ref[...].astype(F32)
    q_all = qb * jax.nn.sigmoid(qb)
    v_all = ib_ref[...].astype(F32)

    ri = lax.broadcasted_iota(jnp.int32, (tt, tt), 0)
    ci = lax.broadcasted_iota(jnp.int32, (tt, tt), 1)
    same = (ri // sub) == (ci // sub)
    tri = jnp.where(same & (ci <= ri), 1.0, 0.0).astype(BF16)
    hi, mid, lo = _split3(logf)
    gl = (_dot(tri, hi) + _dot(tri, mid) + _dot(tri, lo)) * LOG2E

    half = sub // 2
    nsub = tt // sub
    nv = tt // half
    lane_sum = jnp.ones((hd, hd), BF16)
    row_in = lax.broadcasted_iota(jnp.int32, (nv, half, hd), 1)

    def pair_terms(qx, gx, kx, vx, causal):
        n = qx.shape[0]
        out = jnp.zeros_like(qx)
        for rho in range(half):
            kr = pltpu.roll(kx, rho, 1) if rho else kx
            vr = pltpu.roll(vx, rho, 1) if rho else vx
            gr = pltpu.roll(gx[1], rho, 1) if rho else gx[1]
            prod = qx * kr * jnp.exp2(gx[0] - gr)
            if causal and rho:
                prod = jnp.where(row_in >= rho, prod, 0.0)
            a = _dot(prod.reshape(n * half, hd).astype(BF16), lane_sum)
            out = out + a.reshape(n, half, hd) * vr
        return out

    for h in range(HGRN_HEADS):
        cs = slice(h * hd, (h + 1) * hd)
        q = q_all[:, cs]
        k = k_all[:, cs]
        v = v_all[:, cs]
        g = gl[:, cs]

        q3, k3, v3, g3 = (a.reshape(nv, half, hd) for a in (q, k, v, g))
        od = pair_terms(q3, (g3, g3), k3, v3, True).reshape(nsub, 2, half, hd)
        q4, k4, v4, g4 = (a.reshape(nsub, 2, half, hd) for a in (q, k, v, g))
        oh = pair_terms(q4[:, 1], (g4[:, 1], g4[:, 0]), k4[:, 0], v4[:, 0], False)
        o_diag = jnp.stack([od[:, 0], od[:, 1] + oh], axis=1).reshape(tt, hd)

        gs = g.reshape(nsub, sub, hd)
        g_end = gs[:, sub - 1:sub, :]
        qd = (q * jnp.exp2(g)).astype(BF16).reshape(nsub, sub, hd)
        kd = (k.reshape(nsub, sub, hd) * jnp.exp2(g_end - gs)).astype(BF16)
        vb = v.astype(BF16).reshape(nsub, sub, hd)
        dec = jnp.exp2(g_end)
        upd = [_dot_t0(vb[c], kd[c]) for c in range(nsub)]
        st = s_ref[h]
        for c in range(nsub):
            st_ref[c] = st.astype(BF16)
            st = st * dec[c] + upd[c]
        s_ref[h] = st
        for c in range(nsub):
            oi_ref[c * sub:(c + 1) * sub, cs] = lax.dot_general(
                qd[c], st_ref[c], (((1,), (1,)), ((), ())), preferred_element_type=F32)

        o = oi_ref[:, cs] + o_diag
        rr = lax.rsqrt(jnp.mean(o * o, axis=-1, keepdims=True) + RMS_EPS)
        gate = jax.nn.sigmoid(gb_ref[:, cs].astype(F32))
        o_ref[:, cs] = (o * rr * gn_ref[...] * gate).astype(o_ref.dtype)


def _hgrn(pa, lb_logits, out_norm, layer, batch, seq):
    tt = min(HGRN_TT, seq)
    nt = seq // tt
    c0 = 3 * MOBA_W // HGRN_W
    nl = lb_logits.shape[0]

    def cmap(off):
        return lambda b, t: (b * nt + t, c0 + off)

    return pl.pallas_call(
        functools.partial(_hgrn_kernel, layer=layer),
        name="hgrn",
        grid=(batch, nt),
        in_specs=[
            pl.BlockSpec((tt, HGRN_W), cmap(0)),
            pl.BlockSpec((tt, HGRN_W), cmap(1)),
            pl.BlockSpec((tt, HGRN_W), cmap(2)),
            pl.BlockSpec((tt, HGRN_W), cmap(3)),
            pl.BlockSpec((nl, HGRN_W), lambda b, t: (0, 0)),
            pl.BlockSpec((1, HEAD_DIM), lambda b, t: (0, 0)),
        ],
        out_specs=pl.BlockSpec((tt, HGRN_W), lambda b, t: (b * nt + t, 0)),
        out_shape=jax.ShapeDtypeStruct((batch * seq, HGRN_W), BF16),
        scratch_shapes=[pltpu.VMEM((HGRN_HEADS, HEAD_DIM, HEAD_DIM), F32),
                        pltpu.VMEM((tt, HGRN_W), F32),
                        pltpu.VMEM((tt // HGRN_SUB, HEAD_DIM, HEAD_DIM), BF16)],
        compiler_params=_cparams(("parallel", "arbitrary")),
    )(pa, pa, pa, pa, lb_logits, out_norm.reshape(1, HEAD_DIM))


def _memkv_kernel(mem_ref, g_ref, w_ref, kt_ref, v_ref):
    h = _rms(mem_ref[0], g_ref[...]).astype(BF16)
    kv = _dot(h, w_ref[...])
    kt_ref[0] = kv[:, 0:MEM_W].T.astype(BF16)
    v_ref[0] = kv[:, MEM_W:].astype(BF16)


def _memkv(mem, g, w):
    b, m, d = mem.shape
    return pl.pallas_call(
        _memkv_kernel,
        name="memkv",
        grid=(b,),
        in_specs=[
            pl.BlockSpec((1, m, d), lambda i: (i, 0, 0)),
            pl.BlockSpec((1, d), lambda i: (0, 0)),
            pl.BlockSpec((d, 2 * MEM_W), lambda i: (0, 0)),
        ],
        out_specs=[
            pl.BlockSpec((1, MEM_W, m), lambda i: (i, 0, 0)),
            pl.BlockSpec((1, m, MEM_W), lambda i: (i, 0, 0)),
        ],
        out_shape=[
            jax.ShapeDtypeStruct((b, MEM_W, m), BF16),
            jax.ShapeDtypeStruct((b, m, MEM_W), BF16),
        ],
        compiler_params=_cparams(("parallel",)),
    )(mem, g.reshape(1, d), w)


def _merge_kernel(x_ref, oa_ref, ob_ref, qm_ref, g0_ref, g1_ref, g2_ref, mkt_ref, mv_ref,
                  wa_ref, wb_ref, wm_ref, wo_ref, o_ref, om_ref):
    scale = HEAD_DIM ** -0.5
    for h in range(MEM_HEADS):
        cs = slice(h * HEAD_DIM, (h + 1) * HEAD_DIM)
        s = _dot(qm_ref[:, cs], mkt_ref[0, cs, :]) * scale
        p = jnp.exp(s - jnp.max(s, axis=-1, keepdims=True))
        l = jnp.sum(p, axis=-1, keepdims=True)
        om_ref[:, cs] = (_dot(p.astype(BF16), mv_ref[0, :, cs]) / l).astype(BF16)

    y = jax.nn.sigmoid(g0_ref[...].astype(F32)) * _dot(oa_ref[...], wa_ref[...])
    y += jax.nn.sigmoid(g1_ref[...].astype(F32)) * _dot(ob_ref[...], wb_ref[...])
    y += jax.nn.sigmoid(g2_ref[...].astype(F32)) * _dot(om_ref[...], wm_ref[...])
    o_ref[...] = x_ref[...] + _dot(y.astype(BF16), wo_ref[...])


def _merge(x, oa, ob, pa, gates, mkt, mv, wa, wb, wm, wo, batch, seq):
    n, d = x.shape
    m = mv.shape[1]
    tm = _pick(seq, MERGE_TM)
    per_b = seq // tm
    qcol = (3 * MOBA_W + 4 * HGRN_W) // MEM_W
    row = lambda i: (i, 0)
    return pl.pallas_call(
        _merge_kernel,
        name="merge",
        grid=(n // tm,),
        in_specs=[
            pl.BlockSpec((tm, d), row),
            pl.BlockSpec((tm, MOBA_W), row),
            pl.BlockSpec((tm, HGRN_W), row),
            pl.BlockSpec((tm, MEM_W), lambda i: (i, qcol)),
            pl.BlockSpec((tm, d), lambda i: (i, 0)),
            pl.BlockSpec((tm, d), lambda i: (i, 1)),
            pl.BlockSpec((tm, d), lambda i: (i, 2)),
            pl.BlockSpec((1, MEM_W, m), lambda i: (i // per_b, 0, 0)),
            pl.BlockSpec((1, m, MEM_W), lambda i: (i // per_b, 0, 0)),
            _const_spec((MOBA_W, d)),
            _const_spec((HGRN_W, d)),
            _const_spec((MEM_W, d)),
            _const_spec((d, d)),
        ],
        out_specs=pl.BlockSpec((tm, d), row),
        out_shape=jax.ShapeDtypeStruct((n, d), F32),
        scratch_shapes=[pltpu.VMEM((tm, MEM_W), BF16)],
        compiler_params=_cparams(("parallel",)),
    )(x, oa, ob, pa, gates, gates, gates, mkt, mv, wa, wb, wm, wo)


def kernel(x, mem, ffn1_norm, ffn1_w1, ffn1_w3, ffn1_w2, mix_norm, w_in, hgrn_lb_logits,
           hgrn_out_norm, mem_norm, w_mem_kv, w_proj_moba, w_proj_hgrn, w_proj_mem, w_out,
           ffn2_norm, ffn2_w1, ffn2_w3, ffn2_w2, final_norm):
    batch, seq, d = x.shape
    depth = ffn1_w1.shape[0]
    assert seq % (MOBA_QTILE * MOBA_BLOCK) == 0
    assert w_in.shape[-1] == MIX_W + 3 * d
    hs = jnp.arange(1, MOBA_HEADS + 1, dtype=F32)
    slopes = jnp.tile(jnp.exp2(-8.0 * hs / MOBA_HEADS), batch)
    slopes = jnp.broadcast_to(slopes[:, None, None], (batch * MOBA_HEADS, 1, HEAD_DIM))
    mix_scale = jnp.concatenate([jnp.full((MOBA_W,), HEAD_DIM ** -0.5, F32),
                                 jnp.ones((MIX_W - MOBA_W,), F32)])
    gate_scale = jnp.ones((3 * d,), F32)

    xs = x.reshape(batch * seq, d)
    for l in range(depth):
        last = l == depth - 1
        bf = functools.partial(_cast_layer, layer=l)
        xs = _ffn(xs, ffn1_norm[l], bf(ffn1_w1), bf(ffn1_w3), bf(ffn1_w2), final_norm, final=False)

        w_mix, w_gates = bf(w_in, splits=(MIX_W, 3 * d))
        pa = _normproj(xs, mix_norm[l], w_mix, mix_scale)
        gates = _normproj(xs, mix_norm[l], w_gates, gate_scale)

        vtb, kmean = _moba_prep(pa, batch, seq)
        ob = _hgrn(pa, hgrn_lb_logits, hgrn_out_norm[l], l, batch, seq)
        oa, ob = _moba_routed_attention(pa, vtb, kmean, slopes, batch, seq, ob)
        mkt, mv = _memkv(mem, mem_norm[l], bf(w_mem_kv))
        xs = _merge(xs, oa, ob, pa, gates, mkt, mv, bf(w_proj_moba), bf(w_proj_hgrn),
                    bf(w_proj_mem), bf(w_out), batch, seq)

        xs = _ffn(xs, ffn2_norm[l], bf(ffn2_w1), bf(ffn2_w3), bf(ffn2_w2), final_norm, final=last)
    return xs.reshape(batch, seq, d)
```

```python
import functools

import jax
import jax.numpy as jnp
from jax import lax
from jax.experimental import pallas as pl
from jax.experimental.pallas import tpu as pltpu
from jax.experimental.pallas import tpu_sc as plsc

F32 = jnp.float32
BF16 = jnp.bfloat16

HEAD_DIM = 128
MOBA_HEADS = 8
MOBA_BLOCK = 256
MOBA_TOPK = 3
HGRN_HEADS = 4
MEM_HEADS = 4
RMS_EPS = 1e-6
NEG_INF = -1e30
F_MIN = 1e-20
LOG2E = 1.4426950408889634

MOBA_W = MOBA_HEADS * HEAD_DIM
HGRN_W = HGRN_HEADS * HEAD_DIM
MEM_W = MEM_HEADS * HEAD_DIM
MIX_W = 3 * MOBA_W + 4 * HGRN_W + MEM_W

VMEM_LIMIT_BYTES = 60 * 1024 * 1024

LANE = 128
BF16_ROWS = 16
VT_ROWS = HEAD_DIM + BF16_ROWS
MOBA_QTILE = 4

HGRN_SUB = 16
HGRN_TT = 256
FFN_TM = 1024
FFN_TF = 512
FFN_OUT_CHUNK = 512
PROJ_TM = 512
PROJ_TN = 512
MERGE_TM = 512
SLOTS_TQ = 2048
CAST_BLOCK_BYTES = 8 * 1024 * 1024
SLOT_TILE = 512
ROUTED_TILES_PER_STEP = 32
SC_WINDOW = 128


def _cparams(sem):
    return pltpu.CompilerParams(dimension_semantics=sem, vmem_limit_bytes=VMEM_LIMIT_BYTES)


def _dot(a, b):
    return jnp.dot(a, b, preferred_element_type=F32)


def _dot_t0(a, b):
    return lax.dot_general(a, b, (((0,), (0,)), ((), ())), preferred_element_type=F32)


def _rms(xf, g):
    r = lax.rsqrt(jnp.mean(xf * xf, axis=-1, keepdims=True) + RMS_EPS)
    return xf * r * g


def _pick(n, want):
    if n <= want:
        return n
    t = (want // LANE) * LANE
    while t >= LANE:
        if n % t == 0:
            return t
        t -= LANE
    return n


def _const_spec(shape):
    nd = len(shape)
    return pl.BlockSpec(shape, lambda *_: (0,) * nd, pipeline_mode=pl.Buffered(1))


def _cast_kernel(w_ref, *o_refs, splits):
    lo = 0
    for o_ref, width in zip(o_refs, splits):
        o_ref[...] = w_ref[0, :, lo:lo + width].astype(BF16)
        lo += width


def _cast_layer(w_stack, layer, splits=None):
    _, rows, cols = w_stack.shape
    splits = (cols,) if splits is None else tuple(splits)
    assert sum(splits) == cols
    tr = rows
    while tr * cols * 4 > CAST_BLOCK_BYTES and tr % (2 * BF16_ROWS) == 0:
        tr //= 2
    outs = pl.pallas_call(
        functools.partial(_cast_kernel, splits=splits),
        name="cast",
        grid=(rows // tr,),
        in_specs=[pl.BlockSpec((1, tr, cols), lambda i: (layer, i, 0))],
        out_specs=[pl.BlockSpec((tr, w), lambda i: (i, 0)) for w in splits],
        out_shape=[jax.ShapeDtypeStruct((rows, w), BF16) for w in splits],
        compiler_params=_cparams(("parallel",)),
    )(w_stack)
    return outs[0] if len(splits) == 1 else outs


def _ffn_kernel(x_ref, g_ref, w1_ref, w3_ref, w2_ref, fg_ref, o_ref, h_ref, *, final):
    j = pl.program_id(1)

    @pl.when(j == 0)
    def _():
        xf = x_ref[...]
        h_ref[...] = _rms(xf, g_ref[...]).astype(BF16)
        o_ref[...] = xf

    h = h_ref[...]
    u = _dot(h, w1_ref[...])
    v = _dot(h, w3_ref[...])
    a = (0.5 * u * jax.nn.sigmoid(u) * v).astype(BF16)
    tn = FFN_OUT_CHUNK if o_ref.shape[1] % FFN_OUT_CHUNK == 0 else o_ref.shape[1]
    for c in range(o_ref.shape[1] // tn):
        cs = slice(c * tn, (c + 1) * tn)
        o_ref[:, cs] += _dot(a, w2_ref[:, cs])

    if final:
        @pl.when(j == pl.num_programs(1) - 1)
        def _():
            o_ref[...] = _rms(o_ref[...], fg_ref[...])


def _ffn(x, g, w1, w3, w2, final_g, *, final):
    n, d = x.shape
    dff = w1.shape[1]
    tm = _pick(n, FFN_TM)
    tf = _pick(dff, FFN_TF)
    return pl.pallas_call(
        functools.partial(_ffn_kernel, final=final),
        name="ffn_final" if final else "ffn",
        grid=(n // tm, dff // tf),
        in_specs=[
            pl.BlockSpec((tm, d), lambda i, j: (i, 0)),
            pl.BlockSpec((1, d), lambda i, j: (0, 0)),
            pl.BlockSpec((d, tf), lambda i, j: (0, j)),
            pl.BlockSpec((d, tf), lambda i, j: (0, j)),
            pl.BlockSpec((tf, d), lambda i, j: (j, 0)),
            pl.BlockSpec((1, d), lambda i, j: (0, 0)),
        ],
        out_specs=pl.BlockSpec((tm, d), lambda i, j: (i, 0)),
        out_shape=jax.ShapeDtypeStruct((n, d), F32),
        scratch_shapes=[pltpu.VMEM((tm, d), BF16)],
        compiler_params=_cparams(("parallel", "arbitrary")),
    )(x, g.reshape(1, d), w1, w3, w2, final_g.reshape(1, d))


def _normproj_kernel(x_ref, g_ref, w_ref, s_ref, o_ref, h_ref, *, tn):
    h_ref[...] = _rms(x_ref[...], g_ref[...]).astype(BF16)
    for c in range(w_ref.shape[1] // tn):
        cs = slice(c * tn, (c + 1) * tn)
        o_ref[:, cs] = (_dot(h_ref[...], w_ref[:, cs]) * s_ref[:, cs]).astype(o_ref.dtype)


def _normproj(x, g, w, col_scale):
    n, d = x.shape
    nout = w.shape[1]
    tm = _pick(n, PROJ_TM)
    tn = _pick(nout, PROJ_TN)
    return pl.pallas_call(
        functools.partial(_normproj_kernel, tn=tn),
        name="normproj",
        grid=(n // tm,),
        in_specs=[
            pl.BlockSpec((tm, d), lambda i: (i, 0)),
            pl.BlockSpec((1, d), lambda i: (0, 0)),
            pl.BlockSpec((d, nout), lambda i: (0, 0), pipeline_mode=pl.Buffered(1)),
            pl.BlockSpec((1, nout), lambda i: (0, 0)),
        ],
        out_specs=pl.BlockSpec((tm, nout), lambda i: (i, 0)),
        out_shape=jax.ShapeDtypeStruct((n, nout), BF16),
        scratch_shapes=[pltpu.VMEM((tm, d), BF16)],
        compiler_params=_cparams(("parallel",)),
    )(x, g.reshape(1, d), w, col_scale.reshape(1, nout))


def _moba_prep_kernel(k_ref, v_ref, vtb_ref, kmean_ref):
    g = pl.program_id(1)
    bs = MOBA_BLOCK
    nblk = k_ref.shape[0] // bs
    orow = lax.broadcasted_iota(jnp.int32, (VT_ROWS - HEAD_DIM, bs), 0)
    ones_row = jnp.where(orow == 0, 1.0, 0.0).astype(BF16)
    for u in range(nblk):
        k = k_ref[u * bs:(u + 1) * bs, :]
        kmean_ref[0, pl.ds(g * nblk + u, 1), :] = jnp.mean(k.astype(F32), axis=0, keepdims=True)
        vt = v_ref[u * bs:(u + 1) * bs, :].astype(F32).T
        vtb_ref[0, u, 0:HEAD_DIM, :] = vt.astype(BF16)
        vtb_ref[0, u, HEAD_DIM:VT_ROWS, :] = ones_row


def _moba_prep(pa, batch, seq):
    nb = seq // MOBA_BLOCK
    grp = MOBA_QTILE
    ng = nb // grp
    bh = batch * MOBA_HEADS
    kcol = MOBA_W // HEAD_DIM
    vcol = 2 * MOBA_W // HEAD_DIM

    def kmap(n, g):
        return ((n // MOBA_HEADS) * ng + g, kcol + n % MOBA_HEADS)

    def vmap(n, g):
        return ((n // MOBA_HEADS) * ng + g, vcol + n % MOBA_HEADS)

    return pl.pallas_call(
        _moba_prep_kernel,
        name="moba_prep",
        grid=(bh, ng),
        in_specs=[
            pl.BlockSpec((grp * MOBA_BLOCK, HEAD_DIM), kmap),
            pl.BlockSpec((grp * MOBA_BLOCK, HEAD_DIM), vmap),
        ],
        out_specs=[
            pl.BlockSpec((1, grp, VT_ROWS, MOBA_BLOCK), lambda n, g: (n, g, 0, 0)),
            pl.BlockSpec((1, nb, HEAD_DIM), lambda n, g: (n, 0, 0)),
        ],
        out_shape=[
            jax.ShapeDtypeStruct((bh, nb, VT_ROWS, MOBA_BLOCK), BF16),
            jax.ShapeDtypeStruct((bh, nb, HEAD_DIM), F32),
        ],
        compiler_params=_cparams(("parallel", "arbitrary")),
    )(pa, pa)


def _moba_gate_kernel(q_ref, kmean_ref, qrow_ref, sel_ref, rank_ref, cnt_ref):
    t = pl.program_id(1)
    bs = MOBA_BLOCK
    tq = q_ref.shape[0]
    nb = kmean_ref.shape[1]

    q = q_ref[...]
    qrow_ref[...] = q.astype(F32)
    qt = q.astype(F32).T.astype(BF16)

    lane = lax.broadcasted_iota(jnp.int32, (1, tq), 1)
    own = t * (tq // bs) + lane // bs
    km = kmean_ref[0]
    km_hi = km.astype(BF16)
    km_lo = (km - km_hi.astype(F32)).astype(BF16)
    gate = _dot(km_hi, qt) + _dot(km_lo, qt)
    blk = lax.broadcasted_iota(jnp.int32, (nb, tq), 0)
    gate = jnp.where(blk < own, gate, NEG_INF)

    @pl.when(t == 0)
    def _():
        cnt_ref[...] = jnp.zeros_like(cnt_ref)

    run = cnt_ref[0][:, 0:1]
    qi = lax.broadcasted_iota(jnp.int32, (bs, bs), 0)
    qj = lax.broadcasted_iota(jnp.int32, (bs, bs), 1)
    before = jnp.where(qi < qj, 1.0, 0.0).astype(BF16)
    sels, ranks = [], []
    for r in range(MOBA_TOPK):
        mx = jnp.max(gate, axis=0, keepdims=True)
        first = jnp.min(jnp.where(gate == mx, blk, nb), axis=0, keepdims=True)
        hit = blk == first
        gate = jnp.where(hit, -jnp.inf, gate)
        valid = own > r
        oh = jnp.where(hit & valid, 1.0, 0.0)
        parts = []
        for c in range(tq // bs):
            ohc = oh[:, c * bs:(c + 1) * bs]
            prior = _dot(ohc.astype(BF16), before)
            parts.append(jnp.sum(ohc * (run + prior), axis=0, keepdims=True))
            run = run + jnp.sum(ohc, axis=1, keepdims=True)
        ranks.append(jnp.concatenate(parts, axis=1))
        sels.append(jnp.where(valid, first, -1))
    cnt_ref[0] = jnp.broadcast_to(run, cnt_ref.shape[1:])
    pad = jnp.zeros((8 - MOBA_TOPK, tq), jnp.int32)
    sel_ref[0] = jnp.concatenate(sels + [pad], axis=0)
    rank_ref[0] = jnp.concatenate([x.astype(jnp.int32) for x in ranks] + [pad], axis=0)


def _moba_gate(pa, kmean, batch, seq):
    nb = seq // MOBA_BLOCK
    tq = min(MOBA_QTILE * MOBA_BLOCK, seq)
    nt = seq // tq
    bh = batch * MOBA_HEADS

    def qmap(n, t):
        return ((n // MOBA_HEADS) * nt + t, n % MOBA_HEADS)

    return pl.pallas_call(
        _moba_gate_kernel,
        name="moba_gate",
        grid=(bh, nt),
        in_specs=[
            pl.BlockSpec((tq, HEAD_DIM), qmap),
            pl.BlockSpec((1, nb, HEAD_DIM), lambda n, t: (n, 0, 0)),
        ],
        out_specs=[
            pl.BlockSpec((tq, HEAD_DIM), lambda n, t: (n * nt + t, 0)),
            pl.BlockSpec((1, 8, tq), lambda n, t: (n, 0, t)),
            pl.BlockSpec((1, 8, tq), lambda n, t: (n, 0, t)),
            pl.BlockSpec((1, nb, HEAD_DIM), lambda n, t: (n, 0, 0)),
        ],
        out_shape=[
            jax.ShapeDtypeStruct((bh * seq, HEAD_DIM), F32),
            jax.ShapeDtypeStruct((bh, 8, seq), jnp.int32),
            jax.ShapeDtypeStruct((bh, 8, seq), jnp.int32),
            jax.ShapeDtypeStruct((bh, nb, HEAD_DIM), F32),
        ],
        compiler_params=_cparams(("parallel", "arbitrary")),
    )(pa, kmean)


def _moba_slot_kernel(sel_ref, rank_ref, base_ref, slot_ref, *, trash):
    nb = base_ref.shape[1]
    tq = sel_ref.shape[2]
    base = base_ref[0][:, 0:1]
    blk = lax.broadcasted_iota(jnp.int32, (nb, tq), 0)
    spare = trash + lax.broadcasted_iota(jnp.int32, (1, tq), 1) % SLOT_TILE
    rows = []
    for r in range(MOBA_TOPK):
        sel = sel_ref[0, r:r + 1, :]
        start = jnp.sum(jnp.where(blk == sel, base, 0.0), axis=0, keepdims=True)
        slot = start.astype(jnp.int32) + rank_ref[0, r:r + 1, :]
        rows.append(jnp.where(sel >= 0, slot, spare))
    rows.append(jnp.broadcast_to(spare, (8 - MOBA_TOPK, tq)))
    slot_ref[0] = jnp.concatenate(rows, axis=0)


def _moba_slots(sel, rank, base, trash):
    bh, _, seq = sel.shape
    nb = base.shape[1]
    tq = min(SLOTS_TQ, seq)
    spec = pl.BlockSpec((1, 8, tq), lambda n, t: (n, 0, t))
    return pl.pallas_call(
        functools.partial(_moba_slot_kernel, trash=trash),
        name="moba_slots",
        grid=(bh, seq // tq),
        in_specs=[spec, spec, pl.BlockSpec((1, nb, HEAD_DIM), lambda n, t: (n, 0, 0))],
        out_specs=spec,
        out_shape=jax.ShapeDtypeStruct((bh, 8, seq), jnp.int32),
        compiler_params=_cparams(("parallel", "parallel")),
    )(sel, rank, base)


def _sc_mesh():
    return plsc.VectorSubcoreMesh(core_axis_name="core", subcore_axis_name="subcore")


def _sc_scatter_rows(rows, slots, n_out):
    bh8, seq = slots.shape
    bh = bh8 // 8
    nw = seq // SC_WINDOW

    @pl.kernel(out_type=jax.ShapeDtypeStruct((n_out, HEAD_DIM), rows.dtype), mesh=_sc_mesh(),
               scratch_types=[])
    def scatter(x_hbm, i_hbm, o_hbm):
        def body(x_vmem, i_vmem):
            pltpu.sync_copy(x_vmem, o_hbm.at[i_vmem.at[0]])

        pltpu.emit_pipeline(
            body,
            grid=(bh * MOBA_TOPK * nw,),
            in_specs=[
                pl.BlockSpec((SC_WINDOW, HEAD_DIM),
                             index_map=lambda i: ((i // (MOBA_TOPK * nw)) * nw + i % nw, 0)),
                pl.BlockSpec((1, SC_WINDOW),
                             index_map=lambda i: ((i // (MOBA_TOPK * nw)) * 8 + (i // nw) % MOBA_TOPK,
                                                  i % nw)),
            ],
            out_specs=[],
            core_axis_name=("core", "subcore"),
            dimension_semantics=(pltpu.PARALLEL,),
        )(x_hbm, i_hbm)

    return scatter(rows, slots)


def _sc_gather_rows(table, slots):
    bh8, seq = slots.shape
    bh = bh8 // 8
    nw = seq // SC_WINDOW
    n_out = bh * MOBA_TOPK * seq

    @pl.kernel(out_type=jax.ShapeDtypeStruct((n_out, HEAD_DIM), table.dtype), mesh=_sc_mesh())
    def gather(x_hbm, i_hbm, o_hbm):
        def body(i_vmem, o_vmem):
            pltpu.sync_copy(x_hbm.at[i_vmem.at[0]], o_vmem)

        pltpu.emit_pipeline(
            body,
            grid=(bh * MOBA_TOPK * nw,),
            in_specs=[
                pl.BlockSpec((1, SC_WINDOW),
                             index_map=lambda i: ((i // (MOBA_TOPK * nw)) * 8 + (i // nw) % MOBA_TOPK,
                                                  i % nw)),
            ],
            out_specs=[pl.BlockSpec((SC_WINDOW, HEAD_DIM), index_map=lambda i: (i, 0))],
            core_axis_name=("core", "subcore"),
            dimension_semantics=(pltpu.PARALLEL,),
        )(i_hbm, o_hbm)

    return gather(table, slots)


def _moba_routed_kernel(tb_ref, nt_ref, qs_ref, k_ref, vtb_ref, slope_ref, o_ref,
                        sa_ref, sb_ref, sc_ref):
    n = pl.program_id(0)
    g = pl.program_id(1)
    bs = MOBA_BLOCK
    tpg = qs_ref.shape[0] // SLOT_TILE
    tiles_cap = pl.num_programs(1) * tpg

    @pl.when(g * tpg < nt_ref[n])
    def _():
        krow = lax.broadcasted_iota(jnp.int32, (bs, SLOT_TILE), 0).astype(F32)
        bias = slope_ref[0, 0:1, 0:1] * krow
        frow = lax.broadcasted_iota(jnp.int32, (HEAD_DIM, SLOT_TILE), 0)
        blocks = [tb_ref[n * tiles_cap + g * tpg + u] for u in range(tpg)]

        def scores(u):
            qt = qs_ref[u * SLOT_TILE:(u + 1) * SLOT_TILE, :].T.astype(BF16)
            k = k_ref[pl.ds(pl.multiple_of(blocks[u] * bs, bs), bs), :]
            return _dot(k, qt) + bias

        bufs = (sa_ref, sb_ref, sc_ref)
        sa_ref[...] = scores(0)
        sb_ref[...] = scores(1)
        for u in range(tpg):
            if u + 2 < tpg:
                bufs[(u + 2) % 3][...] = scores(u + 2)
            s = bufs[u % 3][...]
            m = jnp.max(s, axis=0, keepdims=True)
            p = jnp.exp(s - m).astype(BF16)
            acc = _dot(vtb_ref[0, blocks[u]], p)
            l = acc[HEAD_DIM:HEAD_DIM + 1]
            o = (acc[0:HEAD_DIM] / l).astype(BF16).astype(F32)
            lse = m + jnp.log(l)
            ob = lax.bitcast_convert_type(o, jnp.uint32)
            lb = lax.bitcast_convert_type(lse, jnp.uint32)
            extra = jnp.where(frow == 0, lb >> 16, jnp.where(frow == 1, lb & 0xFFFF, 0))
            packed = lax.bitcast_convert_type(ob | extra, F32)
            o_ref[u * SLOT_TILE:(u + 1) * SLOT_TILE, :] = packed.T


def _moba_routed(qs, pa, vtb, slopes, tile_block, n_tiles, tiles_cap):
    bh, nb = vtb.shape[0], vtb.shape[1]
    seq = nb * MOBA_BLOCK
    kcol = MOBA_W // HEAD_DIM
    tpg = ROUTED_TILES_PER_STEP
    steps = tiles_cap // tpg
    rows = tpg * SLOT_TILE

    def qmap(n, g, tb, nt):
        used = jnp.maximum((nt[n] + tpg - 1) // tpg, 1)
        return (n * steps + jnp.minimum(g, used - 1), 0)

    grid_spec = pltpu.PrefetchScalarGridSpec(
        num_scalar_prefetch=2,
        grid=(bh, steps),
        in_specs=[
            pl.BlockSpec((rows, HEAD_DIM), qmap),
            pl.BlockSpec((seq, HEAD_DIM),
                         lambda n, g, tb, nt: (n // MOBA_HEADS, kcol + n % MOBA_HEADS)),
            pl.BlockSpec((1, nb, VT_ROWS, MOBA_BLOCK), lambda n, g, tb, nt: (n, 0, 0, 0)),
            pl.BlockSpec((1, 1, HEAD_DIM), lambda n, g, tb, nt: (n, 0, 0)),
        ],
        out_specs=pl.BlockSpec((rows, HEAD_DIM), qmap),
        scratch_shapes=[pltpu.VMEM((MOBA_BLOCK, SLOT_TILE), F32)] * 3,
    )
    return pl.pallas_call(
        _moba_routed_kernel,
        name="moba_routed",
        grid_spec=grid_spec,
        out_shape=jax.ShapeDtypeStruct((qs.shape[0], HEAD_DIM), F32),
        compiler_params=_cparams(("parallel", "arbitrary")),
    )(tile_block, n_tiles, qs, pa, vtb, slopes)


def _moba_merge_kernel(q_ref, k_ref, vtb_ref, og_ref, sel_ref, slope_ref, o_ref,
                       sa_ref, sb_ref, sc_ref):
    t = pl.program_id(1)
    bs = MOBA_BLOCK
    tq = q_ref.shape[0]
    slope = slope_ref[0, 0:1, 0:1]

    qt = q_ref[...].astype(F32).T.astype(BF16)
    lane = lax.broadcasted_iota(jnp.int32, (1, tq), 1)
    tpos = (t * tq + lane).astype(F32)
    lk = lax.broadcasted_iota(jnp.int32, (bs, bs), 0)
    lq = lax.broadcasted_iota(jnp.int32, (bs, bs), 1)
    dist = (lq - lk).astype(F32)
    nq = tq // bs

    def scores(u):
        return _dot(k_ref[u * bs:(u + 1) * bs, :], qt[:, u * bs:(u + 1) * bs])

    sbufs = (sa_ref, sb_ref, sc_ref)
    for u in range(min(2, nq)):
        sbufs[u][...] = scores(u)
    accs, ms = [], []
    for u in range(nq):
        if u + 2 < nq:
            sbufs[(u + 2) % 3][...] = scores(u + 2)
        s = sbufs[u % 3][...]
        s = jnp.where(dist >= 0.0, s - slope * dist, NEG_INF)
        m_u = jnp.max(s, axis=0, keepdims=True)
        p = jnp.exp(s - m_u).astype(BF16)
        accs.append(_dot(vtb_ref[0, u], p))
        ms.append(m_u)
    acc = jnp.concatenate(accs, axis=1)
    l = acc[HEAD_DIM:HEAD_DIM + 1]
    parts = [acc[0:HEAD_DIM] / l]
    lses = [jnp.concatenate(ms, axis=1) + jnp.log(l)]

    for r in range(MOBA_TOPK):
        u = lax.bitcast_convert_type(og_ref[0, r, 0].T, jnp.uint32)
        lb = ((u[0:1] & 0xFFFF) << 16) | (u[1:2] & 0xFFFF)
        lse_r = lax.bitcast_convert_type(lb, F32)
        sel_r = sel_ref[0, r:r + 1, :]
        valid = sel_r >= 0
        lses.append(jnp.where(valid, lse_r + slope * ((sel_r * bs).astype(F32) - tpos), NEG_INF))
        hi_half = lax.bitcast_convert_type((u >> 16) << 16, F32)
        parts.append(jnp.where(valid, hi_half, 0.0))
    mx = functools.reduce(jnp.maximum, lses)
    ws = [jnp.exp(x - mx) for x in lses]
    num = functools.reduce(lambda a, b: a + b, [w * o for w, o in zip(ws, parts)])
    out_t = num / functools.reduce(lambda a, b: a + b, ws)
    o_ref[...] = out_t.T.astype(o_ref.dtype)


def _moba_merge(pa, vtb, og, sel, slopes, batch, seq):
    tq = MOBA_QTILE * MOBA_BLOCK
    nt = seq // tq
    bh = batch * MOBA_HEADS
    kcol = MOBA_W // HEAD_DIM
    og = og.reshape(bh, MOBA_TOPK, nt, tq, HEAD_DIM)

    def qmap(n, t):
        return ((n // MOBA_HEADS) * nt + t, n % MOBA_HEADS)

    def kmap(n, t):
        return ((n // MOBA_HEADS) * nt + t, kcol + n % MOBA_HEADS)

    return pl.pallas_call(
        _moba_merge_kernel,
        name="moba_merge",
        grid=(bh, nt),
        in_specs=[
            pl.BlockSpec((tq, HEAD_DIM), qmap),
            pl.BlockSpec((tq, HEAD_DIM), kmap),
            pl.BlockSpec((1, MOBA_QTILE, VT_ROWS, MOBA_BLOCK), lambda n, t: (n, t, 0, 0)),
            pl.BlockSpec((1, MOBA_TOPK, 1, tq, HEAD_DIM), lambda n, t: (n, 0, t, 0, 0)),
            pl.BlockSpec((1, 8, tq), lambda n, t: (n, 0, t)),
            pl.BlockSpec((1, 1, HEAD_DIM), lambda n, t: (n, 0, 0)),
        ],
        out_specs=pl.BlockSpec((tq, HEAD_DIM), qmap),
        out_shape=jax.ShapeDtypeStruct((batch * seq, MOBA_W), BF16),
        scratch_shapes=[pltpu.VMEM((MOBA_BLOCK, MOBA_BLOCK), F32)] * 3,
        compiler_params=_cparams(("parallel", "parallel")),
    )(pa, pa, vtb, og, sel, slopes)


def _moba_routed_attention(pa, vtb, kmean, slopes, batch, seq, companion):
    nb = seq // MOBA_BLOCK
    bh = batch * MOBA_HEADS
    qrows, sel, rank, cnt = _moba_gate(pa, kmean, batch, seq)

    step_slots = SLOT_TILE * ROUTED_TILES_PER_STEP
    cap = -(-(MOBA_TOPK * seq + nb * SLOT_TILE) // step_slots) * step_slots
    tiles_cap = cap // SLOT_TILE
    counts = cnt[:, :, 0].astype(jnp.int32)
    padded = (counts + SLOT_TILE - 1) // SLOT_TILE * SLOT_TILE
    ends = jnp.cumsum(padded, axis=1)
    base = ends - padded + (jnp.arange(bh, dtype=jnp.int32) * cap)[:, None]
    n_tiles = ends[:, -1] // SLOT_TILE
    tile_idx = jnp.arange(tiles_cap, dtype=jnp.int32)
    tile_block = jnp.sum(ends[:, None, :] // SLOT_TILE <= tile_idx[None, :, None], axis=-1)
    tile_block = jnp.minimum(tile_block, nb - 1).astype(jnp.int32).reshape(bh * tiles_cap)
    base_b = jnp.broadcast_to(base.astype(F32)[:, :, None], (bh, nb, HEAD_DIM))

    n_rows = bh * cap + SLOT_TILE
    slots = _moba_slots(sel, rank, base_b, bh * cap).reshape(bh * 8, seq)
    qs = _sc_scatter_rows(qrows, slots, n_rows)
    qs, companion = lax.optimization_barrier((qs, companion))
    part = _moba_routed(qs, pa, vtb, slopes, tile_block, n_tiles, tiles_cap)
    og = _sc_gather_rows(part, slots)
    return _moba_merge(pa, vtb, og, sel, slopes, batch, seq), companion


def _split3(x):
    hi = x.astype(BF16)
    r1 = x - hi.astype(F32)
    mid = r1.astype(BF16)
    lo = (r1 - mid.astype(F32)).astype(BF16)
    return hi, mid, lo


def _hgrn_kernel(qb_ref, fb_ref, ib_ref, gb_ref, lbl_ref, gn_ref, o_ref, s_ref, oi_ref, st_ref,
                 *, layer):
    tt = qb_ref.shape[0]
    sub = HGRN_SUB
    hd = HEAD_DIM

    @pl.when(pl.program_id(1) == 0)
    def _():
        s_ref[...] = jnp.zeros_like(s_ref)

    logits = lbl_ref[...]
    e = jnp.exp(logits - jnp.max(logits, axis=0, keepdims=True))
    pl_ = e / jnp.sum(e, axis=0, keepdims=True)
    lb = jnp.sum(pl_[0:layer + 1], axis=0, keepdims=True) - pl_[0:1]

    fb = fb_ref[...].astype(F32)
    t = jnp.exp(-jnp.abs(fb))
    r = 1.0 / (1.0 + t)
    sig_pos = jnp.where(fb >= 0, r, t * r)
    sig_neg = jnp.where(fb >= 0, t * r, r)
    f_gate = lb + (1.0 - lb) * sig_pos
    logf = jnp.log(jnp.maximum(f_gate, F_MIN))
    k_all = (1.0 - lb) * sig_neg
    qb = qb_ref[...].astype(F32)
    q_all = qb * jax.nn.sigmoid(qb)
    v_all = ib_ref[...].astype(F32)

    ri = lax.broadcasted_iota(jnp.int32, (tt, tt), 0)
    ci = lax.broadcasted_iota(jnp.int32, (tt, tt), 1)
    same = (ri // sub) == (ci // sub)
    tri = jnp.where(same & (ci <= ri), 1.0, 0.0).astype(BF16)
    hi, mid, lo = _split3(logf)
    gl = (_dot(tri, hi) + _dot(tri, mid) + _dot(tri, lo)) * LOG2E

    half = sub // 2
    nsub = tt // sub
    nv = tt // half
    lane_sum = jnp.ones((hd, hd), BF16)
    row_in = lax.broadcasted_iota(jnp.int32, (nv, half, hd), 1)

    def pair_terms(qx, gx, kx, vx, causal):
        n = qx.shape[0]
        out = jnp.zeros_like(qx)
        for rho in range(half):
            kr = pltpu.roll(kx, rho, 1) if rho else kx
            vr = pltpu.roll(vx, rho, 1) if rho else vx
            gr = pltpu.roll(gx[1], rho, 1) if rho else gx[1]
            prod = qx * kr * jnp.exp2(gx[0] - gr)
            if causal and rho:
                prod = jnp.where(row_in >= rho, prod, 0.0)
            a = _dot(prod.reshape(n * half, hd).astype(BF16), lane_sum)
            out = out + a.reshape(n, half, hd) * vr
        return out

    for h in range(HGRN_HEADS):
        cs = slice(h * hd, (h + 1) * hd)
        q = q_all[:, cs]
        k = k_all[:, cs]
        v = v_all[:, cs]
        g = gl[:, cs]

        q3, k3, v3, g3 = (a.reshape(nv, half, hd) for a in (q, k, v, g))
        od = pair_terms(q3, (g3, g3), k3, v3, True).reshape(nsub, 2, half, hd)
        q4, k4, v4, g4 = (a.reshape(nsub, 2, half, hd) for a in (q, k, v, g))
        oh = pair_terms(q4[:, 1], (g4[:, 1], g4[:, 0]), k4[:, 0], v4[:, 0], False)
        o_diag = jnp.stack([od[:, 0], od[:, 1] + oh], axis=1).reshape(tt, hd)

        gs = g.reshape(nsub, sub, hd)
        g_end = gs[:, sub - 1:sub, :]
        qd = (q * jnp.exp2(g)).astype(BF16).reshape(nsub, sub, hd)
        kd = (k.reshape(nsub, sub, hd) * jnp.exp2(g_end - gs)).astype(BF16)
        vb = v.astype(BF16).reshape(nsub, sub, hd)
        dec = jnp.exp2(g_end)
        upd = [_dot_t0(vb[c], kd[c]) for c in range(nsub)]
        st = s_ref[h]
        for c in range(nsub):
            st_ref[c] = st.astype(BF16)
            st = st * dec[c] + upd[c]
        s_ref[h] = st
        for c in range(nsub):
            oi_ref[c * sub:(c + 1) * sub, cs] = lax.dot_general(
                qd[c], st_ref[c], (((1,), (1,)), ((), ())), preferred_element_type=F32)

        o = oi_ref[:, cs] + o_diag
        rr = lax.rsqrt(jnp.mean(o * o, axis=-1, keepdims=True) + RMS_EPS)
        gate = jax.nn.sigmoid(gb_ref[:, cs].astype(F32))
        o_ref[:, cs] = (o * rr * gn_ref[...] * gate).astype(o_ref.dtype)


def _hgrn(pa, lb_logits, out_norm, layer, batch, seq):
    tt = min(HGRN_TT, seq)
    nt = seq // tt
    c0 = 3 * MOBA_W // HGRN_W
    nl = lb_logits.shape[0]

    def cmap(off):
        return lambda b, t: (b * nt + t, c0 + off)

    return pl.pallas_call(
        functools.partial(_hgrn_kernel, layer=layer),
        name="hgrn",
        grid=(batch, nt),
        in_specs=[
            pl.BlockSpec((tt, HGRN_W), cmap(0)),
            pl.BlockSpec((tt, HGRN_W), cmap(1)),
            pl.BlockSpec((tt, HGRN_W), cmap(2)),
            pl.BlockSpec((tt, HGRN_W), cmap(3)),
            pl.BlockSpec((nl, HGRN_W), lambda b, t: (0, 0)),
            pl.BlockSpec((1, HEAD_DIM), lambda b, t: (0, 0)),
        ],
        out_specs=pl.BlockSpec((tt, HGRN_W), lambda b, t: (b * nt + t, 0)),
        out_shape=jax.ShapeDtypeStruct((batch * seq, HGRN_W), BF16),
        scratch_shapes=[pltpu.VMEM((HGRN_HEADS, HEAD_DIM, HEAD_DIM), F32),
                        pltpu.VMEM((tt, HGRN_W), F32),
                        pltpu.VMEM((tt // HGRN_SUB, HEAD_DIM, HEAD_DIM), BF16)],
        compiler_params=_cparams(("parallel", "arbitrary")),
    )(pa, pa, pa, pa, lb_logits, out_norm.reshape(1, HEAD_DIM))


def _memkv_kernel(mem_ref, g_ref, w_ref, kt_ref, v_ref):
    h = _rms(mem_ref[0], g_ref[...]).astype(BF16)
    kv = _dot(h, w_ref[...])
    kt_ref[0] = kv[:, 0:MEM_W].T.astype(BF16)
    v_ref[0] = kv[:, MEM_W:].astype(BF16)


def _memkv(mem, g, w):
    b, m, d = mem.shape
    return pl.pallas_call(
        _memkv_kernel,
        name="memkv",
        grid=(b,),
        in_specs=[
            pl.BlockSpec((1, m, d), lambda i: (i, 0, 0)),
            pl.BlockSpec((1, d), lambda i: (0, 0)),
            pl.BlockSpec((d, 2 * MEM_W), lambda i: (0, 0)),
        ],
        out_specs=[
            pl.BlockSpec((1, MEM_W, m), lambda i: (i, 0, 0)),
            pl.BlockSpec((1, m, MEM_W), lambda i: (i, 0, 0)),
        ],
        out_shape=[
            jax.ShapeDtypeStruct((b, MEM_W, m), BF16),
            jax.ShapeDtypeStruct((b, m, MEM_W), BF16),
        ],
        compiler_params=_cparams(("parallel",)),
    )(mem, g.reshape(1, d), w)


def _merge_kernel(x_ref, oa_ref, ob_ref, qm_ref, g0_ref, g1_ref, g2_ref, mkt_ref, mv_ref,
                  wa_ref, wb_ref, wm_ref, wo_ref, o_ref, om_ref):
    scale = HEAD_DIM ** -0.5
    for h in range(MEM_HEADS):
        cs = slice(h * HEAD_DIM, (h + 1) * HEAD_DIM)
        s = _dot(qm_ref[:, cs], mkt_ref[0, cs, :]) * scale
        p = jnp.exp(s - jnp.max(s, axis=-1, keepdims=True))
        l = jnp.sum(p, axis=-1, keepdims=True)
        om_ref[:, cs] = (_dot(p.astype(BF16), mv_ref[0, :, cs]) / l).astype(BF16)

    y = jax.nn.sigmoid(g0_ref[...].astype(F32)) * _dot(oa_ref[...], wa_ref[...])
    y += jax.nn.sigmoid(g1_ref[...].astype(F32)) * _dot(ob_ref[...], wb_ref[...])
    y += jax.nn.sigmoid(g2_ref[...].astype(F32)) * _dot(om_ref[...], wm_ref[...])
    o_ref[...] = x_ref[...] + _dot(y.astype(BF16), wo_ref[...])


def _merge(x, oa, ob, pa, gates, mkt, mv, wa, wb, wm, wo, batch, seq):
    n, d = x.shape
    m = mv.shape[1]
    tm = _pick(seq, MERGE_TM)
    per_b = seq // tm
    qcol = (3 * MOBA_W + 4 * HGRN_W) // MEM_W
    row = lambda i: (i, 0)
    return pl.pallas_call(
        _merge_kernel,
        name="merge",
        grid=(n // tm,),
        in_specs=[
            pl.BlockSpec((tm, d), row),
            pl.BlockSpec((tm, MOBA_W), row),
            pl.BlockSpec((tm, HGRN_W), row),
            pl.BlockSpec((tm, MEM_W), lambda i: (i, qcol)),
            pl.BlockSpec((tm, d), lambda i: (i, 0)),
            pl.BlockSpec((tm, d), lambda i: (i, 1)),
            pl.BlockSpec((tm, d), lambda i: (i, 2)),
            pl.BlockSpec((1, MEM_W, m), lambda i: (i // per_b, 0, 0)),
            pl.BlockSpec((1, m, MEM_W), lambda i: (i // per_b, 0, 0)),
            _const_spec((MOBA_W, d)),
            _const_spec((HGRN_W, d)),
            _const_spec((MEM_W, d)),
            _const_spec((d, d)),
        ],
        out_specs=pl.BlockSpec((tm, d), row),
        out_shape=jax.ShapeDtypeStruct((n, d), F32),
        scratch_shapes=[pltpu.VMEM((tm, MEM_W), BF16)],
        compiler_params=_cparams(("parallel",)),
    )(x, oa, ob, pa, gates, gates, gates, mkt, mv, wa, wb, wm, wo)


def kernel(x, mem, ffn1_norm, ffn1_w1, ffn1_w3, ffn1_w2, mix_norm, w_in, hgrn_lb_logits,
           hgrn_out_norm, mem_norm, w_mem_kv, w_proj_moba, w_proj_hgrn, w_proj_mem, w_out,
           ffn2_norm, ffn2_w1, ffn2_w3, ffn2_w2, final_norm):
    batch, seq, d = x.shape
    depth = ffn1_w1.shape[0]
    assert seq % (MOBA_QTILE * MOBA_BLOCK) == 0
    assert w_in.shape[-1] == MIX_W + 3 * d
    hs = jnp.arange(1, MOBA_HEADS + 1, dtype=F32)
    slopes = jnp.tile(jnp.exp2(-8.0 * hs / MOBA_HEADS), batch)
    slopes = jnp.broadcast_to(slopes[:, None, None], (batch * MOBA_HEADS, 1, HEAD_DIM))
    mix_scale = jnp.concatenate([jnp.full((MOBA_W,), HEAD_DIM ** -0.5, F32),
                                 jnp.ones((MIX_W - MOBA_W,), F32)])
    gate_scale = jnp.ones((3 * d,), F32)

    xs = x.reshape(batch * seq, d)
    for l in range(depth):
        last = l == depth - 1
        bf = functools.partial(_cast_layer, layer=l)
        xs = _ffn(xs, ffn1_norm[l], bf(ffn1_w1), bf(ffn1_w3), bf(ffn1_w2), final_norm, final=False)

        w_mix, w_gates = bf(w_in, splits=(MIX_W, 3 * d))
        pa = _normproj(xs, mix_norm[l], w_mix, mix_scale)
        gates = _normproj(xs, mix_norm[l], w_gates, gate_scale)

        vtb, kmean = _moba_prep(pa, batch, seq)
        ob = _hgrn(pa, hgrn_lb_logits, hgrn_out_norm[l], l, batch, seq)
        oa, ob = _moba_routed_attention(pa, vtb, kmean, slopes, batch, seq, ob)
        mkt, mv = _memkv(mem, mem_norm[l], bf(w_mem_kv))
        xs = _merge(xs, oa, ob, pa, gates, mkt, mv, bf(w_proj_moba), bf(w_proj_hgrn),
                    bf(w_proj_mem), bf(w_out), batch, seq)

        xs = _ffn(xs, ffn2_norm[l], bf(ffn2_w1), bf(ffn2_w3), bf(ffn2_w2), final_norm, final=last)
    return xs.reshape(batch, seq, d)
```
